```python
import math
import jax, jax.numpy as jnp
from jax import lax
import numpy as np

D_MODEL = 1024
BATCH = 8
SEQ = 2048
DEPTH = 1
DEC_BATCH = 32
DEC_SEQ = 4
PAST_LEN = 16384
PAGE_SIZE = 128

N_META = 16
D_CONV = D_MODEL // 2
CONV_W = 31
N_HEADS = 8
HEAD_DIM = 64
D_ATTN = N_HEADS * HEAD_DIM
IDX_HEADS = 8
IDX_DIM = 64
TOP_K_MAX = 256
N_BUCKETS = 32
MAX_EXACT = N_BUCKETS // 2
REL_MAX_DIST = 128
D_FF = 2816
FFN_CONV_W = 3
Q_BLOCK = 64
EPS = 1e-6
ATTN_SCALE = HEAD_DIM ** -0.5
IDX_SCALE = (IDX_HEADS * IDX_DIM) ** -0.5
N_IN = 2 * D_CONV + 3 * D_ATTN + IDX_HEADS * IDX_DIM + IDX_DIM + IDX_HEADS + 2 * D_MODEL

kernel_name = 'hybrid_conformer_dsa_convffn_step'


def rms_norm(x, g):
    xf = x.astype(jnp.float32)
    y = xf * lax.rsqrt(jnp.mean(xf * xf, axis=-1, keepdims=True) + EPS)
    return (y * g.astype(jnp.float32)).astype(x.dtype)


def layer_norm(x, g, b):
    xf = x.astype(jnp.float32)
    mu = jnp.mean(xf, axis=-1, keepdims=True)
    var = jnp.mean(jnp.square(xf - mu), axis=-1, keepdims=True)
    y = (xf - mu) * lax.rsqrt(var + EPS)
    return (y * g.astype(jnp.float32) + b.astype(jnp.float32)).astype(x.dtype)


def causal_dwconv(x_ext, w, b):
    c = x_ext.shape[-1]
    out = lax.conv_general_dilated(x_ext, w[:, None, :].astype(x_ext.dtype), window_strides=(1,),
                                   padding='VALID', dimension_numbers=('NWC', 'WIO', 'NWC'),
                                   feature_group_count=c)
    return out + b


def split_in(p):
    sizes = [2 * D_CONV, D_ATTN, D_ATTN, D_ATTN, IDX_HEADS * IDX_DIM, IDX_DIM, IDX_HEADS, D_MODEL, D_MODEL]
    cuts = [int(c) for c in np.cumsum(sizes)[:-1]]
    return jnp.split(p, cuts, axis=-1)


def rel_bucket(rel):
    n = jnp.maximum(rel, 0)
    nf = jnp.maximum(n, 1).astype(jnp.float32)
    large = MAX_EXACT + (jnp.log(nf / MAX_EXACT) / math.log(REL_MAX_DIST / MAX_EXACT)
                         * (N_BUCKETS - MAX_EXACT)).astype(jnp.int32)
    large = jnp.minimum(large, N_BUCKETS - 1)
    return jnp.where(n < MAX_EXACT, n, large)


def indexer_topk(q_idx, w_idx, k_idx, q_pos, n_keep):
    s = jax.nn.relu(jnp.einsum('bqhd,bld->bqhl', q_idx.astype(jnp.float32), k_idx.astype(jnp.float32)))
    score = jnp.einsum('bqhl,bqh->bql', s, w_idx.astype(jnp.float32)) * IDX_SCALE
    key_pos = jnp.arange(k_idx.shape[1])
    score = jnp.where(key_pos[None, None, :] <= q_pos[None, :, None], score, -jnp.inf)
    _, idx = lax.top_k(score, n_keep)
    return idx


def sparse_attention(q, k_sel, v_sel, key_pos, q_pos, rel_bias):
    rel = q_pos[None, :, None] - key_pos
    bias = jnp.moveaxis(rel_bias[rel_bucket(rel)], -1, 2).astype(jnp.float32)
    logits = jnp.einsum('bqhd,bqkhd->bqhk', q, k_sel).astype(jnp.float32) * ATTN_SCALE + bias
    logits = jnp.where((rel >= 0)[:, :, None, :], logits, -jnp.inf)
    p = jax.nn.softmax(logits, axis=-1).astype(v_sel.dtype)
    return jnp.einsum('bqhk,bqkhd->bqhd', p, v_sel)


def take_rows(rows, idx):
    return jax.vmap(lambda r, i: r[i])(rows, idx)


def make_prompt_attend(rel_bias):
    def attend(q, k, v, qi, ki, wi):
        b, t = q.shape[:2]
        n_keep = min(TOP_K_MAX, t // 4)

        def block(args):
            qb, qib, wib, qpos = args
            idx = indexer_topk(qib, wib, ki, qpos, n_keep)
            return sparse_attention(qb, take_rows(k, idx), take_rows(v, idx), idx, qpos, rel_bias)

        meta_out = block((q[:, :N_META], qi[:, :N_META], wi[:, :N_META], jnp.arange(N_META)))
        n_tok = t - N_META
        nb = n_tok // Q_BLOCK

        def to_blocks(a):
            return jnp.moveaxis(a[:, N_META:].reshape((b, nb, Q_BLOCK) + a.shape[2:]), 1, 0)

        qpos = (N_META + jnp.arange(n_tok)).reshape(nb, Q_BLOCK)
        outs = lax.map(block, (to_blocks(q), to_blocks(qi), to_blocks(wi), qpos))
        outs = jnp.moveaxis(outs, 0, 1).reshape(b, n_tok, N_HEADS, HEAD_DIM)
        return jnp.concatenate([meta_out, outs], axis=1)
    return attend


def make_sample_attend(cache_k, cache_v, cache_ik, page_table, rel_bias):
    def attend(q, k, v, qi, ki, wi):
        b, s = q.shape[:2]
        qpos = PAST_LEN + jnp.arange(s)
        ki_past = cache_ik[page_table].reshape(b, PAST_LEN, IDX_DIM)
        ki_all = jnp.concatenate([ki_past, ki.astype(ki_past.dtype)], axis=1)
        n_keep = min(TOP_K_MAX, (PAST_LEN + s) // 4)
        idx = indexer_topk(qi, wi, ki_all, qpos, n_keep)
        is_new = (idx >= PAST_LEN)[..., None, None]
        pi = jnp.minimum(idx, PAST_LEN - 1)
        phys = jax.vmap(lambda pt, i: pt[i // PAGE_SIZE])(page_table, pi)
        off = pi % PAGE_SIZE
        ni = jnp.clip(idx - PAST_LEN, 0, s - 1)
        k_sel = jnp.where(is_new, take_rows(k, ni), cache_k[phys, off].astype(k.dtype))
        v_sel = jnp.where(is_new, take_rows(v, ni), cache_v[phys, off].astype(v.dtype))
        return sparse_attention(q, k_sel, v_sel, idx, qpos, rel_bias)
    return attend


def trunk_layer(x, conv_past, ffn_past, attend, p):
    (g_attn, w_in, conv_w, conv_b, ln_g, ln_b, w_a_out, q_g, k_g, w_b_out, w_o,
     g_ffn, w_up, fcw, fcb, w_down) = p
    b, t, _ = x.shape
    xn = rms_norm(x, g_attn)
    a_in, q, k, v, qi, ki, wi, ga, gb = split_in(xn @ w_in)
    glu = a_in[..., :D_CONV] * jax.nn.sigmoid(a_in[..., D_CONV:])
    glu_ext = jnp.concatenate([conv_past.astype(glu.dtype), glu], axis=1)
    h = layer_norm(causal_dwconv(glu_ext, conv_w, conv_b), ln_g, ln_b)
    y_a = jax.nn.silu(h) @ w_a_out
    q = rms_norm(q.reshape(b, t, N_HEADS, HEAD_DIM), q_g)
    k = rms_norm(k.reshape(b, t, N_HEADS, HEAD_DIM), k_g)
    v = v.reshape(b, t, N_HEADS, HEAD_DIM)
    qi = qi.reshape(b, t, IDX_HEADS, IDX_DIM)
    o = attend(q, k, v, qi, ki, wi)
    y_b = o.reshape(b, t, D_ATTN) @ w_b_out
    x = x + (jax.nn.sigmoid(ga) * y_a + jax.nn.sigmoid(gb) * y_b) @ w_o
    xn2 = rms_norm(x, g_ffn)
    up = xn2 @ w_up
    a_up, b_up = up[..., :D_FF], up[..., D_FF:]
    a_ext = jnp.concatenate([ffn_past.astype(a_up.dtype), a_up], axis=1)
    a_c = causal_dwconv(a_ext, fcw, fcb)
    x = x + (jax.nn.silu(a_c) * b_up) @ w_down
    return x, k, v, ki, glu_ext[:, -(CONV_W - 1):], a_ext[:, -(FFN_CONV_W - 1):]


def setup_inputs(seed: int = 0) -> dict:
    key = jax.random.key(seed)
    ks = jax.random.split(key, 32)
    n_pages = PAST_LEN // PAGE_SIZE
    n_used = DEC_BATCH * n_pages
    n_pool = n_used + n_used // 4

    def nrm(k, shape, s):
        return jax.random.normal(k, shape, jnp.float32) * s

    return {
        'x_prompt': nrm(ks[0], (BATCH, SEQ, D_MODEL), 1.0),
        'x_sample': nrm(ks[1], (DEC_BATCH, DEC_SEQ, D_MODEL), 1.0),
        'cache_k': nrm(ks[2], (DEPTH, n_pool, PAGE_SIZE, N_HEADS, HEAD_DIM), 1.0),
        'cache_v': nrm(ks[3], (DEPTH, n_pool, PAGE_SIZE, N_HEADS, HEAD_DIM), 1.0),
        'cache_idx_k': nrm(ks[4], (DEPTH, n_pool, PAGE_SIZE, IDX_DIM), 1.0),
        'page_table': jax.random.permutation(ks[5], n_pool)[:n_used].reshape(DEC_BATCH, n_pages).astype(jnp.int32),
        'state_conv': nrm(ks[6], (DEPTH, DEC_BATCH, CONV_W - 1, D_CONV), 0.5),
        'state_ffn_conv': nrm(ks[7], (DEPTH, DEC_BATCH, FFN_CONV_W - 1, D_FF), 1.0),
        'meta_tokens': nrm(ks[8], (N_META, D_MODEL), 1.0),
        'g_attn_norm': 1.0 + nrm(ks[9], (DEPTH, D_MODEL), 0.02),
        'w_in': nrm(ks[10], (DEPTH, D_MODEL, N_IN), D_MODEL ** -0.5),
        'conv_w': nrm(ks[11], (DEPTH, CONV_W, D_CONV), CONV_W ** -0.5),
        'conv_b': nrm(ks[12], (DEPTH, D_CONV), 0.02),
        'conv_ln_g': 1.0 + nrm(ks[13], (DEPTH, D_CONV), 0.02),
        'conv_ln_b': nrm(ks[14], (DEPTH, D_CONV), 0.02),
        'w_a_out': nrm(ks[15], (DEPTH, D_CONV, D_MODEL), D_CONV ** -0.5),
        'q_norm_g': 1.0 + nrm(ks[16], (DEPTH, HEAD_DIM), 0.02),
        'k_norm_g': 1.0 + nrm(ks[17], (DEPTH, HEAD_DIM), 0.02),
        'rel_bias': nrm(ks[18], (N_BUCKETS, N_HEADS), 0.5),
        'w_b_out': nrm(ks[19], (DEPTH, D_ATTN, D_MODEL), D_ATTN ** -0.5),
        'w_o': nrm(ks[20], (DEPTH, D_MODEL, D_MODEL), D_MODEL ** -0.5),
        'g_ffn_norm': 1.0 + nrm(ks[21], (DEPTH, D_MODEL), 0.02),
        'w_up': nrm(ks[22], (DEPTH, D_MODEL, 2 * D_FF), D_MODEL ** -0.5),
        'ffn_conv_w': nrm(ks[23], (DEPTH, FFN_CONV_W, D_FF), FFN_CONV_W ** -0.5),
        'ffn_conv_b': nrm(ks[24], (DEPTH, D_FF), 0.02),
        'w_down': nrm(ks[25], (DEPTH, D_FF, D_MODEL), D_FF ** -0.5),
    }


def reference(x_prompt, x_sample, cache_k, cache_v, cache_idx_k, page_table, state_conv, state_ffn_conv,
              meta_tokens, g_attn_norm, w_in, conv_w, conv_b, conv_ln_g, conv_ln_b, w_a_out,
              q_norm_g, k_norm_g, rel_bias, w_b_out, w_o, g_ffn_norm, w_up, ffn_conv_w, ffn_conv_b, w_down):
    b = x_prompt.shape[0]
    xp = jnp.concatenate([jnp.broadcast_to(meta_tokens[None].astype(x_prompt.dtype), (b, N_META, D_MODEL)),
                          x_prompt], axis=1)
    xs = x_sample
    p_k, p_v, p_ik, p_cv, p_fc = [], [], [], [], []
    s_k, s_v, s_ik, s_cv, s_fc = [], [], [], [], []
    prompt_attend = make_prompt_attend(rel_bias)
    for l in range(DEPTH):
        params = (g_attn_norm[l], w_in[l], conv_w[l], conv_b[l], conv_ln_g[l], conv_ln_b[l], w_a_out[l],
                  q_norm_g[l], k_norm_g[l], w_b_out[l], w_o[l], g_ffn_norm[l], w_up[l],
                  ffn_conv_w[l], ffn_conv_b[l], w_down[l])
        conv0 = jnp.zeros((b, CONV_W - 1, D_CONV), xp.dtype)
        ffn0 = jnp.zeros((b, FFN_CONV_W - 1, D_FF), xp.dtype)
        xp, k1, v1, ik1, cv1, fc1 = trunk_layer(xp, conv0, ffn0, prompt_attend, params)
        p_k.append(k1); p_v.append(v1); p_ik.append(ik1); p_cv.append(cv1); p_fc.append(fc1)
        sample_attend = make_sample_attend(cache_k[l], cache_v[l], cache_idx_k[l], page_table, rel_bias)
        xs, k2, v2, ik2, cv2, fc2 = trunk_layer(xs, state_conv[l], state_ffn_conv[l], sample_attend, params)
        s_k.append(k2); s_v.append(v2); s_ik.append(ik2); s_cv.append(cv2); s_fc.append(fc2)
    y_prompt = xp[:, N_META:]
    y_sample = xs
    return (y_prompt, y_sample,
            jnp.stack(p_k), jnp.stack(p_v), jnp.stack(p_ik), jnp.stack(p_cv), jnp.stack(p_fc),
            jnp.stack(s_k), jnp.stack(s_v), jnp.stack(s_ik), jnp.stack(s_cv), jnp.stack(s_fc))
```

```python
import functools
import math

import jax
import jax.numpy as jnp
from jax import lax
from jax.experimental import pallas as pl
from jax.experimental.pallas import tpu as pltpu

F32 = jnp.float32
BF16 = jnp.bfloat16
I32 = jnp.int32

D_MODEL = 1024
D_CONV = D_MODEL // 2
CONV_W = 31
N_HEADS = 8
HEAD_DIM = 64
D_ATTN = N_HEADS * HEAD_DIM
IDX_HEADS = 8
IDX_DIM = 64
TOP_K_MAX = 256
N_BUCKETS = 32
MAX_EXACT = N_BUCKETS // 2
REL_MAX_DIST = 128
D_FF = 2816
FFN_CONV_W = 3
N_META = 16
PAGE_SIZE = 128
EPS = 1e-6
ATTN_SCALE = HEAD_DIM ** -0.5
IDX_SCALE = (IDX_HEADS * IDX_DIM) ** -0.5
D_MAIN = 2 * D_CONV + 4 * D_ATTN + 2 * D_MODEL
COL_Q, COL_K, COL_V, COL_QI, COL_GA, COL_GB = 1024, 1536, 2048, 2560, 3072, 4096

LANES = 128
SUBLANES = 8
KEY_INVALID = -2 ** 31
NEG_BIG = -1e30
VMEM_LIMIT = 56 * 1024 * 1024


def _cparams(sem):
    return pltpu.CompilerParams(dimension_semantics=sem, vmem_limit_bytes=VMEM_LIMIT)


def _sigmoid(x):
    return 1.0 / (1.0 + jnp.exp(-x))


def _ceil_to(x, m):
    return -(-x // m) * m


def _in_proj_kernel(x_ref, g_ref, wm_ref, ws_ref, gqk_ref, seg_ref, om_ref, os_ref, xn_ref):
    j = pl.program_id(1)

    @pl.when(j == 0)
    def _():
        x = x_ref[...]
        ms = jnp.mean(x * x, axis=-1, keepdims=True)
        xn_ref[...] = (x * lax.rsqrt(ms + EPS) * g_ref[...]).astype(BF16)
        os_ref[...] = jnp.dot(xn_ref[...], ws_ref[...], preferred_element_type=F32)

    y = jnp.dot(xn_ref[...], wm_ref[...], preferred_element_type=F32)
    is_qk = jnp.logical_or(j == COL_Q // D_ATTN, j == COL_K // D_ATTN)

    @pl.when(is_qk)
    def _():
        y2 = y * y
        hi = y2.astype(BF16)
        lo = (y2 - hi.astype(F32)).astype(BF16)
        ms = (jnp.dot(hi, seg_ref[...], preferred_element_type=F32)
              + jnp.dot(lo, seg_ref[...], preferred_element_type=F32))
        g = gqk_ref[pl.ds(j - COL_Q // D_ATTN, 1), :]
        om_ref[...] = y * lax.rsqrt(ms + EPS) * g

    @pl.when(jnp.logical_not(is_qk))
    def _():
        om_ref[...] = y


def _in_proj(x2d, g, wm, ws, gqk, seg, tm):
    n = x2d.shape[0]
    tn = D_ATTN
    return pl.pallas_call(
        _in_proj_kernel,
        grid=(n // tm, D_MAIN // tn),
        in_specs=[
            pl.BlockSpec((tm, D_MODEL), lambda i, j: (i, 0)),
            pl.BlockSpec((1, D_MODEL), lambda i, j: (0, 0)),
            pl.BlockSpec((D_MODEL, tn), lambda i, j: (0, j)),
            pl.BlockSpec((D_MODEL, LANES), lambda i, j: (0, 0)),
            pl.BlockSpec((2, D_ATTN), lambda i, j: (0, 0)),
            pl.BlockSpec((D_ATTN, D_ATTN), lambda i, j: (0, 0)),
        ],
        out_specs=[
            pl.BlockSpec((tm, tn), lambda i, j: (i, j)),
            pl.BlockSpec((tm, LANES), lambda i, j: (i, 0)),
        ],
        out_shape=[jax.ShapeDtypeStruct((n, D_MAIN), F32), jax.ShapeDtypeStruct((n, LANES), F32)],
        scratch_shapes=[pltpu.VMEM((tm, D_MODEL), BF16)],
        compiler_params=_cparams(("parallel", "arbitrary")),
        name="in_proj",
    )(x2d, g, wm, ws, gqk, seg)


def _conv_kernel(a_ref, past_ref, cw_ref, cb_ref, lg_ref, lb_ref, wa_ref, ya_ref, tail_ref, ext_ref, h_ref,
                 *, stride, tm, pad, rc, tail_tile, tail_loc):
    t = pl.program_id(1)

    @pl.when(t == 0)
    def _():
        ext_ref[0:pad, :] = past_ref[0]

    @pl.when(t > 0)
    def _():
        ext_ref[0:pad, :] = ext_ref[tm:tm + pad, :]

    a = a_ref[...]
    ext_ref[pad:pad + tm, :] = a[:, :D_CONV] * _sigmoid(a[:, D_CONV:])

    @pl.when(t <= tail_tile)
    def _():
        tail_ref[0] = ext_ref[tail_loc:tail_loc + pad, :]

    cb = cb_ref[...]
    lg = lg_ref[...]
    lb = lb_ref[...]
    for r0 in range(0, tm, rc):
        acc = jnp.zeros((rc, D_CONV), F32) + cb
        for w in range(CONV_W):
            off = pad - (CONV_W - 1 - w) * stride + r0
            acc = acc + ext_ref[off:off + rc, :] * cw_ref[w:w + 1, :]
        mu = jnp.mean(acc, axis=-1, keepdims=True)
        d = acc - mu
        var = jnp.mean(d * d, axis=-1, keepdims=True)
        h = d * lax.rsqrt(var + EPS) * lg + lb
        h_ref[r0:r0 + rc, :] = (h * _sigmoid(h)).astype(BF16)
    ya_ref[...] = jnp.dot(h_ref[...], wa_ref[...], preferred_element_type=F32)


def _tail_position(rows_real, tm):
    tail_tile = (rows_real - 1) // tm
    return tail_tile, rows_real - tail_tile * tm


def _conv_branch(main, past, cw, cb, lg, lb, wa, *, nb, nt, tm, stride, rows_real):
    pad = past.shape[1]
    rc = 32 if tm % 32 == 0 else SUBLANES
    tail_tile, tail_loc = _tail_position(rows_real, tm)
    kern = functools.partial(_conv_kernel, stride=stride, tm=tm, pad=pad, rc=rc,
                             tail_tile=tail_tile, tail_loc=tail_loc)
    return pl.pallas_call(
        kern,
        grid=(nb, nt),
        in_specs=[
            pl.BlockSpec((tm, 2 * D_CONV), lambda b, t: (b * nt + t, 0)),
            pl.BlockSpec((1, pad, D_CONV), lambda b, t: (b, 0, 0)),
            pl.BlockSpec((32, D_CONV), lambda b, t: (0, 0)),
            pl.BlockSpec((1, D_CONV), lambda b, t: (0, 0)),
            pl.BlockSpec((1, D_CONV), lambda b, t: (0, 0)),
            pl.BlockSpec((1, D_CONV), lambda b, t: (0, 0)),
            pl.BlockSpec((D_CONV, D_MODEL), lambda b, t: (0, 0)),
        ],
        out_specs=[
            pl.BlockSpec((tm, D_MODEL), lambda b, t: (b * nt + t, 0)),
            pl.BlockSpec((1, pad, D_CONV), lambda b, t: (b, 0, 0)),
        ],
        out_shape=[jax.ShapeDtypeStruct((nb * nt * tm, D_MODEL), F32),
                   jax.ShapeDtypeStruct((nb, pad, D_CONV), F32)],
        scratch_shapes=[pltpu.VMEM((pad + tm, D_CONV), F32), pltpu.VMEM((tm, D_CONV), BF16)],
        compiler_params=_cparams(("parallel", "arbitrary")),
        name="conv_branch",
    )(main, past, cw, cb, lg, lb, wa)


def _rel_bucket(rel):
    n = jnp.maximum(rel, 0)
    nf = jnp.maximum(n, 1).astype(F32)
    large = MAX_EXACT + (jnp.log(nf / MAX_EXACT) / math.log(REL_MAX_DIST / MAX_EXACT)
                         * (N_BUCKETS - MAX_EXACT)).astype(I32)
    large = jnp.minimum(large, N_BUCKETS - 1)
    return jnp.where(n < MAX_EXACT, n, large)


def _bias_tile(rel, rb_rows):
    bucket = _rel_bucket(rel)
    out = jnp.zeros(rel.shape, F32)
    for b in range(N_BUCKETS):
        out = jnp.where(bucket == b, rb_rows[:, b:b + 1], out)
    return out


def _prompt_bias_kernel(rbt_ref, o_ref):
    row = lax.broadcasted_iota(I32, (LANES, LANES), 0)
    col = lax.broadcasted_iota(I32, (LANES, LANES), 1)
    for h in range(N_HEADS):
        for d in range(3):
            o_ref[h * 3 + d] = _bias_tile(d * LANES + row - col, rbt_ref[h:h + 1, :])


def _prompt_bias(rbt):
    return pl.pallas_call(
        _prompt_bias_kernel,
        out_shape=jax.ShapeDtypeStruct((N_HEADS * 3, LANES, LANES), F32),
        name="prompt_bias",
    )(rbt)


def _sample_bias_kernel(rbt_ref, o_ref, *, n_q):
    rows = n_q * N_HEADS
    q = lax.broadcasted_iota(I32, (rows, LANES), 0) // N_HEADS
    col = lax.broadcasted_iota(I32, (rows, LANES), 1)
    rb = rbt_ref[...]
    o_ref[0] = _bias_tile(PAGE_SIZE + q - col, rb)
    o_ref[1] = _bias_tile(q - col, rb)
    o_ref[2] = _bias_tile(jnp.full((rows, LANES), REL_MAX_DIST, I32), rb)


def _sample_bias(rbt_rows, n_q):
    rows = n_q * N_HEADS
    return pl.pallas_call(
        functools.partial(_sample_bias_kernel, n_q=n_q),
        out_shape=jax.ShapeDtypeStruct((3, rows, LANES), F32),
        name="sample_bias",
    )(rbt_rows)


def _score_to_key(s, valid):
    bits = pltpu.bitcast(s, I32)
    key = bits ^ ((bits >> 31) & 0x7FFFFFFF)
    key = jnp.where(bits == KEY_INVALID, 0, key)
    return jnp.where(valid, key, KEY_INVALID)


def _threshold_search(count_ge, n_keep, shape):
    cnt_valid = count_ge(jnp.full(shape, KEY_INVALID + 1, I32))
    c0 = count_ge(jnp.zeros(shape, I32))
    ok0 = c0 >= n_keep
    base = jnp.where(ok0, 0, KEY_INVALID).astype(I32)
    cnt = jnp.where(ok0, c0, cnt_valid)

    def body(t, carry):
        base, cnt = carry
        cand = base | lax.shift_left(jnp.int32(1), 30 - t)
        c = count_ge(cand)
        ok = c >= n_keep
        return jnp.where(ok, cand, base), jnp.where(ok, c, cnt)

    base, cnt = lax.fori_loop(0, 31, body, (base, cnt))
    return jnp.maximum(base, KEY_INVALID + 1), cnt


def _pattn_kernel(q_ref, kt_ref, v_ref, qi_ref, kit_ref, wi_ref, bias_ref, tri_ref, o_ref,
                  keys_ref, m_ref, l_ref, acc_ref, *, n_keep):
    i = pl.program_id(1)
    tq = LANES
    shape = (tq, LANES)
    nchunks = i + 1
    row = lax.broadcasted_iota(I32, shape, 0)
    col = lax.broadcasted_iota(I32, shape, 1)

    wi = wi_ref[0] * IDX_SCALE
    wcols = [jnp.broadcast_to(wi[:, h:h + 1], shape) for h in range(IDX_HEADS)]

    def idx_body(j, carry):
        off = pl.multiple_of(j * LANES, LANES)
        kic = kit_ref[0, :, pl.ds(off, LANES)]
        s = jnp.zeros(shape, F32)
        for h in range(IDX_HEADS):
            s = s + wcols[h] * jnp.maximum(jnp.dot(qi_ref[0, h], kic, preferred_element_type=F32), 0.0)
        valid = jnp.logical_or(j < i, col <= row)
        keys_ref[:, pl.ds(off, LANES)] = _score_to_key(s, valid)
        return carry

    lax.fori_loop(0, nchunks, idx_body, 0)

    def count_ge(cand):
        def body(j, acc):
            off = pl.multiple_of(j * LANES, LANES)
            return acc + jnp.where(keys_ref[:, pl.ds(off, LANES)] >= cand, 1.0, 0.0)
        acc = lax.fori_loop(0, nchunks, body, jnp.zeros(shape, F32))
        return jnp.broadcast_to(jnp.sum(acc, axis=1, keepdims=True), shape)

    base, cnt = _threshold_search(count_ge, float(n_keep), shape)

    @pl.when(jnp.max(cnt) > float(n_keep))
    def _():
        need = float(n_keep) - count_ge(base + 1)
        tie_row = cnt > float(n_keep)

        def tie_body(j, carry):
            off = pl.multiple_of(j * LANES, LANES)
            k = keys_ref[:, pl.ds(off, LANES)]
            eq = jnp.logical_and(k == base, tie_row)
            eqf = jnp.where(eq, 1.0, 0.0)
            pref = carry + jnp.dot(eqf.astype(BF16), tri_ref[...], preferred_element_type=F32)
            drop = jnp.logical_and(eq, pref > need)
            keys_ref[:, pl.ds(off, LANES)] = jnp.where(drop, KEY_INVALID, k)
            return carry + jnp.broadcast_to(jnp.sum(eqf, axis=1, keepdims=True), shape)

        lax.fori_loop(0, nchunks, tie_body, jnp.zeros(shape, F32))

    m_ref[...] = jnp.full(m_ref.shape, NEG_BIG, F32)
    l_ref[...] = jnp.zeros(l_ref.shape, F32)
    acc_ref[...] = jnp.zeros(acc_ref.shape, F32)

    def att_body(j, carry):
        off = pl.multiple_of(j * LANES, LANES)
        sel = keys_ref[:, pl.ds(off, LANES)] >= base
        d = jnp.minimum(i - j, 2)
        for h in range(N_HEADS):
            s = jnp.dot(q_ref[0, h], kt_ref[0, h, :, pl.ds(off, LANES)], preferred_element_type=F32)
            s = jnp.where(sel, s + bias_ref[h * 3 + d], NEG_BIG)
            m_old = m_ref[h]
            m_new = jnp.maximum(m_old, jnp.max(s, axis=1, keepdims=True))
            p = jnp.exp(s - m_new)
            alpha = jnp.exp(m_old - m_new)
            l_ref[h] = alpha * l_ref[h] + jnp.sum(p, axis=1, keepdims=True)
            pv = jnp.dot(p.astype(BF16), v_ref[0, h, pl.ds(off, LANES), :], preferred_element_type=F32)
            acc_ref[h] = alpha * acc_ref[h] + pv
            m_ref[h] = m_new
        return carry

    lax.fori_loop(0, nchunks, att_body, 0)
    for h in range(N_HEADS):
        o_ref[0, h] = acc_ref[h] / l_ref[h]


def _prompt_attention(qh, kt, vh, qih, kit, wi, bias, tri, n_keep):
    nb, _, tp, _ = qh.shape
    tq = LANES
    nq = tp // tq
    return pl.pallas_call(
        functools.partial(_pattn_kernel, n_keep=n_keep),
        grid=(nb, nq),
        in_specs=[
            pl.BlockSpec((1, N_HEADS, tq, HEAD_DIM), lambda b, i: (b, 0, i, 0)),
            pl.BlockSpec((1, N_HEADS, HEAD_DIM, tp), lambda b, i: (b, 0, 0, 0)),
            pl.BlockSpec((1, N_HEADS, tp, HEAD_DIM), lambda b, i: (b, 0, 0, 0)),
            pl.BlockSpec((1, IDX_HEADS, tq, IDX_DIM), lambda b, i: (b, 0, i, 0)),
            pl.BlockSpec((1, IDX_DIM, tp), lambda b, i: (b, 0, 0)),
            pl.BlockSpec((1, tq, IDX_HEADS), lambda b, i: (b, i, 0)),
            pl.BlockSpec((N_HEADS * 3, LANES, LANES), lambda b, i: (0, 0, 0)),
            pl.BlockSpec((LANES, LANES), lambda b, i: (0, 0)),
        ],
        out_specs=pl.BlockSpec((1, N_HEADS, tq, HEAD_DIM), lambda b, i: (b, 0, i, 0)),
        out_shape=jax.ShapeDtypeStruct((nb, N_HEADS, tp, HEAD_DIM), F32),
        scratch_shapes=[
            pltpu.VMEM((tq, tp), I32),
            pltpu.VMEM((N_HEADS, tq, 1), F32),
            pltpu.VMEM((N_HEADS, tq, 1), F32),
            pltpu.VMEM((N_HEADS, tq, HEAD_DIM), F32),
        ],
        compiler_params=_cparams(("parallel", "arbitrary")),
        name="prompt_attention",
    )(qh, kt, vh, qih, kit, wi, bias, tri)


def _sidx_kernel(pt_ref, qi_ref, wi_ref, kin_ref, tri_ref, cache_ref, sel_ref, kibuf, sem, keys_ref,
                 *, n_keep, n_pages, n_q, pages_per_chunk):
    b = pl.program_id(0)
    nb = pl.num_programs(0)
    past = n_pages * PAGE_SIZE
    lp = past + LANES
    slot = b % 2

    def page_copy(bb, p, sl):
        return pltpu.make_async_copy(cache_ref.at[pt_ref[bb, p]], kibuf.at[sl, p], sem.at[sl])

    def start_all(bb, sl):
        def body(p, c):
            page_copy(bb, p, sl).start()
            return c
        lax.fori_loop(0, n_pages, body, 0)

    def wait_all(bb, sl):
        def body(p, c):
            page_copy(bb, p, sl).wait()
            return c
        lax.fori_loop(0, n_pages, body, 0)

    @pl.when(b == 0)
    def _():
        start_all(b, slot)

    @pl.when(b + 1 < nb)
    def _():
        start_all(b + 1, 1 - slot)

    wait_all(b, slot)

    rows = n_q * IDX_HEADS
    qi = qi_ref[0]
    w = wi_ref[0] * IDX_SCALE
    ck = pages_per_chunk * PAGE_SIZE
    keys_ref[...] = jnp.full(keys_ref.shape, KEY_INVALID, I32)

    def score_rows(kc):
        s = lax.dot_general(qi, kc, (((1,), (1,)), ((), ())), preferred_element_type=F32)
        s = jnp.maximum(s, 0.0) * w
        return jnp.sum(s.reshape(n_q, IDX_HEADS, s.shape[-1]), axis=1)

    def chunk_body(c, carry):
        p0 = pl.multiple_of(c * pages_per_chunk, pages_per_chunk)
        kc = kibuf[slot, pl.ds(p0, pages_per_chunk)].reshape(ck, IDX_DIM).astype(BF16)
        s = score_rows(kc)
        off = pl.multiple_of(c * ck, ck)
        keys_ref[0:n_q, pl.ds(off, ck)] = _score_to_key(s, jnp.full(s.shape, True))
        return carry

    lax.fori_loop(0, n_pages // pages_per_chunk, chunk_body, 0)
    s_new = score_rows(kin_ref[0])
    qrow = lax.broadcasted_iota(I32, (n_q, LANES), 0)
    jcol = lax.broadcasted_iota(I32, (n_q, LANES), 1)
    keys_ref[0:n_q, past:lp] = _score_to_key(s_new, jcol <= qrow)

    shape = (SUBLANES, LANES)
    nl = lp // LANES

    def count_ge(cand):
        def body(j, acc):
            off = pl.multiple_of(j * LANES, LANES)
            return acc + jnp.where(keys_ref[:, pl.ds(off, LANES)] >= cand, 1.0, 0.0)
        acc = lax.fori_loop(0, nl, body, jnp.zeros(shape, F32))
        return jnp.broadcast_to(jnp.sum(acc, axis=1, keepdims=True), shape)

    base, cnt = _threshold_search(count_ge, float(n_keep), shape)

    @pl.when(jnp.max(cnt) > float(n_keep))
    def _():
        need = float(n_keep) - count_ge(base + 1)
        tie_row = cnt > float(n_keep)

        def tie_body(j, carry):
            off = pl.multiple_of(j * LANES, LANES)
            k = keys_ref[:, pl.ds(off, LANES)]
            eq = jnp.logical_and(k == base, tie_row)
            eqf = jnp.where(eq, 1.0, 0.0)
            pref = carry + jnp.dot(eqf.astype(BF16), tri_ref[...], preferred_element_type=F32)
            drop = jnp.logical_and(eq, pref > need)
            keys_ref[:, pl.ds(off, LANES)] = jnp.where(drop, KEY_INVALID, k)
            return carry + jnp.broadcast_to(jnp.sum(eqf, axis=1, keepdims=True), shape)

        lax.fori_loop(0, nl, tie_body, jnp.zeros(shape, F32))

    def out_body(j, carry):
        off = pl.multiple_of(j * LANES, LANES)
        sel_ref[0, :, pl.ds(off, LANES)] = jnp.where(keys_ref[:, pl.ds(off, LANES)] >= base, 1.0, 0.0)
        return carry

    lax.fori_loop(0, nl, out_body, 0)


def _sample_index(page_table, qi_rows, wi_rows, ki_new, tri, cache_ik, n_keep):
    nb, n_pages = page_table.shape
    rows = qi_rows.shape[1]
    n_q = rows // IDX_HEADS
    lp = n_pages * PAGE_SIZE + LANES
    pages_per_chunk = 8 if n_pages % 8 == 0 else 1
    kern = functools.partial(_sidx_kernel, n_keep=n_keep, n_pages=n_pages, n_q=n_q,
                             pages_per_chunk=pages_per_chunk)
    return pl.pallas_call(
        kern,
        grid_spec=pltpu.PrefetchScalarGridSpec(
            num_scalar_prefetch=1,
            grid=(nb,),
            in_specs=[
                pl.BlockSpec((1, rows, IDX_DIM), lambda b, pt: (b, 0, 0)),
                pl.BlockSpec((1, rows, 1), lambda b, pt: (b, 0, 0)),
                pl.BlockSpec((1, LANES, IDX_DIM), lambda b, pt: (b, 0, 0)),
                pl.BlockSpec((LANES, LANES), lambda b, pt: (0, 0)),
                pl.BlockSpec(memory_space=pl.ANY),
            ],
            out_specs=pl.BlockSpec((1, SUBLANES, lp), lambda b, pt: (b, 0, 0)),
            scratch_shapes=[
                pltpu.VMEM((2, n_pages, PAGE_SIZE, IDX_DIM), F32),
                pltpu.SemaphoreType.DMA((2,)),
                pltpu.VMEM((SUBLANES, lp), I32),
            ],
        ),
        out_shape=jax.ShapeDtypeStruct((nb, SUBLANES, lp), F32),
        compiler_params=_cparams(("arbitrary",)),
        name="sample_index",
    )(page_table, qi_rows, wi_rows, ki_new, tri, cache_ik)


def _sattn_kernel(pt_ref, qbd_ref, sel_ref, seln_ref, kn_ref, vn_ref, bias_ref, hsel_ref, ck_ref, cv_ref, o_ref,
                  kbuf, vbuf, sem, bias_buf, m_ref, l_ref, acc_ref, *, n_q, n_chunks, ppc):
    b = pl.program_id(0)
    c = pl.program_id(1)
    nb = pl.num_programs(0)
    step = b * n_chunks + c
    slot = step % 2
    rows = n_q * N_HEADS
    ck = ppc * PAGE_SIZE

    def copies(bb, cc, sl, p):
        page = pt_ref[bb, cc * ppc + p]
        return (pltpu.make_async_copy(ck_ref.at[page], kbuf.at[sl, p], sem.at[0, sl]),
                pltpu.make_async_copy(cv_ref.at[page], vbuf.at[sl, p], sem.at[1, sl]))

    def start_all(bb, cc, sl):
        def body(p, carry):
            kc, vc = copies(bb, cc, sl, p)
            kc.start()
            vc.start()
            return carry
        lax.fori_loop(0, ppc, body, 0)

    def wait_all(bb, cc, sl):
        def body(p, carry):
            kc, vc = copies(bb, cc, sl, p)
            kc.wait()
            vc.wait()
            return carry
        lax.fori_loop(0, ppc, body, 0)

    @pl.when(step == 0)
    def _():
        start_all(b, c, slot)

    @pl.when(step + 1 < nb * n_chunks)
    def _():
        nxt = step + 1
        start_all(nxt // n_chunks, nxt % n_chunks, 1 - slot)

    @pl.when(c == 0)
    def _():
        m_ref[...] = jnp.full(m_ref.shape, NEG_BIG, F32)
        l_ref[...] = jnp.zeros(l_ref.shape, F32)
        acc_ref[...] = jnp.zeros(acc_ref.shape, F32)
        bias_buf[...] = jnp.broadcast_to(bias_ref[2][:, 0:1], bias_buf.shape)

    @pl.when(c == n_chunks - 1)
    def _():
        bias_buf[:, ck - PAGE_SIZE:ck] = bias_ref[0]

    wait_all(b, c, slot)
    qbd = qbd_ref[0]

    def expand(sel):
        n = sel.shape[-1]
        return jnp.broadcast_to(sel[:, None, :], (n_q, N_HEADS, n)).reshape(rows, n)

    def update(s, sel, v):
        s = jnp.where(expand(sel) > 0.5, s, NEG_BIG)
        m_old = m_ref[...]
        m_new = jnp.maximum(m_old, jnp.max(s, axis=1, keepdims=True))
        p = jnp.exp(s - m_new)
        alpha = jnp.exp(m_old - m_new)
        l_ref[...] = alpha * l_ref[...] + jnp.sum(p, axis=1, keepdims=True)
        acc_ref[...] = alpha * acc_ref[...] + jnp.dot(p.astype(BF16), v, preferred_element_type=F32)
        m_ref[...] = m_new

    kc = kbuf[slot].reshape(ck, D_ATTN).astype(BF16)
    vc = vbuf[slot].reshape(ck, D_ATTN).astype(BF16)
    s = lax.dot_general(qbd, kc, (((1,), (1,)), ((), ())), preferred_element_type=F32) + bias_buf[...]
    update(s, sel_ref[0, 0:n_q, :], vc)

    @pl.when(c == n_chunks - 1)
    def _():
        s_new = lax.dot_general(qbd, kn_ref[0], (((1,), (1,)), ((), ())), preferred_element_type=F32)
        update(s_new + bias_ref[1], seln_ref[0, 0:n_q, :], vn_ref[0])
        full = acc_ref[...] / l_ref[...] * hsel_ref[...]
        o_ref[0] = jnp.sum(full.reshape(n_q, N_HEADS, D_ATTN), axis=1)


def _sample_attention(page_table, qbd, sel, k_new, v_new, sbias, hsel, cache_k, cache_v):
    nb, n_pages = page_table.shape
    rows = qbd.shape[1]
    n_q = rows // N_HEADS
    ppc = 16 if n_pages % 16 == 0 else 1
    n_chunks = n_pages // ppc
    ck = ppc * PAGE_SIZE
    kern = functools.partial(_sattn_kernel, n_q=n_q, n_chunks=n_chunks, ppc=ppc)
    return pl.pallas_call(
        kern,
        grid_spec=pltpu.PrefetchScalarGridSpec(
            num_scalar_prefetch=1,
            grid=(nb, n_chunks),
            in_specs=[
                pl.BlockSpec((1, rows, D_ATTN), lambda b, c, pt: (b, 0, 0)),
                pl.BlockSpec((1, SUBLANES, ck), lambda b, c, pt: (b, 0, c)),
                pl.BlockSpec((1, SUBLANES, LANES), lambda b, c, pt: (b, 0, n_chunks * ppc)),
                pl.BlockSpec((1, LANES, D_ATTN), lambda b, c, pt: (b, 0, 0)),
                pl.BlockSpec((1, LANES, D_ATTN), lambda b, c, pt: (b, 0, 0)),
                pl.BlockSpec((3, rows, LANES), lambda b, c, pt: (0, 0, 0)),
                pl.BlockSpec((rows, D_ATTN), lambda b, c, pt: (0, 0)),
                pl.BlockSpec(memory_space=pl.ANY),
                pl.BlockSpec(memory_space=pl.ANY),
            ],
            out_specs=pl.BlockSpec((1, n_q, D_ATTN), lambda b, c, pt: (b, 0, 0)),
            scratch_shapes=[
                pltpu.VMEM((2, ppc, PAGE_SIZE, D_ATTN), F32),
                pltpu.VMEM((2, ppc, PAGE_SIZE, D_ATTN), F32),
                pltpu.SemaphoreType.DMA((2, 2)),
                pltpu.VMEM((rows, ck), F32),
                pltpu.VMEM((rows, 1), F32),
                pltpu.VMEM((rows, 1), F32),
                pltpu.VMEM((rows, D_ATTN), F32),
            ],
        ),
        out_shape=jax.ShapeDtypeStruct((nb, n_q, D_ATTN), F32),
        compiler_params=_cparams(("arbitrary", "arbitrary")),
        name="sample_attention",
    )(page_table, qbd, sel, sel, k_new, v_new, sbias, hsel, cache_k, cache_v)


def _merge_kernel(o_ref, ya_ref, ga_ref, gb_ref, x_ref, wb_ref, wo_ref, x1_ref):
    yb = jnp.dot(o_ref[...].astype(BF16), wb_ref[...], preferred_element_type=F32)
    m = _sigmoid(ga_ref[...]) * ya_ref[...] + _sigmoid(gb_ref[...]) * yb
    x1_ref[...] = x_ref[...] + jnp.dot(m.astype(BF16), wo_ref[...], preferred_element_type=F32)


def _merge(o2d, ya, main, x2d, wb, wo, tm):
    n = x2d.shape[0]
    return pl.pallas_call(
        _merge_kernel,
        grid=(n // tm,),
        in_specs=[
            pl.BlockSpec((tm, D_ATTN), lambda i: (i, 0)),
            pl.BlockSpec((tm, D_MODEL), lambda i: (i, 0)),
            pl.BlockSpec((tm, D_MODEL), lambda i: (i, COL_GA // D_MODEL)),
            pl.BlockSpec((tm, D_MODEL), lambda i: (i, COL_GB // D_MODEL)),
            pl.BlockSpec((tm, D_MODEL), lambda i: (i, 0)),
            pl.BlockSpec((D_ATTN, D_MODEL), lambda i: (0, 0)),
            pl.BlockSpec((D_MODEL, D_MODEL), lambda i: (0, 0)),
        ],
        out_specs=pl.BlockSpec((tm, D_MODEL), lambda i: (i, 0)),
        out_shape=jax.ShapeDtypeStruct((n, D_MODEL), F32),
        compiler_params=_cparams(("parallel",)),
        name="merge_out_proj",
    )(o2d, ya, main, main, x2d, wb, wo)


def _ffn_kernel(x_ref, g_ref, wua_ref, wub_ref, fcw_ref, fcb_ref, past_ref, wd_ref, y_ref, tail_ref,
                xn_ref, carry_ref, ext_ref, acc_ref, *, stride, tm, pad, tail_loc):
    t = pl.program_id(1)
    f = pl.program_id(2)
    nf = pl.num_programs(2)

    @pl.when(f == 0)
    def _():
        x = x_ref[...]
        ms = jnp.mean(x * x, axis=-1, keepdims=True)
        xn_ref[...] = (x * lax.rsqrt(ms + EPS) * g_ref[...]).astype(BF16)
        acc_ref[...] = jnp.zeros(acc_ref.shape, F32)

    a = jnp.dot(xn_ref[...], wua_ref[...], preferred_element_type=F32)
    bq = jnp.dot(xn_ref[...], wub_ref[...], preferred_element_type=F32)

    @pl.when(t == 0)
    def _():
        ext_ref[0:pad, :] = past_ref[0]

    @pl.when(t > 0)
    def _():
        ext_ref[0:pad, :] = carry_ref[f]

    ext_ref[pad:pad + tm, :] = a
    carry_ref[f] = ext_ref[tm:tm + pad, :]
    tail_ref[0, 0] = ext_ref[tail_loc:tail_loc + pad, :]

    conv = (ext_ref[pad - 2 * stride:pad - 2 * stride + tm, :] * fcw_ref[0:1, :]
            + ext_ref[pad - stride:pad - stride + tm, :] * fcw_ref[1:2, :]
            + a * fcw_ref[2:3, :] + fcb_ref[...])
    h = conv * _sigmoid(conv) * bq
    acc_ref[...] += jnp.dot(h.astype(BF16), wd_ref[...], preferred_element_type=F32)

    @pl.when(f == nf - 1)
    def _():
        y_ref[...] = x_ref[...] + acc_ref[...]


def _ffn(x1, g, wup, fcw, fcb, past, wd, *, nb, nt, tm, stride, tf, rows_real):
    pad = past.shape[1]
    nf = D_FF // tf
    tail_tile, tail_loc = _tail_position(rows_real, tm)
    kern = functools.partial(_ffn_kernel, stride=stride, tm=tm, pad=pad, tail_loc=tail_loc)
    y, tail = pl.pallas_call(
        kern,
        grid=(nb, nt, nf),
        in_specs=[
            pl.BlockSpec((tm, D_MODEL), lambda b, t, f: (b * nt + t, 0)),
            pl.BlockSpec((1, D_MODEL), lambda b, t, f: (0, 0)),
            pl.BlockSpec((D_MODEL, tf), lambda b, t, f: (0, f)),
            pl.BlockSpec((D_MODEL, tf), lambda b, t, f: (0, nf + f)),
            pl.BlockSpec((SUBLANES, tf), lambda b, t, f: (0, f)),
            pl.BlockSpec((1, tf), lambda b, t, f: (0, f)),
            pl.BlockSpec((1, pad, tf), lambda b, t, f: (b, 0, f)),
            pl.BlockSpec((tf, D_MODEL), lambda b, t, f: (f, 0)),
        ],
        out_specs=[
            pl.BlockSpec((tm, D_MODEL), lambda b, t, f: (b * nt + t, 0)),
            pl.BlockSpec((1, 1, pad, tf), lambda b, t, f: (b, t, 0, f)),
        ],
        out_shape=[jax.ShapeDtypeStruct((nb * nt * tm, D_MODEL), F32),
                   jax.ShapeDtypeStruct((nb, nt, pad, D_FF), F32)],
        scratch_shapes=[
            pltpu.VMEM((tm, D_MODEL), BF16),
            pltpu.VMEM((nf, pad, tf), F32),
            pltpu.VMEM((pad + tm, tf), F32),
            pltpu.VMEM((tm, D_MODEL), F32),
        ],
        compiler_params=_cparams(("parallel", "arbitrary", "arbitrary")),
        name="conv_ffn",
    )(x1, g, wup, wup, fcw, fcb, past, wd)
    return y, tail[:, tail_tile]


def _row_tile(n, cap):
    best = SUBLANES
    for cand in range(SUBLANES, cap + 1, SUBLANES):
        if n % cand == 0:
            best = cand
    return best


def kernel(x_prompt, x_sample, cache_k, cache_v, cache_idx_k, page_table, state_conv, state_ffn_conv, meta_tokens, g_attn_norm, w_in, conv_w, conv_b, conv_ln_g, conv_ln_b, w_a_out, q_norm_g, k_norm_g, rel_bias, w_b_out, w_o, g_ffn_norm, w_up, ffn_conv_w, ffn_conv_b, w_down):
    nbp, seq, _ = x_prompt.shape
    nbs, n_q, _ = x_sample.shape
    n_pages = page_table.shape[1]
    past_len = n_pages * PAGE_SIZE
    n_pool = cache_k.shape[1]
    t_real = seq + N_META
    tp = _ceil_to(t_real, LANES)
    keep_p = min(TOP_K_MAX, t_real // 4)
    keep_s = min(TOP_K_MAX, (past_len + n_q) // 4)
    assert g_attn_norm.shape[0] == 1, "single trunk layer"

    w = w_in[0]
    c_a, c_q, c_k, c_v, c_qi = 0, 1024, 1536, 2048, 2560
    c_ki, c_wi, c_ga, c_gb = 3072, 3136, 3144, 4168
    wm = jnp.concatenate([w[:, c_a:c_ki], w[:, c_ga:]], axis=1).astype(BF16)
    ws = jnp.concatenate([w[:, c_ki:c_ga], jnp.zeros((D_MODEL, LANES - IDX_DIM - IDX_HEADS), F32)],
                         axis=1).astype(BF16)
    g_attn = g_attn_norm[0][None, :]
    gqk = jnp.stack([jnp.tile(q_norm_g[0], N_HEADS) * ATTN_SCALE, jnp.tile(k_norm_g[0], N_HEADS)])
    hid = jnp.arange(D_ATTN) // HEAD_DIM
    seg = jnp.where(hid[:, None] == hid[None, :], 1.0 / HEAD_DIM, 0.0).astype(BF16)
    cw = jnp.concatenate([conv_w[0], jnp.zeros((32 - CONV_W, D_CONV), F32)], axis=0)
    cb, lg, lb = conv_b[0][None, :], conv_ln_g[0][None, :], conv_ln_b[0][None, :]
    wa = w_a_out[0].astype(BF16)
    wb = w_b_out[0].astype(BF16)
    wo = w_o[0].astype(BF16)
    g_ffn = g_ffn_norm[0][None, :]
    wup = w_up[0].astype(BF16)
    fcw = jnp.concatenate([ffn_conv_w[0], jnp.zeros((SUBLANES - FFN_CONV_W, D_FF), F32)], axis=0)
    fcb = ffn_conv_b[0][None, :]
    wd = w_down[0].astype(BF16)
    rbt = rel_bias.T
    ar = jnp.arange(LANES)
    tri = (ar[:, None] <= ar[None, :]).astype(BF16)

    xp = jnp.concatenate([jnp.broadcast_to(meta_tokens[None], (nbp, N_META, D_MODEL)), x_prompt,
                          jnp.zeros((nbp, tp - t_real, D_MODEL), F32)], axis=1).reshape(nbp * tp, D_MODEL)
    n_p = nbp * tp
    main_p, small_p = _in_proj(xp, g_attn, wm, ws, gqk, seg, _row_tile(n_p, 1024))

    tm_p = _row_tile(tp, 640)
    nt_p = tp // tm_p
    ya_p, ctail_p = _conv_branch(main_p, jnp.zeros((nbp, 32, D_CONV), F32), cw, cb, lg, lb, wa,
                                 nb=nbp, nt=nt_p, tm=tm_p, stride=1, rows_real=t_real)

    def heads(col):
        return main_p[:, col:col + D_ATTN].reshape(nbp, tp, N_HEADS, HEAD_DIM)

    qh = heads(COL_Q).transpose(0, 2, 1, 3).astype(BF16)
    k_p = heads(COL_K)
    v_p = heads(COL_V)
    kt = k_p.transpose(0, 2, 3, 1).astype(BF16)
    vh = v_p.transpose(0, 2, 1, 3).astype(BF16)
    qih = heads(COL_QI).transpose(0, 2, 1, 3).astype(BF16)
    small_p3 = small_p.reshape(nbp, tp, LANES)
    ki_p = small_p3[:, :, :IDX_DIM]
    kit = ki_p.transpose(0, 2, 1).astype(BF16)
    wi_p = small_p3[:, :, IDX_DIM:IDX_DIM + IDX_HEADS]
    o_p = _prompt_attention(qh, kt, vh, qih, kit, wi_p, _prompt_bias(rbt), tri, keep_p)
    o_p2d = o_p.transpose(0, 2, 1, 3).reshape(n_p, D_ATTN)

    x1_p = _merge(o_p2d, ya_p, main_p, xp, wb, wo, _row_tile(n_p, 512))
    y_p, ftail_p = _ffn(x1_p, g_ffn, wup, fcw, fcb, jnp.zeros((nbp, SUBLANES, D_FF), F32), wd,
                        nb=nbp, nt=nt_p, tm=tm_p, stride=1, tf=256, rows_real=t_real)

    y_prompt = y_p.reshape(nbp, tp, D_MODEL)[:, N_META:t_real]
    p_k = k_p[None, :, :t_real]
    p_v = v_p[None, :, :t_real]
    p_ik = ki_p[None, :, :t_real]
    p_cv = ctail_p[None, :, 32 - (CONV_W - 1):]
    p_fc = ftail_p[None, :, SUBLANES - (FFN_CONV_W - 1):]

    n_s = nbs * n_q
    xs = x_sample.transpose(1, 0, 2).reshape(n_s, D_MODEL)
    main_s, small_s = _in_proj(xs, g_attn, wm, ws, gqk, seg, _row_tile(n_s, 1024))

    pad_c = _ceil_to((CONV_W - 1) * nbs, SUBLANES)
    past_c = state_conv[0].transpose(1, 0, 2).reshape(1, (CONV_W - 1) * nbs, D_CONV)
    past_c = jnp.pad(past_c, ((0, 0), (pad_c - (CONV_W - 1) * nbs, 0), (0, 0)))
    ya_s, ctail_s = _conv_branch(main_s, past_c, cw, cb, lg, lb, wa, nb=1, nt=1, tm=n_s, stride=nbs,
                                 rows_real=n_s)

    def heads_s(col):
        return main_s[:, col:col + D_ATTN].reshape(n_q, nbs, N_HEADS, HEAD_DIM).transpose(1, 0, 2, 3)

    q_s = heads_s(COL_Q)
    k_s = heads_s(COL_K)
    v_s = heads_s(COL_V)
    qi_s = heads_s(COL_QI)
    small_s3 = small_s.reshape(n_q, nbs, LANES).transpose(1, 0, 2)
    ki_s = small_s3[:, :, :IDX_DIM]
    wi_s = small_s3[:, :, IDX_DIM:IDX_DIM + IDX_HEADS]
    rows = n_q * N_HEADS

    qi_rows = qi_s.reshape(nbs, rows, IDX_DIM).astype(BF16)
    wi_rows = wi_s.reshape(nbs, rows, 1)
    ki_new = jnp.pad(ki_s, ((0, 0), (0, LANES - n_q), (0, 0))).astype(BF16)
    sel = _sample_index(page_table, qi_rows, wi_rows, ki_new, tri,
                        cache_idx_k[0], keep_s)

    eye = jnp.eye(N_HEADS, dtype=F32)
    qbd = (q_s[:, :, :, None, :] * eye[None, None, :, :, None]).reshape(nbs, rows, D_ATTN).astype(BF16)
    k_new = jnp.pad(k_s.reshape(nbs, n_q, D_ATTN), ((0, 0), (0, LANES - n_q), (0, 0))).astype(BF16)
    v_new = jnp.pad(v_s.reshape(nbs, n_q, D_ATTN), ((0, 0), (0, LANES - n_q), (0, 0))).astype(BF16)
    sbias = _sample_bias(jnp.tile(rbt, (n_q, 1)), n_q)
    hsel = (jnp.arange(rows)[:, None] % N_HEADS == hid[None, :]).astype(F32)
    o_s = _sample_attention(page_table, qbd, sel, k_new, v_new, sbias, hsel,
                            cache_k[0].reshape(n_pool, PAGE_SIZE, D_ATTN),
                            cache_v[0].reshape(n_pool, PAGE_SIZE, D_ATTN))
    o_s2d = o_s.transpose(1, 0, 2).reshape(n_s, D_ATTN)

    x1_s = _merge(o_s2d, ya_s, main_s, xs, wb, wo, _row_tile(n_s, 512))
    pad_f = _ceil_to((FFN_CONV_W - 1) * nbs, SUBLANES)
    past_f = state_ffn_conv[0].transpose(1, 0, 2).reshape(1, (FFN_CONV_W - 1) * nbs, D_FF)
    past_f = jnp.pad(past_f, ((0, 0), (pad_f - (FFN_CONV_W - 1) * nbs, 0), (0, 0)))
    y_s, ftail_s = _ffn(x1_s, g_ffn, wup, fcw, fcb, past_f, wd, nb=1, nt=1, tm=n_s, stride=nbs, tf=256,
                        rows_real=n_s)

    y_sample = y_s.reshape(n_q, nbs, D_MODEL).transpose(1, 0, 2)
    s_cv = ctail_s[0, pad_c - (CONV_W - 1) * nbs:].reshape(CONV_W - 1, nbs, D_CONV).transpose(1, 0, 2)[None]
    s_fc = ftail_s[0, pad_f - (FFN_CONV_W - 1) * nbs:].reshape(FFN_CONV_W - 1, nbs, D_FF).transpose(1, 0, 2)[None]

    return (y_prompt, y_sample, p_k, p_v, p_ik, p_cv, p_fc,
            k_s[None], v_s[None], ki_s[None], s_cv, s_fc)
```

```python
import functools
import math

import jax
import jax.numpy as jnp
from jax import lax
from jax.experimental import pallas as pl
from jax.experimental.pallas import tpu as pltpu

F32 = jnp.float32
BF16 = jnp.bfloat16
I32 = jnp.int32

D_MODEL = 1024
D_CONV = D_MODEL // 2
CONV_W = 31
N_HEADS = 8
HEAD_DIM = 64
D_ATTN = N_HEADS * HEAD_DIM
IDX_HEADS = 8
IDX_DIM = 64
TOP_K_MAX = 256
N_BUCKETS = 32
MAX_EXACT = N_BUCKETS // 2
REL_MAX_DIST = 128
D_FF = 2816
FFN_CONV_W = 3
N_META = 16
PAGE_SIZE = 128
PAGE_SHIFT = 7
GATHER_UNROLL = 8
EPS = 1e-6
ATTN_SCALE = HEAD_DIM ** -0.5
IDX_SCALE = (IDX_HEADS * IDX_DIM) ** -0.5
D_MAIN = 2 * D_CONV + 4 * D_ATTN + 2 * D_MODEL
COL_Q, COL_K, COL_V, COL_QI, COL_GA, COL_GB = 1024, 1536, 2048, 2560, 3072, 4096

LANES = 128
SUBLANES = 8
KEY_INVALID = -2 ** 31
NEG_BIG = -1e30
VMEM_LIMIT = 56 * 1024 * 1024


def _cparams(sem):
    return pltpu.CompilerParams(dimension_semantics=sem, vmem_limit_bytes=VMEM_LIMIT)


def _sigmoid(x):
    return 1.0 / (1.0 + jnp.exp(-x))


def _ceil_to(x, m):
    return -(-x // m) * m


def _in_proj_kernel(x_ref, g_ref, wm_ref, ws_ref, gqk_ref, seg_ref, om_ref, os_ref, xn_ref):
    j = pl.program_id(1)

    @pl.when(j == 0)
    def _():
        x = x_ref[...]
        ms = jnp.mean(x * x, axis=-1, keepdims=True)
        xn_ref[...] = (x * lax.rsqrt(ms + EPS) * g_ref[...]).astype(BF16)
        os_ref[...] = jnp.dot(xn_ref[...], ws_ref[...], preferred_element_type=F32)

    y = jnp.dot(xn_ref[...], wm_ref[...], preferred_element_type=F32)
    is_qk = jnp.logical_or(j == COL_Q // D_ATTN, j == COL_K // D_ATTN)

    @pl.when(is_qk)
    def _():
        y2 = y * y
        hi = y2.astype(BF16)
        lo = (y2 - hi.astype(F32)).astype(BF16)
        ms = (jnp.dot(hi, seg_ref[...], preferred_element_type=F32)
              + jnp.dot(lo, seg_ref[...], preferred_element_type=F32))
        g = gqk_ref[pl.ds(j - COL_Q // D_ATTN, 1), :]
        om_ref[...] = y * lax.rsqrt(ms + EPS) * g

    @pl.when(jnp.logical_not(is_qk))
    def _():
        om_ref[...] = y


def _in_proj(x2d, g, wm, ws, gqk, seg, tm):
    n = x2d.shape[0]
    tn = D_ATTN
    return pl.pallas_call(
        _in_proj_kernel,
        grid=(n // tm, D_MAIN // tn),
        in_specs=[
            pl.BlockSpec((tm, D_MODEL), lambda i, j: (i, 0)),
            pl.BlockSpec((1, D_MODEL), lambda i, j: (0, 0)),
            pl.BlockSpec((D_MODEL, tn), lambda i, j: (0, j)),
            pl.BlockSpec((D_MODEL, LANES), lambda i, j: (0, 0)),
            pl.BlockSpec((2, D_ATTN), lambda i, j: (0, 0)),
            pl.BlockSpec((D_ATTN, D_ATTN), lambda i, j: (0, 0)),
        ],
        out_specs=[
            pl.BlockSpec((tm, tn), lambda i, j: (i, j)),
            pl.BlockSpec((tm, LANES), lambda i, j: (i, 0)),
        ],
        out_shape=[jax.ShapeDtypeStruct((n, D_MAIN), F32), jax.ShapeDtypeStruct((n, LANES), F32)],
        scratch_shapes=[pltpu.VMEM((tm, D_MODEL), BF16)],
        compiler_params=_cparams(("parallel", "arbitrary")),
        name="in_proj",
    )(x2d, g, wm, ws, gqk, seg)


def _conv_kernel(a_ref, past_ref, cw_ref, cb_ref, lg_ref, lb_ref, wa_ref, ya_ref, tail_ref, ext_ref, h_ref,
                 *, stride, tm, pad, rc, tail_tile, tail_loc):
    t = pl.program_id(1)

    @pl.when(t == 0)
    def _():
        ext_ref[0:pad, :] = past_ref[0]

    @pl.when(t > 0)
    def _():
        ext_ref[0:pad, :] = ext_ref[tm:tm + pad, :]

    a = a_ref[...]
    ext_ref[pad:pad + tm, :] = a[:, :D_CONV] * _sigmoid(a[:, D_CONV:])

    @pl.when(t <= tail_tile)
    def _():
        tail_ref[0] = ext_ref[tail_loc:tail_loc + pad, :]

    cb = cb_ref[...]
    lg = lg_ref[...]
    lb = lb_ref[...]
    for r0 in range(0, tm, rc):
        acc = jnp.zeros((rc, D_CONV), F32) + cb
        for w in range(CONV_W):
            off = pad - (CONV_W - 1 - w) * stride + r0
            acc = acc + ext_ref[off:off + rc, :] * cw_ref[w:w + 1, :]
        mu = jnp.mean(acc, axis=-1, keepdims=True)
        d = acc - mu
        var = jnp.mean(d * d, axis=-1, keepdims=True)
        h = d * lax.rsqrt(var + EPS) * lg + lb
        h_ref[r0:r0 + rc, :] = (h * _sigmoid(h)).astype(BF16)
    ya_ref[...] = jnp.dot(h_ref[...], wa_ref[...], preferred_element_type=F32)


def _tail_position(rows_real, tm):
    tail_tile = (rows_real - 1) // tm
    return tail_tile, rows_real - tail_tile * tm


def _conv_branch(main, past, cw, cb, lg, lb, wa, *, nb, nt, tm, stride, rows_real):
    pad = past.shape[1]
    rc = 32 if tm % 32 == 0 else SUBLANES
    tail_tile, tail_loc = _tail_position(rows_real, tm)
    kern = functools.partial(_conv_kernel, stride=stride, tm=tm, pad=pad, rc=rc,
                             tail_tile=tail_tile, tail_loc=tail_loc)
    return pl.pallas_call(
        kern,
        grid=(nb, nt),
        in_specs=[
            pl.BlockSpec((tm, 2 * D_CONV), lambda b, t: (b * nt + t, 0)),
            pl.BlockSpec((1, pad, D_CONV), lambda b, t: (b, 0, 0)),
            pl.BlockSpec((32, D_CONV), lambda b, t: (0, 0)),
            pl.BlockSpec((1, D_CONV), lambda b, t: (0, 0)),
            pl.BlockSpec((1, D_CONV), lambda b, t: (0, 0)),
            pl.BlockSpec((1, D_CONV), lambda b, t: (0, 0)),
            pl.BlockSpec((D_CONV, D_MODEL), lambda b, t: (0, 0)),
        ],
        out_specs=[
            pl.BlockSpec((tm, D_MODEL), lambda b, t: (b * nt + t, 0)),
            pl.BlockSpec((1, pad, D_CONV), lambda b, t: (b, 0, 0)),
        ],
        out_shape=[jax.ShapeDtypeStruct((nb * nt * tm, D_MODEL), F32),
                   jax.ShapeDtypeStruct((nb, pad, D_CONV), F32)],
        scratch_shapes=[pltpu.VMEM((pad + tm, D_CONV), F32), pltpu.VMEM((tm, D_CONV), BF16)],
        compiler_params=_cparams(("parallel", "arbitrary")),
        name="conv_branch",
    )(main, past, cw, cb, lg, lb, wa)


def _rel_bucket(rel):
    n = jnp.maximum(rel, 0)
    nf = jnp.maximum(n, 1).astype(F32)
    large = MAX_EXACT + (jnp.log(nf / MAX_EXACT) / math.log(REL_MAX_DIST / MAX_EXACT)
                         * (N_BUCKETS - MAX_EXACT)).astype(I32)
    large = jnp.minimum(large, N_BUCKETS - 1)
    return jnp.where(n < MAX_EXACT, n, large)


def _bias_tile(rel, rb_rows):
    bucket = _rel_bucket(rel)
    out = jnp.zeros(rel.shape, F32)
    for b in range(N_BUCKETS):
        out = jnp.where(bucket == b, rb_rows[:, b:b + 1], out)
    return out


def _prompt_bias_kernel(rbt_ref, o_ref):
    row = lax.broadcasted_iota(I32, (LANES, LANES), 0)
    col = lax.broadcasted_iota(I32, (LANES, LANES), 1)
    for h in range(N_HEADS):
        for d in range(3):
            o_ref[h * 3 + d] = _bias_tile(d * LANES + row - col, rbt_ref[h:h + 1, :])


def _prompt_bias(rbt):
    return pl.pallas_call(
        _prompt_bias_kernel,
        out_shape=jax.ShapeDtypeStruct((N_HEADS * 3, LANES, LANES), F32),
        name="prompt_bias",
    )(rbt)


def _sample_bias_kernel(rbt_ref, o_ref, *, n_q):
    rows = n_q * N_HEADS
    q = lax.broadcasted_iota(I32, (rows, LANES), 0) // N_HEADS
    col = lax.broadcasted_iota(I32, (rows, LANES), 1)
    rb = rbt_ref[...]
    o_ref[0] = _bias_tile(PAGE_SIZE + q - col, rb)
    o_ref[1] = _bias_tile(q - col, rb)
    o_ref[2] = _bias_tile(jnp.full((rows, LANES), REL_MAX_DIST, I32), rb)


def _sample_bias(rbt_rows, n_q):
    rows = n_q * N_HEADS
    return pl.pallas_call(
        functools.partial(_sample_bias_kernel, n_q=n_q),
        out_shape=jax.ShapeDtypeStruct((3, rows, LANES), F32),
        name="sample_bias",
    )(rbt_rows)


def _score_to_key(s, valid):
    bits = pltpu.bitcast(s, I32)
    key = bits ^ ((bits >> 31) & 0x7FFFFFFF)
    key = jnp.where(bits == KEY_INVALID, 0, key)
    return jnp.where(valid, key, KEY_INVALID)


def _threshold_search(count_ge, n_keep, shape):
    cnt_valid = count_ge(jnp.full(shape, KEY_INVALID + 1, I32))
    c0 = count_ge(jnp.zeros(shape, I32))
    ok0 = c0 >= n_keep
    base = jnp.where(ok0, 0, KEY_INVALID).astype(I32)
    cnt = jnp.where(ok0, c0, cnt_valid)

    def body(t, carry):
        base, cnt = carry
        cand = base | lax.shift_left(jnp.int32(1), 30 - t)
        c = count_ge(cand)
        ok = c >= n_keep
        return jnp.where(ok, cand, base), jnp.where(ok, c, cnt)

    base, cnt = lax.fori_loop(0, 31, body, (base, cnt))
    return jnp.maximum(base, KEY_INVALID + 1), cnt


def _pattn_kernel(q_ref, kt_ref, v_ref, qi_ref, kit_ref, wi_ref, bias_ref, tri_ref, o_ref,
                  keys_ref, m_ref, l_ref, acc_ref, *, n_keep):
    i = pl.program_id(1)
    tq = LANES
    shape = (tq, LANES)
    nchunks = i + 1
    row = lax.broadcasted_iota(I32, shape, 0)
    col = lax.broadcasted_iota(I32, shape, 1)

    wi = wi_ref[0] * IDX_SCALE
    wcols = [jnp.broadcast_to(wi[:, h:h + 1], shape) for h in range(IDX_HEADS)]

    def idx_body(j, carry):
        off = pl.multiple_of(j * LANES, LANES)
        kic = kit_ref[0, :, pl.ds(off, LANES)]
        s = jnp.zeros(shape, F32)
        for h in range(IDX_HEADS):
            s = s + wcols[h] * jnp.maximum(jnp.dot(qi_ref[0, h], kic, preferred_element_type=F32), 0.0)
        valid = jnp.logical_or(j < i, col <= row)
        keys_ref[:, pl.ds(off, LANES)] = _score_to_key(s, valid)
        return carry

    lax.fori_loop(0, nchunks, idx_body, 0)

    def count_ge(cand):
        def body(j, acc):
            off = pl.multiple_of(j * LANES, LANES)
            return acc + jnp.where(keys_ref[:, pl.ds(off, LANES)] >= cand, 1.0, 0.0)
        acc = lax.fori_loop(0, nchunks, body, jnp.zeros(shape, F32))
        return jnp.broadcast_to(jnp.sum(acc, axis=1, keepdims=True), shape)

    base, cnt = _threshold_search(count_ge, float(n_keep), shape)

    @pl.when(jnp.max(cnt) > float(n_keep))
    def _():
        need = float(n_keep) - count_ge(base + 1)
        tie_row = cnt > float(n_keep)

        def tie_body(j, carry):
            off = pl.multiple_of(j * LANES, LANES)
            k = keys_ref[:, pl.ds(off, LANES)]
            eq = jnp.logical_and(k == base, tie_row)
            eqf = jnp.where(eq, 1.0, 0.0)
            pref = carry + jnp.dot(eqf.astype(BF16), tri_ref[...], preferred_element_type=F32)
            drop = jnp.logical_and(eq, pref > need)
            keys_ref[:, pl.ds(off, LANES)] = jnp.where(drop, KEY_INVALID, k)
            return carry + jnp.broadcast_to(jnp.sum(eqf, axis=1, keepdims=True), shape)

        lax.fori_loop(0, nchunks, tie_body, jnp.zeros(shape, F32))

    m_ref[...] = jnp.full(m_ref.shape, NEG_BIG, F32)
    l_ref[...] = jnp.zeros(l_ref.shape, F32)
    acc_ref[...] = jnp.zeros(acc_ref.shape, F32)

    def att_body(j, carry):
        off = pl.multiple_of(j * LANES, LANES)
        sel = keys_ref[:, pl.ds(off, LANES)] >= base
        d = jnp.minimum(i - j, 2)
        for h in range(N_HEADS):
            s = jnp.dot(q_ref[0, h], kt_ref[0, h, :, pl.ds(off, LANES)], preferred_element_type=F32)
            s = jnp.where(sel, s + bias_ref[h * 3 + d], NEG_BIG)
            m_old = m_ref[h]
            m_new = jnp.maximum(m_old, jnp.max(s, axis=1, keepdims=True))
            p = jnp.exp(s - m_new)
            alpha = jnp.exp(m_old - m_new)
            l_ref[h] = alpha * l_ref[h] + jnp.sum(p, axis=1, keepdims=True)
            pv = jnp.dot(p.astype(BF16), v_ref[0, h, pl.ds(off, LANES), :], preferred_element_type=F32)
            acc_ref[h] = alpha * acc_ref[h] + pv
            m_ref[h] = m_new
        return carry

    lax.fori_loop(0, nchunks, att_body, 0)
    for h in range(N_HEADS):
        o_ref[0, h] = acc_ref[h] / l_ref[h]


def _prompt_attention(qh, kt, vh, qih, kit, wi, bias, tri, n_keep):
    nb, _, tp, _ = qh.shape
    tq = LANES
    nq = tp // tq
    return pl.pallas_call(
        functools.partial(_pattn_kernel, n_keep=n_keep),
        grid=(nb, nq),
        in_specs=[
            pl.BlockSpec((1, N_HEADS, tq, HEAD_DIM), lambda b, i: (b, 0, i, 0)),
            pl.BlockSpec((1, N_HEADS, HEAD_DIM, tp), lambda b, i: (b, 0, 0, 0)),
            pl.BlockSpec((1, N_HEADS, tp, HEAD_DIM), lambda b, i: (b, 0, 0, 0)),
            pl.BlockSpec((1, IDX_HEADS, tq, IDX_DIM), lambda b, i: (b, 0, i, 0)),
            pl.BlockSpec((1, IDX_DIM, tp), lambda b, i: (b, 0, 0)),
            pl.BlockSpec((1, tq, IDX_HEADS), lambda b, i: (b, i, 0)),
            pl.BlockSpec((N_HEADS * 3, LANES, LANES), lambda b, i: (0, 0, 0)),
            pl.BlockSpec((LANES, LANES), lambda b, i: (0, 0)),
        ],
        out_specs=pl.BlockSpec((1, N_HEADS, tq, HEAD_DIM), lambda b, i: (b, 0, i, 0)),
        out_shape=jax.ShapeDtypeStruct((nb, N_HEADS, tp, HEAD_DIM), F32),
        scratch_shapes=[
            pltpu.VMEM((tq, tp), I32),
            pltpu.VMEM((N_HEADS, tq, 1), F32),
            pltpu.VMEM((N_HEADS, tq, 1), F32),
            pltpu.VMEM((N_HEADS, tq, HEAD_DIM), F32),
        ],
        compiler_params=_cparams(("parallel", "arbitrary")),
        name="prompt_attention",
    )(qh, kt, vh, qih, kit, wi, bias, tri)


def _sidx_kernel(pt_ref, qi_ref, wi_ref, kin_ref, tri_ref, cache_ref, sel_ref, kibuf, sem, keys_ref,
                 *, n_keep, n_pages, n_q, pages_per_chunk):
    b = pl.program_id(0)
    nb = pl.num_programs(0)
    past = n_pages * PAGE_SIZE
    lp = past + LANES
    slot = b % 2

    def page_copy(bb, p, sl):
        return pltpu.make_async_copy(cache_ref.at[pt_ref[bb, p]], kibuf.at[sl, p], sem.at[sl])

    def start_all(bb, sl):
        def body(p, c):
            page_copy(bb, p, sl).start()
            return c
        lax.fori_loop(0, n_pages, body, 0)

    def wait_all(bb, sl):
        def body(p, c):
            page_copy(bb, p, sl).wait()
            return c
        lax.fori_loop(0, n_pages, body, 0)

    @pl.when(b == 0)
    def _():
        start_all(b, slot)

    @pl.when(b + 1 < nb)
    def _():
        start_all(b + 1, 1 - slot)

    wait_all(b, slot)

    rows = n_q * IDX_HEADS
    qi = qi_ref[0]
    w = wi_ref[0] * IDX_SCALE
    ck = pages_per_chunk * PAGE_SIZE
    keys_ref[...] = jnp.full(keys_ref.shape, KEY_INVALID, I32)

    def score_rows(kc):
        s = lax.dot_general(qi, kc, (((1,), (1,)), ((), ())), preferred_element_type=F32)
        s = jnp.maximum(s, 0.0) * w
        return jnp.sum(s.reshape(n_q, IDX_HEADS, s.shape[-1]), axis=1)

    def chunk_body(c, carry):
        p0 = pl.multiple_of(c * pages_per_chunk, pages_per_chunk)
        kc = kibuf[slot, pl.ds(p0, pages_per_chunk)].reshape(ck, IDX_DIM).astype(BF16)
        s = score_rows(kc)
        off = pl.multiple_of(c * ck, ck)
        keys_ref[0:n_q, pl.ds(off, ck)] = _score_to_key(s, jnp.full(s.shape, True))
        return carry

    lax.fori_loop(0, n_pages // pages_per_chunk, chunk_body, 0)
    s_new = score_rows(kin_ref[0])
    qrow = lax.broadcasted_iota(I32, (n_q, LANES), 0)
    jcol = lax.broadcasted_iota(I32, (n_q, LANES), 1)
    keys_ref[0:n_q, past:lp] = _score_to_key(s_new, jcol <= qrow)

    shape = (SUBLANES, LANES)
    nl = lp // LANES

    def count_ge(cand):
        def body(j, acc):
            off = pl.multiple_of(j * LANES, LANES)
            return acc + jnp.where(keys_ref[:, pl.ds(off, LANES)] >= cand, 1.0, 0.0)
        acc = lax.fori_loop(0, nl, body, jnp.zeros(shape, F32))
        return jnp.broadcast_to(jnp.sum(acc, axis=1, keepdims=True), shape)

    base, cnt = _threshold_search(count_ge, float(n_keep), shape)

    @pl.when(jnp.max(cnt) > float(n_keep))
    def _():
        need = float(n_keep) - count_ge(base + 1)
        tie_row = cnt > float(n_keep)

        def tie_body(j, carry):
            off = pl.multiple_of(j * LANES, LANES)
            k = keys_ref[:, pl.ds(off, LANES)]
            eq = jnp.logical_and(k == base, tie_row)
            eqf = jnp.where(eq, 1.0, 0.0)
            pref = carry + jnp.dot(eqf.astype(BF16), tri_ref[...], preferred_element_type=F32)
            drop = jnp.logical_and(eq, pref > need)
            keys_ref[:, pl.ds(off, LANES)] = jnp.where(drop, KEY_INVALID, k)
            return carry + jnp.broadcast_to(jnp.sum(eqf, axis=1, keepdims=True), shape)

        lax.fori_loop(0, nl, tie_body, jnp.zeros(shape, F32))

    def out_body(j, carry):
        off = pl.multiple_of(j * LANES, LANES)
        sel_ref[0, :, pl.ds(off, LANES)] = jnp.where(keys_ref[:, pl.ds(off, LANES)] >= base, 1.0, 0.0)
        return carry

    lax.fori_loop(0, nl, out_body, 0)


def _sample_index(page_table, qi_rows, wi_rows, ki_new, tri, cache_ik, n_keep):
    nb, n_pages = page_table.shape
    rows = qi_rows.shape[1]
    n_q = rows // IDX_HEADS
    lp = n_pages * PAGE_SIZE + LANES
    pages_per_chunk = 8 if n_pages % 8 == 0 else 1
    kern = functools.partial(_sidx_kernel, n_keep=n_keep, n_pages=n_pages, n_q=n_q,
                             pages_per_chunk=pages_per_chunk)
    return pl.pallas_call(
        kern,
        grid_spec=pltpu.PrefetchScalarGridSpec(
            num_scalar_prefetch=1,
            grid=(nb,),
            in_specs=[
                pl.BlockSpec((1, rows, IDX_DIM), lambda b, pt: (b, 0, 0)),
                pl.BlockSpec((1, rows, 1), lambda b, pt: (b, 0, 0)),
                pl.BlockSpec((1, LANES, IDX_DIM), lambda b, pt: (b, 0, 0)),
                pl.BlockSpec((LANES, LANES), lambda b, pt: (0, 0)),
                pl.BlockSpec(memory_space=pl.ANY),
            ],
            out_specs=pl.BlockSpec((1, SUBLANES, lp), lambda b, pt: (b, 0, 0)),
            scratch_shapes=[
                pltpu.VMEM((2, n_pages, PAGE_SIZE, IDX_DIM), F32),
                pltpu.SemaphoreType.DMA((2,)),
                pltpu.VMEM((SUBLANES, lp), I32),
            ],
        ),
        out_shape=jax.ShapeDtypeStruct((nb, SUBLANES, lp), F32),
        compiler_params=_cparams(("arbitrary",)),
        name="sample_index",
    )(page_table, qi_rows, wi_rows, ki_new, tri, cache_ik)


def _sattn_kernel(pt_ref, qbd_ref, sel_ref, seln_ref, kn_ref, vn_ref, bias_ref, hsel_ref, ck_ref, cv_ref, o_ref,
                  kbuf, vbuf, sem, bias_buf, m_ref, l_ref, acc_ref, *, n_q, n_chunks, ppc):
    b = pl.program_id(0)
    c = pl.program_id(1)
    nb = pl.num_programs(0)
    step = b * n_chunks + c
    slot = step % 2
    rows = n_q * N_HEADS
    ck = ppc * PAGE_SIZE

    def copies(bb, cc, sl, p):
        page = pt_ref[bb, cc * ppc + p]
        return (pltpu.make_async_copy(ck_ref.at[page], kbuf.at[sl, p], sem.at[0, sl]),
                pltpu.make_async_copy(cv_ref.at[page], vbuf.at[sl, p], sem.at[1, sl]))

    def start_all(bb, cc, sl):
        def body(p, carry):
            kc, vc = copies(bb, cc, sl, p)
            kc.start()
            vc.start()
            return carry
        lax.fori_loop(0, ppc, body, 0)

    def wait_all(bb, cc, sl):
        def body(p, carry):
            kc, vc = copies(bb, cc, sl, p)
            kc.wait()
            vc.wait()
            return carry
        lax.fori_loop(0, ppc, body, 0)

    @pl.when(step == 0)
    def _():
        start_all(b, c, slot)

    @pl.when(step + 1 < nb * n_chunks)
    def _():
        nxt = step + 1
        start_all(nxt // n_chunks, nxt % n_chunks, 1 - slot)

    @pl.when(c == 0)
    def _():
        m_ref[...] = jnp.full(m_ref.shape, NEG_BIG, F32)
        l_ref[...] = jnp.zeros(l_ref.shape, F32)
        acc_ref[...] = jnp.zeros(acc_ref.shape, F32)
        bias_buf[...] = jnp.broadcast_to(bias_ref[2][:, 0:1], bias_buf.shape)

    @pl.when(c == n_chunks - 1)
    def _():
        bias_buf[:, ck - PAGE_SIZE:ck] = bias_ref[0]

    wait_all(b, c, slot)
    qbd = qbd_ref[0]

    def expand(sel):
        n = sel.shape[-1]
        return jnp.broadcast_to(sel[:, None, :], (n_q, N_HEADS, n)).reshape(rows, n)

    def update(s, sel, v):
        s = jnp.where(expand(sel) > 0.5, s, NEG_BIG)
        m_old = m_ref[...]
        m_new = jnp.maximum(m_old, jnp.max(s, axis=1, keepdims=True))
        p = jnp.exp(s - m_new)
        alpha = jnp.exp(m_old - m_new)
        l_ref[...] = alpha * l_ref[...] + jnp.sum(p, axis=1, keepdims=True)
        acc_ref[...] = alpha * acc_ref[...] + jnp.dot(p.astype(BF16), v, preferred_element_type=F32)
        m_ref[...] = m_new

    kc = kbuf[slot].reshape(ck, D_ATTN).astype(BF16)
    vc = vbuf[slot].reshape(ck, D_ATTN).astype(BF16)
    s = lax.dot_general(qbd, kc, (((1,), (1,)), ((), ())), preferred_element_type=F32) + bias_buf[...]
    update(s, sel_ref[0, 0:n_q, :], vc)

    @pl.when(c == n_chunks - 1)
    def _():
        s_new = lax.dot_general(qbd, kn_ref[0], (((1,), (1,)), ((), ())), preferred_element_type=F32)
        update(s_new + bias_ref[1], seln_ref[0, 0:n_q, :], vn_ref[0])
        full = acc_ref[...] / l_ref[...] * hsel_ref[...]
        o_ref[0] = jnp.sum(full.reshape(n_q, N_HEADS, D_ATTN), axis=1)


def _sample_attention(page_table, qbd, sel, k_new, v_new, sbias, hsel, cache_k, cache_v):
    nb, n_pages = page_table.shape
    rows = qbd.shape[1]
    n_q = rows // N_HEADS
    ppc = 16 if n_pages % 16 == 0 else 1
    n_chunks = n_pages // ppc
    ck = ppc * PAGE_SIZE
    kern = functools.partial(_sattn_kernel, n_q=n_q, n_chunks=n_chunks, ppc=ppc)
    return pl.pallas_call(
        kern,
        grid_spec=pltpu.PrefetchScalarGridSpec(
            num_scalar_prefetch=1,
            grid=(nb, n_chunks),
            in_specs=[
                pl.BlockSpec((1, rows, D_ATTN), lambda b, c, pt: (b, 0, 0)),
                pl.BlockSpec((1, SUBLANES, ck), lambda b, c, pt: (b, 0, c)),
                pl.BlockSpec((1, SUBLANES, LANES), lambda b, c, pt: (b, 0, n_chunks * ppc)),
                pl.BlockSpec((1, LANES, D_ATTN), lambda b, c, pt: (b, 0, 0)),
                pl.BlockSpec((1, LANES, D_ATTN), lambda b, c, pt: (b, 0, 0)),
                pl.BlockSpec((3, rows, LANES), lambda b, c, pt: (0, 0, 0)),
                pl.BlockSpec((rows, D_ATTN), lambda b, c, pt: (0, 0)),
                pl.BlockSpec(memory_space=pl.ANY),
                pl.BlockSpec(memory_space=pl.ANY),
            ],
            out_specs=pl.BlockSpec((1, n_q, D_ATTN), lambda b, c, pt: (b, 0, 0)),
            scratch_shapes=[
                pltpu.VMEM((2, ppc, PAGE_SIZE, D_ATTN), F32),
                pltpu.VMEM((2, ppc, PAGE_SIZE, D_ATTN), F32),
                pltpu.SemaphoreType.DMA((2, 2)),
                pltpu.VMEM((rows, ck), F32),
                pltpu.VMEM((rows, 1), F32),
                pltpu.VMEM((rows, 1), F32),
                pltpu.VMEM((rows, D_ATTN), F32),
            ],
        ),
        out_shape=jax.ShapeDtypeStruct((nb, n_q, D_ATTN), F32),
        compiler_params=_cparams(("arbitrary", "arbitrary")),
        name="sample_attention",
    )(page_table, qbd, sel, sel, k_new, v_new, sbias, hsel, cache_k, cache_v)


def _prompt_bias_t_kernel(rbt_ref, o_ref):
    key = lax.broadcasted_iota(I32, (LANES, LANES), 0)
    qry = lax.broadcasted_iota(I32, (LANES, LANES), 1)
    for h in range(N_HEADS):
        for d in range(3):
            o_ref[h * 3 + d] = _bias_tile(d * LANES + qry - key, rbt_ref[h:h + 1, :])


def _prompt_bias_t(rbt):
    return pl.pallas_call(
        _prompt_bias_t_kernel,
        out_shape=jax.ShapeDtypeStruct((N_HEADS * 3, LANES, LANES), F32),
        name="prompt_bias",
    )(rbt)


def _pattn_t_kernel(qt_ref, k_ref, vt_ref, qit_ref, ki_ref, wit_ref, bias_ref, tril_ref, o_ref,
                    keys_ref, qiw_ref, qbd_ref, m_ref, l_ref, acc_ref, ot_ref, *, n_keep):
    i = pl.program_id(1)
    tq = LANES
    shape = (LANES, tq)
    row1 = (1, tq)
    nchunks = i + 1
    key_r = lax.broadcasted_iota(I32, shape, 0)
    qry_c = lax.broadcasted_iota(I32, shape, 1)

    for h in range(IDX_HEADS):
        qiw_ref[:, h * tq:(h + 1) * tq] = qit_ref[0, h * IDX_DIM:(h + 1) * IDX_DIM, :]
    qbd_ref[...] = jnp.zeros(qbd_ref.shape, BF16)
    for h in range(N_HEADS):
        hp, e = divmod(h, 2)
        qbd_ref[hp, e * HEAD_DIM:(e + 1) * HEAD_DIM, e * tq:(e + 1) * tq] = qt_ref[0, h * HEAD_DIM:(h + 1) * HEAD_DIM, :]
    wit = wit_ref[0] * IDX_SCALE

    def idx_body(j, carry):
        off = pl.multiple_of(j * LANES, LANES)
        sall = jnp.dot(ki_ref[0, pl.ds(off, LANES), :], qiw_ref[...], preferred_element_type=F32)
        s = jnp.zeros(shape, F32)
        for h in range(IDX_HEADS):
            s = s + wit[h:h + 1, :] * jnp.maximum(sall[:, h * tq:(h + 1) * tq], 0.0)
        valid = jnp.logical_or(j < i, key_r <= qry_c)
        keys_ref[pl.ds(off, LANES), :] = _score_to_key(s, valid)
        return carry

    lax.fori_loop(0, nchunks, idx_body, 0)

    def count_ge(cand):
        cb = jnp.broadcast_to(cand, shape)

        def body(j, acc):
            off = pl.multiple_of(j * LANES, LANES)
            return acc + jnp.where(keys_ref[pl.ds(off, LANES), :] >= cb, 1.0, 0.0)
        acc = lax.fori_loop(0, nchunks, body, jnp.zeros(shape, F32))
        return jnp.sum(acc, axis=0, keepdims=True)

    base, cnt = _threshold_search(count_ge, float(n_keep), row1)

    @pl.when(jnp.max(cnt) > float(n_keep))
    def _():
        need = float(n_keep) - count_ge(base + 1)
        tie_q = cnt > float(n_keep)

        def tie_body(j, carry):
            off = pl.multiple_of(j * LANES, LANES)
            k = keys_ref[pl.ds(off, LANES), :]
            eq = jnp.logical_and(k == base, tie_q)
            eqf = jnp.where(eq, 1.0, 0.0)
            pref = carry + jnp.dot(tril_ref[...], eqf.astype(BF16), preferred_element_type=F32)
            drop = jnp.logical_and(eq, pref > need)
            keys_ref[pl.ds(off, LANES), :] = jnp.where(drop, KEY_INVALID, k)
            return carry + jnp.sum(eqf, axis=0, keepdims=True)

        lax.fori_loop(0, nchunks, tie_body, jnp.zeros(row1, F32))

    m_ref[...] = jnp.full(m_ref.shape, NEG_BIG, F32)
    l_ref[...] = jnp.zeros(l_ref.shape, F32)
    acc_ref[...] = jnp.zeros(acc_ref.shape, F32)
    base_b = jnp.broadcast_to(base, shape)

    def att_body(j, carry):
        off = pl.multiple_of(j * LANES, LANES)
        sel = keys_ref[pl.ds(off, LANES), :] >= base_b
        d = jnp.minimum(i - j, 2)
        for hp in range(N_HEADS // 2):
            s2 = jnp.dot(k_ref[0, pl.ds(off, LANES), hp * LANES:(hp + 1) * LANES], qbd_ref[hp],
                         preferred_element_type=F32)
            for e in range(2):
                h = 2 * hp + e
                s = jnp.where(sel, s2[:, e * tq:(e + 1) * tq] + bias_ref[h * 3 + d], NEG_BIG)
                m_old = m_ref[h]
                m_new = jnp.maximum(m_old, jnp.max(s, axis=0, keepdims=True))
                p = jnp.exp(s - m_new)
                alpha = jnp.exp(m_old - m_new)
                l_ref[h] = alpha * l_ref[h] + jnp.sum(p, axis=0, keepdims=True)
                pv = jnp.dot(vt_ref[0, h * HEAD_DIM:(h + 1) * HEAD_DIM, pl.ds(off, LANES)], p.astype(BF16),
                             preferred_element_type=F32)
                acc_ref[h] = alpha * acc_ref[h] + pv
                m_ref[h] = m_new
        return carry

    lax.fori_loop(0, nchunks, att_body, 0)
    for h in range(N_HEADS):
        ot_ref[h * HEAD_DIM:(h + 1) * HEAD_DIM, :] = acc_ref[h] / l_ref[h]
    o_ref[0] = ot_ref[...].T


def _prompt_attention_t(qt, kn, vt, qit, ki, wit, bias, tril, n_keep):
    nb, tp, _ = kn.shape
    tq = LANES
    nq = tp // tq
    return pl.pallas_call(
        functools.partial(_pattn_t_kernel, n_keep=n_keep),
        grid=(nb, nq),
        in_specs=[
            pl.BlockSpec((1, D_ATTN, tq), lambda b, i: (b, 0, i)),
            pl.BlockSpec((1, tp, D_ATTN), lambda b, i: (b, 0, 0)),
            pl.BlockSpec((1, D_ATTN, tp), lambda b, i: (b, 0, 0)),
            pl.BlockSpec((1, IDX_HEADS * IDX_DIM, tq), lambda b, i: (b, 0, i)),
            pl.BlockSpec((1, tp, IDX_DIM), lambda b, i: (b, 0, 0)),
            pl.BlockSpec((1, IDX_HEADS, tq), lambda b, i: (b, 0, i)),
            pl.BlockSpec((N_HEADS * 3, LANES, LANES), lambda b, i: (0, 0, 0)),
            pl.BlockSpec((LANES, LANES), lambda b, i: (0, 0)),
        ],
        out_specs=pl.BlockSpec((1, tq, D_ATTN), lambda b, i: (b, i, 0)),
        out_shape=jax.ShapeDtypeStruct((nb, tp, D_ATTN), F32),
        scratch_shapes=[
            pltpu.VMEM((tp, tq), I32),
            pltpu.VMEM((IDX_DIM, IDX_HEADS * tq), BF16),
            pltpu.VMEM((N_HEADS // 2, LANES, 2 * tq), BF16),
            pltpu.VMEM((N_HEADS, 1, tq), F32),
            pltpu.VMEM((N_HEADS, 1, tq), F32),
            pltpu.VMEM((N_HEADS, HEAD_DIM, tq), F32),
            pltpu.VMEM((D_ATTN, tq), F32),
        ],
        compiler_params=_cparams(("parallel", "arbitrary")),
        name="prompt_attention",
    )(qt, kn, vt, qit, ki, wit, bias, tril)


def _sidx2_kernel(pt_ref, qi_ref, wi_ref, kin_ref, tri_ref, lin_ref, cache_ref, idx_ref, seln_ref,
                  kibuf, sem, keys_ref, *, n_keep, n_pages, n_q, ppc, ncs, ncp, ns):
    b = pl.program_id(0)
    nb = pl.num_programs(0)
    slot = b % 2

    def page_copy(bb, p, sl):
        return pltpu.make_async_copy(cache_ref.at[pt_ref[bb, p]], kibuf.at[sl, p], sem.at[sl])

    def start_all(bb, sl):
        def body(p, c):
            page_copy(bb, p, sl).start()
            return c
        lax.fori_loop(0, n_pages, body, 0)

    def wait_all(bb, sl):
        def body(p, c):
            page_copy(bb, p, sl).wait()
            return c
        lax.fori_loop(0, n_pages, body, 0)

    @pl.when(b == 0)
    def _():
        start_all(b, slot)

    @pl.when(b + 1 < nb)
    def _():
        start_all(b + 1, 1 - slot)

    wait_all(b, slot)

    qi = qi_ref[0]
    w = wi_ref[0] * IDX_SCALE
    ck = ppc * PAGE_SIZE
    keys_ref[...] = jnp.full(keys_ref.shape, KEY_INVALID, I32)

    def score_rows(kc):
        s = lax.dot_general(qi, kc, (((1,), (1,)), ((), ())), preferred_element_type=F32)
        s = jnp.maximum(s, 0.0) * w
        return jnp.sum(s.reshape(n_q, IDX_HEADS, s.shape[-1]), axis=1)

    def chunk_body(c, carry):
        p0 = pl.multiple_of(c * ppc, ppc)
        kc = kibuf[slot, pl.ds(p0, ppc)].reshape(ck, IDX_DIM).astype(BF16)
        key = _score_to_key(score_rows(kc), jnp.full((n_q, ck), True))
        for q in range(n_q):
            for p in range(ppc):
                keys_ref[q, pl.ds(p0 + p, 1), :] = key[q:q + 1, p * LANES:(p + 1) * LANES]
        return carry

    lax.fori_loop(0, n_pages // ppc, chunk_body, 0)
    s_new = score_rows(kin_ref[0])
    qrow = lax.broadcasted_iota(I32, (n_q, LANES), 0)
    jcol = lax.broadcasted_iota(I32, (n_q, LANES), 1)
    key_new = _score_to_key(s_new, jcol <= qrow)
    for q in range(n_q):
        keys_ref[q, n_pages:n_pages + 1, :] = key_new[q:q + 1, :]

    shape = (n_q, 1, LANES)

    def count_ge(cand):
        hit = jnp.where(keys_ref[:, 0:ncs, :] >= cand, 1.0, 0.0)
        part = jnp.sum(hit, axis=1, keepdims=True)
        return jnp.broadcast_to(jnp.sum(part, axis=2, keepdims=True), shape)

    base, cnt = _threshold_search(count_ge, float(n_keep), shape)

    @pl.when(jnp.max(cnt) > float(n_keep))
    def _():
        need = float(n_keep) - count_ge(base + 1)
        ones = jnp.ones((LANES, LANES), BF16)
        for q in range(n_q):
            k = keys_ref[q]
            eq = jnp.logical_and(k == base[q], cnt[q] > float(n_keep))
            eqb = jnp.where(eq, 1.0, 0.0).astype(BF16)
            within = jnp.dot(eqb, tri_ref[...], preferred_element_type=F32)
            rowcnt = jnp.dot(eqb, ones, preferred_element_type=F32).astype(BF16)
            before = jnp.dot(lin_ref[...], rowcnt, preferred_element_type=F32) - rowcnt.astype(F32)
            drop = jnp.logical_and(eq, within + before > need[q])
            keys_ref[q] = jnp.where(drop, KEY_INVALID, k)

    lane_c = lax.broadcasted_iota(I32, (ns, ncp), 1)
    slot_c = lax.broadcasted_iota(I32, (ns, ncp), 0)
    slot_l = lax.broadcasted_iota(I32, (ns, LANES), 0).astype(F32)
    ones8 = jnp.ones((SUBLANES, LANES), BF16)
    lane8 = lax.broadcasted_iota(I32, (SUBLANES, LANES), 1).astype(BF16)
    for q in range(n_q):
        selb = jnp.where(keys_ref[q] >= base[q], 1.0, 0.0).astype(BF16)
        seln_ref[0, q:q + 1, :] = selb[n_pages:n_pages + 1, :].astype(F32)
        within = jnp.dot(selb, tri_ref[...], preferred_element_type=F32)
        upto = jnp.dot(lin_ref[...], selb, preferred_element_type=F32).astype(BF16)
        pin_row = lax.dot_general(ones8, upto, (((1,), (1,)), ((), ())),
                                  preferred_element_type=F32)[0:1, :]
        chunk_of = jnp.sum(jnp.where(pin_row <= slot_c.astype(F32), 1.0, 0.0), axis=1, keepdims=True)
        onehot = jnp.where(lane_c.astype(F32) == chunk_of, 1.0, 0.0).astype(BF16)
        g_sel = jnp.dot(onehot, selb, preferred_element_type=F32)
        g_within = jnp.dot(onehot, within.astype(BF16), preferred_element_type=F32)
        g_upto = jnp.dot(onehot, upto, preferred_element_type=F32)
        g_pin = jnp.sum(g_upto, axis=1, keepdims=True)
        g_cnt = jnp.sum(g_sel, axis=1, keepdims=True)
        target = slot_l - (g_pin - g_cnt) + 1.0
        match = jnp.logical_and(g_sel > 0.5, g_within == target)
        matchb = jnp.where(match, 1.0, 0.0).astype(BF16)
        row_l = lax.dot_general(lane8, matchb, (((1,), (1,)), ((), ())), preferred_element_type=F32)
        chunk_b = jnp.broadcast_to(chunk_of, (ns, LANES))
        row_c = lax.dot_general(ones8, jnp.where(match, chunk_b, 0.0).astype(BF16),
                                (((1,), (1,)), ((), ())), preferred_element_type=F32)
        idx_ref[0, q:q + 1, :] = (row_c * float(LANES) + row_l)[0:1, :].astype(I32)
    for q in range(n_q, SUBLANES):
        idx_ref[0, q:q + 1, :] = jnp.zeros((1, ns), I32)
        seln_ref[0, q:q + 1, :] = jnp.zeros((1, LANES), F32)


def _sample_index2(page_table, qi_rows, wi_rows, ki_new, tri, cache_ik, n_keep):
    nb, n_pages = page_table.shape
    rows = qi_rows.shape[1]
    n_q = rows // IDX_HEADS
    ppc = 8 if n_pages % 8 == 0 else 1
    ncs = _ceil_to(n_pages + 1, SUBLANES)
    ncp = _ceil_to(n_pages + 1, LANES)
    ns = _ceil_to(n_keep, LANES)
    ar = jnp.arange(ncp)
    lin = (ar[None, :] <= ar[:, None]).astype(BF16)
    kern = functools.partial(_sidx2_kernel, n_keep=n_keep, n_pages=n_pages, n_q=n_q, ppc=ppc,
                             ncs=ncs, ncp=ncp, ns=ns)
    return pl.pallas_call(
        kern,
        grid_spec=pltpu.PrefetchScalarGridSpec(
            num_scalar_prefetch=1,
            grid=(nb,),
            in_specs=[
                pl.BlockSpec((1, rows, IDX_DIM), lambda b, pt: (b, 0, 0)),
                pl.BlockSpec((1, rows, 1), lambda b, pt: (b, 0, 0)),
                pl.BlockSpec((1, LANES, IDX_DIM), lambda b, pt: (b, 0, 0)),
                pl.BlockSpec((LANES, LANES), lambda b, pt: (0, 0)),
                pl.BlockSpec((ncp, ncp), lambda b, pt: (0, 0)),
                pl.BlockSpec(memory_space=pl.ANY),
            ],
            out_specs=[pl.BlockSpec((1, SUBLANES, ns), lambda b, pt: (b, 0, 0)),
                       pl.BlockSpec((1, SUBLANES, LANES), lambda b, pt: (b, 0, 0))],
            scratch_shapes=[
                pltpu.VMEM((2, n_pages, PAGE_SIZE, IDX_DIM), F32),
                pltpu.SemaphoreType.DMA((2,)),
                pltpu.VMEM((n_q, ncp, LANES), I32),
            ],
        ),
        out_shape=[jax.ShapeDtypeStruct((nb, SUBLANES, ns), I32),
                   jax.ShapeDtypeStruct((nb, SUBLANES, LANES), F32)],
        compiler_params=_cparams(("arbitrary",)),
        name="sample_index",
    )(page_table, qi_rows, wi_rows, ki_new, tri, lin, cache_ik)


def _sgather_kernel(pt_ref, idx_smem, idx_ref, seln_ref, q8_ref, kn_ref, vn_ref, rbt_ref, bnew_ref, ck_ref, cv_ref,
                    o_ref, kbuf, vbuf, sem, *, n_keep, n_q, ns, past):
    b = pl.program_id(0)
    nb = pl.num_programs(0)
    slot = b % 2

    def start_all(bb, sl):
        for q in range(n_q):
            def body(s, carry):
                pos = jnp.minimum(idx_smem[bb * n_q + q, s], past - 1)
                page = pt_ref[bb, lax.shift_right_logical(pos, PAGE_SHIFT)]
                off = lax.bitwise_and(pos, PAGE_SIZE - 1)
                pltpu.make_async_copy(ck_ref.at[page, off], kbuf.at[sl, :, q * ns + s], sem.at[0, sl]).start()
                pltpu.make_async_copy(cv_ref.at[page, off], vbuf.at[sl, :, q * ns + s], sem.at[1, sl]).start()
                return carry
            lax.fori_loop(0, n_keep, body, 0, unroll=GATHER_UNROLL if n_keep % GATHER_UNROLL == 0 else 1)

    def wait_all(sl):
        rows = pl.ds(0, n_q * n_keep)
        pltpu.make_async_copy(kbuf.at[sl, :, rows], kbuf.at[sl, :, rows], sem.at[0, sl]).wait()
        pltpu.make_async_copy(vbuf.at[sl, :, rows], vbuf.at[sl, :, rows], sem.at[1, sl]).wait()

    @pl.when(b == 0)
    def _():
        if ns > n_keep:
            kbuf[...] = jnp.zeros(kbuf.shape, F32)
            vbuf[...] = jnp.zeros(vbuf.shape, F32)
        start_all(b, slot)

    @pl.when(b + 1 < nb)
    def _():
        start_all(b + 1, 1 - slot)

    wait_all(slot)

    rbt = rbt_ref[...]
    lane = lax.broadcasted_iota(I32, (1, ns), 1)
    for q in range(n_q):
        pos = idx_ref[0, q:q + 1, :]
        live = jnp.logical_and(lane < n_keep, pos < past)
        bucket = _rel_bucket(past + q - pos)
        bias = jnp.zeros((N_HEADS, ns), F32)
        for bk in range(N_BUCKETS):
            bias = jnp.where(bucket == bk, rbt[:, bk:bk + 1], bias)
        s = jnp.zeros((N_HEADS, ns), F32)
        s_new = jnp.zeros((N_HEADS, LANES), F32)
        for h in range(N_HEADS):
            qh = q8_ref[0, q, h]
            kh = kbuf[slot, h, pl.ds(q * ns, ns), :].astype(BF16)
            s = s + lax.dot_general(qh, kh, (((1,), (1,)), ((), ())), preferred_element_type=F32)
            s_new = s_new + lax.dot_general(qh, kn_ref[0, h], (((1,), (1,)), ((), ())),
                                            preferred_element_type=F32)
        s = jnp.where(live, s + bias, NEG_BIG)
        s_new = jnp.where(seln_ref[0, q:q + 1, :] > 0.5, s_new + bnew_ref[q], NEG_BIG)
        m = jnp.maximum(jnp.max(s, axis=1, keepdims=True), jnp.max(s_new, axis=1, keepdims=True))
        p = jnp.exp(s - m)
        p_new = jnp.exp(s_new - m)
        denom = jnp.sum(p, axis=1, keepdims=True) + jnp.sum(p_new, axis=1, keepdims=True)
        pb = p.astype(BF16)
        pnb = p_new.astype(BF16)
        for h in range(N_HEADS):
            vh = vbuf[slot, h, pl.ds(q * ns, ns), :].astype(BF16)
            o8 = (jnp.dot(pb, vh, preferred_element_type=F32)
                  + jnp.dot(pnb, vn_ref[0, h], preferred_element_type=F32)) / denom
            o_ref[0, q:q + 1, h * HEAD_DIM:(h + 1) * HEAD_DIM] = o8[h:h + 1, :]


def _sample_gather_attention(page_table, idx, seln, q8, knh, vnh, rbt, bnew, ck, cv, n_keep, past):
    nb = page_table.shape[0]
    n_q = q8.shape[1]
    ns = idx.shape[2]
    kern = functools.partial(_sgather_kernel, n_keep=n_keep, n_q=n_q, ns=ns, past=past)
    return pl.pallas_call(
        kern,
        grid_spec=pltpu.PrefetchScalarGridSpec(
            num_scalar_prefetch=2,
            grid=(nb,),
            in_specs=[
                pl.BlockSpec((1, SUBLANES, ns), lambda b, pt, ix: (b, 0, 0)),
                pl.BlockSpec((1, SUBLANES, LANES), lambda b, pt, ix: (b, 0, 0)),
                pl.BlockSpec((1, n_q, N_HEADS, SUBLANES, HEAD_DIM), lambda b, pt, ix: (b, 0, 0, 0, 0)),
                pl.BlockSpec((1, N_HEADS, LANES, HEAD_DIM), lambda b, pt, ix: (b, 0, 0, 0)),
                pl.BlockSpec((1, N_HEADS, LANES, HEAD_DIM), lambda b, pt, ix: (b, 0, 0, 0)),
                pl.BlockSpec((N_HEADS, N_BUCKETS), lambda b, pt, ix: (0, 0)),
                pl.BlockSpec((n_q, N_HEADS, LANES), lambda b, pt, ix: (0, 0, 0)),
                pl.BlockSpec(memory_space=pl.ANY),
                pl.BlockSpec(memory_space=pl.ANY),
            ],
            out_specs=pl.BlockSpec((1, n_q, D_ATTN), lambda b, pt, ix: (b, 0, 0)),
            scratch_shapes=[
                pltpu.VMEM((2, N_HEADS, n_q * ns, HEAD_DIM), F32),
                pltpu.VMEM((2, N_HEADS, n_q * ns, HEAD_DIM), F32),
                pltpu.SemaphoreType.DMA((2, 2)),
            ],
        ),
        out_shape=jax.ShapeDtypeStruct((nb, n_q, D_ATTN), F32),
        compiler_params=_cparams(("arbitrary",)),
        name="sample_attention",
    )(page_table, idx[:, :n_q].reshape(nb * n_q, ns), idx, seln, q8, knh, vnh, rbt, bnew, ck, cv)


def _sample_new_bias_kernel(rbt_ref, o_ref, *, n_q):
    col = lax.broadcasted_iota(I32, (N_HEADS, LANES), 1)
    for q in range(n_q):
        bucket = _rel_bucket(q - col)
        out = jnp.zeros((N_HEADS, LANES), F32)
        for bk in range(N_BUCKETS):
            out = jnp.where(bucket == bk, rbt_ref[:, bk:bk + 1], out)
        o_ref[q] = out


def _sample_new_bias(rbt, n_q):
    return pl.pallas_call(
        functools.partial(_sample_new_bias_kernel, n_q=n_q),
        out_shape=jax.ShapeDtypeStruct((n_q, N_HEADS, LANES), F32),
        name="sample_bias",
    )(rbt)


def _merge_kernel(o_ref, ya_ref, ga_ref, gb_ref, x_ref, wb_ref, wo_ref, x1_ref):
    yb = jnp.dot(o_ref[...].astype(BF16), wb_ref[...], preferred_element_type=F32)
    m = _sigmoid(ga_ref[...]) * ya_ref[...] + _sigmoid(gb_ref[...]) * yb
    x1_ref[...] = x_ref[...] + jnp.dot(m.astype(BF16), wo_ref[...], preferred_element_type=F32)


def _merge(o2d, ya, main, x2d, wb, wo, tm):
    n = x2d.shape[0]
    return pl.pallas_call(
        _merge_kernel,
        grid=(n // tm,),
        in_specs=[
            pl.BlockSpec((tm, D_ATTN), lambda i: (i, 0)),
            pl.BlockSpec((tm, D_MODEL), lambda i: (i, 0)),
            pl.BlockSpec((tm, D_MODEL), lambda i: (i, COL_GA // D_MODEL)),
            pl.BlockSpec((tm, D_MODEL), lambda i: (i, COL_GB // D_MODEL)),
            pl.BlockSpec((tm, D_MODEL), lambda i: (i, 0)),
            pl.BlockSpec((D_ATTN, D_MODEL), lambda i: (0, 0)),
            pl.BlockSpec((D_MODEL, D_MODEL), lambda i: (0, 0)),
        ],
        out_specs=pl.BlockSpec((tm, D_MODEL), lambda i: (i, 0)),
        out_shape=jax.ShapeDtypeStruct((n, D_MODEL), F32),
        compiler_params=_cparams(("parallel",)),
        name="merge_out_proj",
    )(o2d, ya, main, main, x2d, wb, wo)


def _ffn_kernel(x_ref, g_ref, wua_ref, wub_ref, fcw_ref, fcb_ref, past_ref, wd_ref, y_ref, tail_ref,
                xn_ref, carry_ref, ext_ref, acc_ref, *, stride, tm, pad, tail_loc):
    t = pl.program_id(1)
    f = pl.program_id(2)
    nf = pl.num_programs(2)

    @pl.when(f == 0)
    def _():
        x = x_ref[...]
        ms = jnp.mean(x * x, axis=-1, keepdims=True)
        xn_ref[...] = (x * lax.rsqrt(ms + EPS) * g_ref[...]).astype(BF16)
        acc_ref[...] = jnp.zeros(acc_ref.shape, F32)

    a = jnp.dot(xn_ref[...], wua_ref[...], preferred_element_type=F32)
    bq = jnp.dot(xn_ref[...], wub_ref[...], preferred_element_type=F32)

    @pl.when(t == 0)
    def _():
        ext_ref[0:pad, :] = past_ref[0]

    @pl.when(t > 0)
    def _():
        ext_ref[0:pad, :] = carry_ref[f]

    ext_ref[pad:pad + tm, :] = a
    carry_ref[f] = ext_ref[tm:tm + pad, :]
    tail_ref[0, 0] = ext_ref[tail_loc:tail_loc + pad, :]

    conv = (ext_ref[pad - 2 * stride:pad - 2 * stride + tm, :] * fcw_ref[0:1, :]
            + ext_ref[pad - stride:pad - stride + tm, :] * fcw_ref[1:2, :]
            + a * fcw_ref[2:3, :] + fcb_ref[...])
    h = conv * _sigmoid(conv) * bq
    acc_ref[...] += jnp.dot(h.astype(BF16), wd_ref[...], preferred_element_type=F32)

    @pl.when(f == nf - 1)
    def _():
        y_ref[...] = x_ref[...] + acc_ref[...]


def _ffn(x1, g, wup, fcw, fcb, past, wd, *, nb, nt, tm, stride, tf, rows_real):
    pad = past.shape[1]
    nf = D_FF // tf
    tail_tile, tail_loc = _tail_position(rows_real, tm)
    kern = functools.partial(_ffn_kernel, stride=stride, tm=tm, pad=pad, tail_loc=tail_loc)
    y, tail = pl.pallas_call(
        kern,
        grid=(nb, nt, nf),
        in_specs=[
            pl.BlockSpec((tm, D_MODEL), lambda b, t, f: (b * nt + t, 0)),
            pl.BlockSpec((1, D_MODEL), lambda b, t, f: (0, 0)),
            pl.BlockSpec((D_MODEL, tf), lambda b, t, f: (0, f)),
            pl.BlockSpec((D_MODEL, tf), lambda b, t, f: (0, nf + f)),
            pl.BlockSpec((SUBLANES, tf), lambda b, t, f: (0, f)),
            pl.BlockSpec((1, tf), lambda b, t, f: (0, f)),
            pl.BlockSpec((1, pad, tf), lambda b, t, f: (b, 0, f)),
            pl.BlockSpec((tf, D_MODEL), lambda b, t, f: (f, 0)),
        ],
        out_specs=[
            pl.BlockSpec((tm, D_MODEL), lambda b, t, f: (b * nt + t, 0)),
            pl.BlockSpec((1, 1, pad, tf), lambda b, t, f: (b, t, 0, f)),
        ],
        out_shape=[jax.ShapeDtypeStruct((nb * nt * tm, D_MODEL), F32),
                   jax.ShapeDtypeStruct((nb, nt, pad, D_FF), F32)],
        scratch_shapes=[
            pltpu.VMEM((tm, D_MODEL), BF16),
            pltpu.VMEM((nf, pad, tf), F32),
            pltpu.VMEM((pad + tm, tf), F32),
            pltpu.VMEM((tm, D_MODEL), F32),
        ],
        compiler_params=_cparams(("parallel", "arbitrary", "arbitrary")),
        name="conv_ffn",
    )(x1, g, wup, wup, fcw, fcb, past, wd)
    return y, tail[:, tail_tile]


def _row_tile(n, cap):
    best = SUBLANES
    for cand in range(SUBLANES, cap + 1, SUBLANES):
        if n % cand == 0:
            best = cand
    return best


def kernel(x_prompt, x_sample, cache_k, cache_v, cache_idx_k, page_table, state_conv, state_ffn_conv, meta_tokens, g_attn_norm, w_in, conv_w, conv_b, conv_ln_g, conv_ln_b, w_a_out, q_norm_g, k_norm_g, rel_bias, w_b_out, w_o, g_ffn_norm, w_up, ffn_conv_w, ffn_conv_b, w_down):
    nbp, seq, _ = x_prompt.shape
    nbs, n_q, _ = x_sample.shape
    n_pages = page_table.shape[1]
    past_len = n_pages * PAGE_SIZE
    n_pool = cache_k.shape[1]
    t_real = seq + N_META
    tp = _ceil_to(t_real, LANES)
    keep_p = min(TOP_K_MAX, t_real // 4)
    keep_s = min(TOP_K_MAX, (past_len + n_q) // 4)
    assert g_attn_norm.shape[0] == 1, "single trunk layer"

    w = w_in[0]
    c_a, c_q, c_k, c_v, c_qi = 0, 1024, 1536, 2048, 2560
    c_ki, c_wi, c_ga, c_gb = 3072, 3136, 3144, 4168
    wm = jnp.concatenate([w[:, c_a:c_ki], w[:, c_ga:]], axis=1).astype(BF16)
    ws = jnp.concatenate([w[:, c_ki:c_ga], jnp.zeros((D_MODEL, LANES - IDX_DIM - IDX_HEADS), F32)],
                         axis=1).astype(BF16)
    g_attn = g_attn_norm[0][None, :]
    gqk = jnp.stack([jnp.tile(q_norm_g[0], N_HEADS) * ATTN_SCALE, jnp.tile(k_norm_g[0], N_HEADS)])
    hid = jnp.arange(D_ATTN) // HEAD_DIM
    seg = jnp.where(hid[:, None] == hid[None, :], 1.0 / HEAD_DIM, 0.0).astype(BF16)
    cw = jnp.concatenate([conv_w[0], jnp.zeros((32 - CONV_W, D_CONV), F32)], axis=0)
    cb, lg, lb = conv_b[0][None, :], conv_ln_g[0][None, :], conv_ln_b[0][None, :]
    wa = w_a_out[0].astype(BF16)
    wb = w_b_out[0].astype(BF16)
    wo = w_o[0].astype(BF16)
    g_ffn = g_ffn_norm[0][None, :]
    wup = w_up[0].astype(BF16)
    fcw = jnp.concatenate([ffn_conv_w[0], jnp.zeros((SUBLANES - FFN_CONV_W, D_FF), F32)], axis=0)
    fcb = ffn_conv_b[0][None, :]
    wd = w_down[0].astype(BF16)
    rbt = rel_bias.T
    ar = jnp.arange(LANES)
    tri = (ar[:, None] <= ar[None, :]).astype(BF16)

    xp = jnp.concatenate([jnp.broadcast_to(meta_tokens[None], (nbp, N_META, D_MODEL)), x_prompt,
                          jnp.zeros((nbp, tp - t_real, D_MODEL), F32)], axis=1).reshape(nbp * tp, D_MODEL)
    n_p = nbp * tp
    main_p, small_p = _in_proj(xp, g_attn, wm, ws, gqk, seg, _row_tile(n_p, 1024))

    tm_p = _row_tile(tp, 640)
    nt_p = tp // tm_p
    ya_p, ctail_p = _conv_branch(main_p, jnp.zeros((nbp, 32, D_CONV), F32), cw, cb, lg, lb, wa,
                                 nb=nbp, nt=nt_p, tm=tm_p, stride=1, rows_real=t_real)

    def heads(col):
        return main_p[:, col:col + D_ATTN].reshape(nbp, tp, N_HEADS, HEAD_DIM)

    def cols(col):
        return main_p[:, col:col + D_ATTN].reshape(nbp, tp, D_ATTN)

    k_p = heads(COL_K)
    v_p = heads(COL_V)
    qt = cols(COL_Q).transpose(0, 2, 1).astype(BF16)
    vt = cols(COL_V).transpose(0, 2, 1).astype(BF16)
    qit = cols(COL_QI).transpose(0, 2, 1).astype(BF16)
    small_p3 = small_p.reshape(nbp, tp, LANES)
    ki_p = small_p3[:, :, :IDX_DIM]
    wit = small_p3[:, :, IDX_DIM:IDX_DIM + IDX_HEADS].transpose(0, 2, 1)
    o_p = _prompt_attention_t(qt, cols(COL_K).astype(BF16), vt, qit, ki_p.astype(BF16), wit,
                              _prompt_bias_t(rbt), tri.T, keep_p)
    o_p2d = o_p.reshape(n_p, D_ATTN)

    x1_p = _merge(o_p2d, ya_p, main_p, xp, wb, wo, _row_tile(n_p, 512))
    y_p, ftail_p = _ffn(x1_p, g_ffn, wup, fcw, fcb, jnp.zeros((nbp, SUBLANES, D_FF), F32), wd,
                        nb=nbp, nt=nt_p, tm=tm_p, stride=1, tf=256, rows_real=t_real)

    y_prompt = y_p.reshape(nbp, tp, D_MODEL)[:, N_META:t_real]
    p_k = k_p[None, :, :t_real]
    p_v = v_p[None, :, :t_real]
    p_ik = ki_p[None, :, :t_real]
    p_cv = ctail_p[None, :, 32 - (CONV_W - 1):]
    p_fc = ftail_p[None, :, SUBLANES - (FFN_CONV_W - 1):]

    n_s = nbs * n_q
    xs = x_sample.transpose(1, 0, 2).reshape(n_s, D_MODEL)
    main_s, small_s = _in_proj(xs, g_attn, wm, ws, gqk, seg, _row_tile(n_s, 1024))

    pad_c = _ceil_to((CONV_W - 1) * nbs, SUBLANES)
    past_c = state_conv[0].transpose(1, 0, 2).reshape(1, (CONV_W - 1) * nbs, D_CONV)
    past_c = jnp.pad(past_c, ((0, 0), (pad_c - (CONV_W - 1) * nbs, 0), (0, 0)))
    ya_s, ctail_s = _conv_branch(main_s, past_c, cw, cb, lg, lb, wa, nb=1, nt=1, tm=n_s, stride=nbs,
                                 rows_real=n_s)

    def heads_s(col):
        return main_s[:, col:col + D_ATTN].reshape(n_q, nbs, N_HEADS, HEAD_DIM).transpose(1, 0, 2, 3)

    q_s = heads_s(COL_Q)
    k_s = heads_s(COL_K)
    v_s = heads_s(COL_V)
    qi_s = heads_s(COL_QI)
    small_s3 = small_s.reshape(n_q, nbs, LANES).transpose(1, 0, 2)
    ki_s = small_s3[:, :, :IDX_DIM]
    wi_s = small_s3[:, :, IDX_DIM:IDX_DIM + IDX_HEADS]
    rows = n_q * N_HEADS

    qi_rows = qi_s.reshape(nbs, rows, IDX_DIM).astype(BF16)
    wi_rows = wi_s.reshape(nbs, rows, 1)
    ki_new = jnp.pad(ki_s, ((0, 0), (0, LANES - n_q), (0, 0))).astype(BF16)
    idx_s, seln = _sample_index2(page_table, qi_rows, wi_rows, ki_new, tri, cache_idx_k[0], keep_s)

    eye = jnp.eye(N_HEADS, dtype=F32)
    q8 = (q_s[:, :, :, None, :] * eye[None, None, :, :, None]).astype(BF16)
    knh = jnp.pad(k_s.transpose(0, 2, 1, 3), ((0, 0), (0, 0), (0, LANES - n_q), (0, 0))).astype(BF16)
    vnh = jnp.pad(v_s.transpose(0, 2, 1, 3), ((0, 0), (0, 0), (0, LANES - n_q), (0, 0))).astype(BF16)
    o_s = _sample_gather_attention(page_table, idx_s, seln, q8, knh, vnh, rbt, _sample_new_bias(rbt, n_q),
                                   cache_k[0], cache_v[0], keep_s, past_len)
    o_s2d = o_s.transpose(1, 0, 2).reshape(n_s, D_ATTN)

    x1_s = _merge(o_s2d, ya_s, main_s, xs, wb, wo, _row_tile(n_s, 512))
    pad_f = _ceil_to((FFN_CONV_W - 1) * nbs, SUBLANES)
    past_f = state_ffn_conv[0].transpose(1, 0, 2).reshape(1, (FFN_CONV_W - 1) * nbs, D_FF)
    past_f = jnp.pad(past_f, ((0, 0), (pad_f - (FFN_CONV_W - 1) * nbs, 0), (0, 0)))
    y_s, ftail_s = _ffn(x1_s, g_ffn, wup, fcw, fcb, past_f, wd, nb=1, nt=1, tm=n_s, stride=nbs, tf=256,
                        rows_real=n_s)

    y_sample = y_s.reshape(n_q, nbs, D_MODEL).transpose(1, 0, 2)
    s_cv = ctail_s[0, pad_c - (CONV_W - 1) * nbs:].reshape(CONV_W - 1, nbs, D_CONV).transpose(1, 0, 2)[None]
    s_fc = ftail_s[0, pad_f - (FFN_CONV_W - 1) * nbs:].reshape(FFN_CONV_W - 1, nbs, D_FF).transpose(1, 0, 2)[None]

    return (y_prompt, y_sample, p_k, p_v, p_ik, p_cv, p_fc,
            k_s[None], v_s[None], ki_s[None], s_cv, s_fc)
```

```python
import functools
import math

import jax
import jax.numpy as jnp
from jax import lax
from jax.experimental import pallas as pl
from jax.experimental.pallas import tpu as pltpu

F32 = jnp.float32
BF16 = jnp.bfloat16
I32 = jnp.int32

D_MODEL = 1024
D_CONV = D_MODEL // 2
CONV_W = 31
N_HEADS = 8
HEAD_DIM = 64
D_ATTN = N_HEADS * HEAD_DIM
IDX_HEADS = 8
IDX_DIM = 64
TOP_K_MAX = 256
N_BUCKETS = 32
MAX_EXACT = N_BUCKETS // 2
REL_MAX_DIST = 128
D_FF = 2816
FFN_CONV_W = 3
N_META = 16
PAGE_SIZE = 128
PAGE_SHIFT = 7
GATHER_UNROLL = 8
FFN_TILE = D_FF // 2
EPS = 1e-6
ATTN_SCALE = HEAD_DIM ** -0.5
IDX_SCALE = (IDX_HEADS * IDX_DIM) ** -0.5
D_MAIN = 2 * D_CONV + 4 * D_ATTN + 2 * D_MODEL
JB_Q, JB_K, JB_V, JB_QI, JB_G = 2, 3, 4, 5, 6

LANES = 128
SUBLANES = 8
KEY_INVALID = -2 ** 31
NEG_BIG = -1e30
VMEM_LIMIT = 56 * 1024 * 1024
NT_DIMS = (((1,), (1,)), ((), ()))


def _cparams(sem):
    return pltpu.CompilerParams(dimension_semantics=sem, vmem_limit_bytes=VMEM_LIMIT)


def _sigmoid(x):
    return 1.0 / (1.0 + jnp.exp(-x))


def _ceil_to(x, m):
    return -(-x // m) * m


def _split_bf16(x):
    hi = x.astype(BF16)
    return hi, (x - hi.astype(F32)).astype(BF16)


def _in_proj_kernel(x_ref, g_ref, wm_ref, wt_ref, ws_ref, wst_ref, gq_ref, gk_ref, seg_ref,
                    a_ref, k_ref, kb_ref, v_ref, qt_ref, vt_ref, qit_ref, g2_ref, ki_ref, kib_ref, wit_ref,
                    xn_ref):
    j = pl.program_id(1)

    @pl.when(j == 0)
    def _():
        x = x_ref[...]
        ms = jnp.mean(x * x, axis=-1, keepdims=True)
        xn_ref[...] = (x * lax.rsqrt(ms + EPS) * g_ref[...]).astype(BF16)
        ki = jnp.dot(xn_ref[...], ws_ref[...], preferred_element_type=F32)[:, :IDX_DIM]
        ki_ref[...] = ki
        kib_ref[...] = ki.astype(BF16)
        wit_ref[...] = lax.dot_general(wst_ref[...], xn_ref[...], NT_DIMS, preferred_element_type=F32)

    @pl.when(j < JB_Q)
    def _():
        a_ref[...] = jnp.dot(xn_ref[...], wm_ref[...], preferred_element_type=F32)

    @pl.when(j >= JB_G)
    def _():
        g2_ref[...] = jnp.dot(xn_ref[...], wm_ref[...], preferred_element_type=F32)

    @pl.when(j == JB_Q)
    def _():
        yt = lax.dot_general(wt_ref[0], xn_ref[...], NT_DIMS, preferred_element_type=F32)
        hi, lo = _split_bf16(yt * yt)
        ms = (jnp.dot(seg_ref[...], hi, preferred_element_type=F32)
              + jnp.dot(seg_ref[...], lo, preferred_element_type=F32))
        qt_ref[...] = (yt * lax.rsqrt(ms + EPS) * gq_ref[...]).astype(BF16)

    @pl.when(j == JB_K)
    def _():
        y = jnp.dot(xn_ref[...], wm_ref[...], preferred_element_type=F32)
        hi, lo = _split_bf16(y * y)
        ms = (jnp.dot(hi, seg_ref[...], preferred_element_type=F32)
              + jnp.dot(lo, seg_ref[...], preferred_element_type=F32))
        k = y * lax.rsqrt(ms + EPS) * gk_ref[...]
        k_ref[...] = k
        kb_ref[...] = k.astype(BF16)

    @pl.when(j == JB_V)
    def _():
        v_ref[...] = jnp.dot(xn_ref[...], wm_ref[...], preferred_element_type=F32)
        vt_ref[...] = lax.dot_general(wt_ref[1], xn_ref[...], NT_DIMS, preferred_element_type=F32).astype(BF16)

    @pl.when(j == JB_QI)
    def _():
        qit_ref[...] = lax.dot_general(wt_ref[2], xn_ref[...], NT_DIMS, preferred_element_type=F32).astype(BF16)


def _in_proj(x2d, g, wm, wt, ws, wst, gq, gk, seg, tm):
    n = x2d.shape[0]
    tn = D_ATTN
    nj = D_MAIN // tn
    row = lambda i, j: (i, 0)
    colt = lambda i, j: (0, i)
    const2 = lambda i, j: (0, 0)
    out_shape = [
        jax.ShapeDtypeStruct((n, 2 * D_CONV), F32),
        jax.ShapeDtypeStruct((n, D_ATTN), F32),
        jax.ShapeDtypeStruct((n, D_ATTN), BF16),
        jax.ShapeDtypeStruct((n, D_ATTN), F32),
        jax.ShapeDtypeStruct((D_ATTN, n), BF16),
        jax.ShapeDtypeStruct((D_ATTN, n), BF16),
        jax.ShapeDtypeStruct((D_ATTN, n), BF16),
        jax.ShapeDtypeStruct((n, 2 * D_MODEL), F32),
        jax.ShapeDtypeStruct((n, IDX_DIM), F32),
        jax.ShapeDtypeStruct((n, IDX_DIM), BF16),
        jax.ShapeDtypeStruct((IDX_HEADS, n), F32),
    ]
    out_specs = [
        pl.BlockSpec((tm, tn), lambda i, j: (i, jnp.minimum(j, 1))),
        pl.BlockSpec((tm, tn), row),
        pl.BlockSpec((tm, tn), row),
        pl.BlockSpec((tm, tn), row),
        pl.BlockSpec((tn, tm), colt),
        pl.BlockSpec((tn, tm), colt),
        pl.BlockSpec((tn, tm), colt),
        pl.BlockSpec((tm, tn), lambda i, j: (i, jnp.clip(j - JB_G, 0, nj - JB_G - 1))),
        pl.BlockSpec((tm, IDX_DIM), row),
        pl.BlockSpec((tm, IDX_DIM), row),
        pl.BlockSpec((IDX_HEADS, tm), colt),
    ]
    return pl.pallas_call(
        _in_proj_kernel,
        grid=(n // tm, nj),
        in_specs=[
            pl.BlockSpec((tm, D_MODEL), row),
            pl.BlockSpec((1, D_MODEL), const2),
            pl.BlockSpec((D_MODEL, tn), lambda i, j: (0, j)),
            pl.BlockSpec((3, tn, D_MODEL), lambda i, j: (0, 0, 0)),
            pl.BlockSpec((D_MODEL, LANES), const2),
            pl.BlockSpec((IDX_HEADS, D_MODEL), const2),
            pl.BlockSpec((D_ATTN, 1), const2),
            pl.BlockSpec((1, D_ATTN), const2),
            pl.BlockSpec((D_ATTN, D_ATTN), const2),
        ],
        out_specs=out_specs,
        out_shape=out_shape,
        scratch_shapes=[pltpu.VMEM((tm, D_MODEL), BF16)],
        compiler_params=_cparams(("parallel", "arbitrary")),
        name="in_proj",
    )(x2d, g, wm, wt, ws, wst, gq, gk, seg)


def _conv_kernel(a_ref, past_ref, cw_ref, cb_ref, lg_ref, lb_ref, wa_ref, ya_ref, tail_ref, ext_ref, h_ref,
                 *, stride, tm, pad, rc, tail_tile, tail_loc):
    t = pl.program_id(1)

    @pl.when(t == 0)
    def _():
        ext_ref[0:pad, :] = past_ref[0]

    @pl.when(t > 0)
    def _():
        ext_ref[0:pad, :] = ext_ref[tm:tm + pad, :]

    a = a_ref[...]
    ext_ref[pad:pad + tm, :] = a[:, :D_CONV] * _sigmoid(a[:, D_CONV:])

    @pl.when(t <= tail_tile)
    def _():
        tail_ref[0] = ext_ref[tail_loc:tail_loc + pad, :]

    cb = cb_ref[...]
    lg = lg_ref[...]
    lb = lb_ref[...]
    for r0 in range(0, tm, rc):
        acc = jnp.zeros((rc, D_CONV), F32) + cb
        for w in range(CONV_W):
            off = pad - (CONV_W - 1 - w) * stride + r0
            acc = acc + ext_ref[off:off + rc, :] * cw_ref[w:w + 1, :]
        mu = jnp.mean(acc, axis=-1, keepdims=True)
        d = acc - mu
        var = jnp.mean(d * d, axis=-1, keepdims=True)
        h = d * lax.rsqrt(var + EPS) * lg + lb
        h_ref[r0:r0 + rc, :] = (h * _sigmoid(h)).astype(BF16)
    ya_ref[...] = jnp.dot(h_ref[...], wa_ref[...], preferred_element_type=F32)


def _tail_position(rows_real, tm):
    tail_tile = (rows_real - 1) // tm
    return tail_tile, rows_real - tail_tile * tm


def _conv_branch(a_in, past, cw, cb, lg, lb, wa, *, nb, nt, tm, stride, rows_real):
    pad = past.shape[1]
    rc = 32 if tm % 32 == 0 else SUBLANES
    tail_tile, tail_loc = _tail_position(rows_real, tm)
    kern = functools.partial(_conv_kernel, stride=stride, tm=tm, pad=pad, rc=rc,
                             tail_tile=tail_tile, tail_loc=tail_loc)
    return pl.pallas_call(
        kern,
        grid=(nb, nt),
        in_specs=[
            pl.BlockSpec((tm, 2 * D_CONV), lambda b, t: (b * nt + t, 0)),
            pl.BlockSpec((1, pad, D_CONV), lambda b, t: (b, 0, 0)),
            pl.BlockSpec((32, D_CONV), lambda b, t: (0, 0)),
            pl.BlockSpec((1, D_CONV), lambda b, t: (0, 0)),
            pl.BlockSpec((1, D_CONV), lambda b, t: (0, 0)),
            pl.BlockSpec((1, D_CONV), lambda b, t: (0, 0)),
            pl.BlockSpec((D_CONV, D_MODEL), lambda b, t: (0, 0)),
        ],
        out_specs=[
            pl.BlockSpec((tm, D_MODEL), lambda b, t: (b * nt + t, 0)),
            pl.BlockSpec((1, pad, D_CONV), lambda b, t: (b, 0, 0)),
        ],
        out_shape=[jax.ShapeDtypeStruct((nb * nt * tm, D_MODEL), F32),
                   jax.ShapeDtypeStruct((nb, pad, D_CONV), F32)],
        scratch_shapes=[pltpu.VMEM((pad + tm, D_CONV), F32), pltpu.VMEM((tm, D_CONV), BF16)],
        compiler_params=_cparams(("parallel", "arbitrary")),
        name="conv_branch",
    )(a_in, past, cw, cb, lg, lb, wa)


def _rel_bucket(rel):
    n = jnp.maximum(rel, 0)
    nf = jnp.maximum(n, 1).astype(F32)
    large = MAX_EXACT + (jnp.log(nf / MAX_EXACT) / math.log(REL_MAX_DIST / MAX_EXACT)
                         * (N_BUCKETS - MAX_EXACT)).astype(I32)
    large = jnp.minimum(large, N_BUCKETS - 1)
    return jnp.where(n < MAX_EXACT, n, large)


def _bias_of(rel, rb_rows):
    bucket = _rel_bucket(rel)
    out = jnp.zeros(rel.shape, F32)
    for b in range(N_BUCKETS):
        out = jnp.where(bucket == b, rb_rows[:, b:b + 1], out)
    return out


def _prompt_bias_kernel(rbt_ref, b1_ref):
    key = lax.broadcasted_iota(I32, (LANES, LANES), 0)
    qry = lax.broadcasted_iota(I32, (LANES, LANES), 1)
    for h in range(N_HEADS):
        for d in range(3):
            b1_ref[h * 3 + d] = _bias_of(d * LANES + qry - key, rbt_ref[h:h + 1, :])


def _prompt_bias(rbt):
    return pl.pallas_call(
        _prompt_bias_kernel,
        out_shape=jax.ShapeDtypeStruct((N_HEADS * 3, LANES, LANES), F32),
        name="prompt_bias",
    )(rbt)


def _sample_new_bias_kernel(rbt_ref, o_ref, *, n_q):
    col = lax.broadcasted_iota(I32, (N_HEADS, LANES), 1)
    for q in range(n_q):
        o_ref[q] = _bias_of(q - col, rbt_ref[...])


def _sample_new_bias(rbt, n_q):
    return pl.pallas_call(
        functools.partial(_sample_new_bias_kernel, n_q=n_q),
        out_shape=jax.ShapeDtypeStruct((n_q, N_HEADS, LANES), F32),
        name="sample_bias",
    )(rbt)


def _score_to_key(s, valid):
    bits = pltpu.bitcast(s, I32)
    key = bits ^ ((bits >> 31) & 0x7FFFFFFF)
    key = jnp.where(bits == KEY_INVALID, 0, key)
    return jnp.where(valid, key, KEY_INVALID)


def _threshold_search(count_ge, n_keep, shape):
    cnt_valid = count_ge(jnp.full(shape, KEY_INVALID + 1, I32))
    c0 = count_ge(jnp.zeros(shape, I32))
    ok0 = c0 >= n_keep
    base = jnp.where(ok0, 0, KEY_INVALID).astype(I32)
    cnt = jnp.where(ok0, c0, cnt_valid)

    def body(t, carry):
        base, cnt = carry
        cand = base | lax.shift_left(jnp.int32(1), 30 - t)
        c = count_ge(cand)
        ok = c >= n_keep
        return jnp.where(ok, cand, base), jnp.where(ok, c, cnt)

    base, cnt = lax.fori_loop(0, 31, body, (base, cnt))
    return jnp.maximum(base, KEY_INVALID + 1), cnt


def _pattn_kernel(qt_ref, k_ref, vt_ref, qit_ref, ki_ref, wit_ref, b1_ref, tril_ref, o_ref,
                  keys_ref, qiw_ref, qbd_ref, m_ref, l_ref, acc_ref, ot_ref, *, n_keep):
    i = pl.program_id(1)
    tq = LANES
    row1 = (1, tq)
    lead = (i + 1) % 2
    npair = (i + 1) // 2
    qpos = i * tq + lax.broadcasted_iota(I32, row1, 1)

    def for_chunks(fn):
        @pl.when(lead == 1)
        def _():
            fn(0, LANES)

        def body(j, carry):
            fn(pl.multiple_of(lead * LANES + j * 2 * LANES, LANES), 2 * LANES)
            return carry
        lax.fori_loop(0, npair, body, 0)

    for h in range(IDX_HEADS):
        qiw_ref[:, h * tq:(h + 1) * tq] = qit_ref[h * IDX_DIM:(h + 1) * IDX_DIM, :]
    qbd_ref[...] = jnp.zeros(qbd_ref.shape, BF16)
    for h in range(N_HEADS):
        hp, e = divmod(h, 2)
        qbd_ref[hp, e * HEAD_DIM:(e + 1) * HEAD_DIM, e * tq:(e + 1) * tq] = qt_ref[h * HEAD_DIM:(h + 1) * HEAD_DIM, :]
    wit = wit_ref[...] * IDX_SCALE

    def idx_chunk(off, ck):
        sall = jnp.dot(ki_ref[pl.ds(off, ck), :], qiw_ref[...], preferred_element_type=F32)
        s = jnp.zeros((ck, tq), F32)
        for h in range(IDX_HEADS):
            s = s + wit[h:h + 1, :] * jnp.maximum(sall[:, h * tq:(h + 1) * tq], 0.0)
        kpos = off + lax.broadcasted_iota(I32, (ck, tq), 0)
        keys_ref[pl.ds(off, ck), :] = _score_to_key(s, kpos <= qpos)

    for_chunks(idx_chunk)

    blk = (LANES, tq)
    nblk = i + 1

    def count_ge(cand):
        cb = jnp.broadcast_to(cand, blk)

        def body(j, acc):
            off = pl.multiple_of(j * LANES, LANES)
            return acc + jnp.where(keys_ref[pl.ds(off, LANES), :] >= cb, 1.0, 0.0)
        acc = lax.fori_loop(0, nblk, body, jnp.zeros(blk, F32))
        return jnp.sum(acc, axis=0, keepdims=True)

    base, cnt = _threshold_search(count_ge, float(n_keep), row1)

    @pl.when(jnp.max(cnt) > float(n_keep))
    def _():
        need = float(n_keep) - count_ge(base + 1)
        tie_q = cnt > float(n_keep)

        def body(j, seen):
            off = pl.multiple_of(j * LANES, LANES)
            k = keys_ref[pl.ds(off, LANES), :]
            eq = jnp.logical_and(k == base, tie_q)
            eqf = jnp.where(eq, 1.0, 0.0)
            pref = seen + jnp.dot(tril_ref[...], eqf.astype(BF16), preferred_element_type=F32)
            keys_ref[pl.ds(off, LANES), :] = jnp.where(jnp.logical_and(eq, pref > need), KEY_INVALID, k)
            return seen + jnp.sum(eqf, axis=0, keepdims=True)
        lax.fori_loop(0, nblk, body, jnp.zeros(row1, F32))

    m_ref[...] = jnp.full(m_ref.shape, NEG_BIG, F32)
    l_ref[...] = jnp.zeros(l_ref.shape, F32)
    acc_ref[...] = jnp.zeros(acc_ref.shape, F32)
    base_b = jnp.broadcast_to(base, blk)

    def att_body(j, carry):
        off = pl.multiple_of(j * LANES, LANES)
        sel = keys_ref[pl.ds(off, LANES), :] >= base_b
        d = jnp.minimum(i - j, 2)
        for hp in range(N_HEADS // 2):
            s2 = jnp.dot(k_ref[pl.ds(off, LANES), hp * LANES:(hp + 1) * LANES], qbd_ref[hp],
                         preferred_element_type=F32)
            for e in range(2):
                h = 2 * hp + e
                s = jnp.where(sel, s2[:, e * tq:(e + 1) * tq] + b1_ref[h * 3 + d], NEG_BIG)
                m_old = m_ref[h]
                m_new = jnp.maximum(m_old, jnp.max(s, axis=0, keepdims=True))
                p = jnp.exp(s - m_new)
                alpha = jnp.exp(m_old - m_new)
                l_ref[h] = alpha * l_ref[h] + jnp.sum(p, axis=0, keepdims=True)
                pv = jnp.dot(vt_ref[h * HEAD_DIM:(h + 1) * HEAD_DIM, pl.ds(off, LANES)], p.astype(BF16),
                             preferred_element_type=F32)
                acc_ref[h] = alpha * acc_ref[h] + pv
                m_ref[h] = m_new
        return carry

    lax.fori_loop(0, nblk, att_body, 0)
    for h in range(N_HEADS):
        ot_ref[h * HEAD_DIM:(h + 1) * HEAD_DIM, :] = acc_ref[h] / l_ref[h]
    o_ref[...] = ot_ref[...].T


def _prompt_attention(qt, kb, vt, qit, kib, wit, b1, tril, n_keep, nb, tp):
    tq = LANES
    nq = tp // tq
    n = nb * tp
    qcol = lambda b, i: (0, b * nq + i)
    return pl.pallas_call(
        functools.partial(_pattn_kernel, n_keep=n_keep),
        grid=(nb, nq),
        in_specs=[
            pl.BlockSpec((D_ATTN, tq), qcol),
            pl.BlockSpec((tp, D_ATTN), lambda b, i: (b, 0)),
            pl.BlockSpec((D_ATTN, tp), lambda b, i: (0, b)),
            pl.BlockSpec((IDX_HEADS * IDX_DIM, tq), qcol),
            pl.BlockSpec((tp, IDX_DIM), lambda b, i: (b, 0)),
            pl.BlockSpec((IDX_HEADS, tq), qcol),
            pl.BlockSpec((N_HEADS * 3, LANES, LANES), lambda b, i: (0, 0, 0)),
            pl.BlockSpec((LANES, LANES), lambda b, i: (0, 0)),
        ],
        out_specs=pl.BlockSpec((tq, D_ATTN), lambda b, i: (b * nq + i, 0)),
        out_shape=jax.ShapeDtypeStruct((n, D_ATTN), F32),
        scratch_shapes=[
            pltpu.VMEM((tp, tq), I32),
            pltpu.VMEM((IDX_DIM, IDX_HEADS * tq), BF16),
            pltpu.VMEM((N_HEADS // 2, LANES, 2 * tq), BF16),
            pltpu.VMEM((N_HEADS, 1, tq), F32),
            pltpu.VMEM((N_HEADS, 1, tq), F32),
            pltpu.VMEM((N_HEADS, HEAD_DIM, tq), F32),
            pltpu.VMEM((D_ATTN, tq), F32),
        ],
        compiler_params=_cparams(("parallel", "arbitrary")),
        name="prompt_attention",
    )(qt, kb, vt, qit, kib, wit, b1, tril)


def _sidx_kernel(pt_ref, qi_ref, wi_ref, kin_ref, tri_ref, lin_ref, cache_ref, idx_ref, seln_ref,
                 kibuf, sem, keys_ref, *, n_keep, n_pages, n_q, ppc, ncs, ncp, ns):
    b = pl.program_id(0)
    nb = pl.num_programs(0)
    slot = b % 2

    def page_copy(bb, p, sl):
        return pltpu.make_async_copy(cache_ref.at[0, pt_ref[bb, p]], kibuf.at[sl, p], sem.at[sl])

    def start_all(bb, sl):
        def body(p, c):
            page_copy(bb, p, sl).start()
            return c
        lax.fori_loop(0, n_pages, body, 0)

    def wait_all(bb, sl):
        def body(p, c):
            page_copy(bb, p, sl).wait()
            return c
        lax.fori_loop(0, n_pages, body, 0)

    @pl.when(b == 0)
    def _():
        start_all(b, slot)

    @pl.when(b + 1 < nb)
    def _():
        start_all(b + 1, 1 - slot)

    wait_all(b, slot)

    qi = qi_ref[0]
    w = wi_ref[0] * IDX_SCALE
    ck = ppc * PAGE_SIZE
    keys_ref[...] = jnp.full(keys_ref.shape, KEY_INVALID, I32)

    def score_rows(kc):
        s = lax.dot_general(qi, kc, NT_DIMS, preferred_element_type=F32)
        s = jnp.maximum(s, 0.0) * w
        return jnp.sum(s.reshape(n_q, IDX_HEADS, s.shape[-1]), axis=1)

    def chunk_body(c, carry):
        p0 = pl.multiple_of(c * ppc, ppc)
        kc = kibuf[slot, pl.ds(p0, ppc)].reshape(ck, IDX_DIM).astype(BF16)
        key = _score_to_key(score_rows(kc), jnp.full((n_q, ck), True))
        for q in range(n_q):
            for p in range(ppc):
                keys_ref[q, pl.ds(p0 + p, 1), :] = key[q:q + 1, p * LANES:(p + 1) * LANES]
        return carry

    lax.fori_loop(0, n_pages // ppc, chunk_body, 0)
    s_new = score_rows(kin_ref[0])
    qrow = lax.broadcasted_iota(I32, (n_q, LANES), 0)
    jcol = lax.broadcasted_iota(I32, (n_q, LANES), 1)
    key_new = _score_to_key(s_new, jcol <= qrow)
    for q in range(n_q):
        keys_ref[q, n_pages:n_pages + 1, :] = key_new[q:q + 1, :]

    shape = (n_q, 1, LANES)

    def count_ge(cand):
        hit = jnp.where(keys_ref[:, 0:ncs, :] >= cand, 1.0, 0.0)
        part = jnp.sum(hit, axis=1, keepdims=True)
        return jnp.broadcast_to(jnp.sum(part, axis=2, keepdims=True), shape)

    base, cnt = _threshold_search(count_ge, float(n_keep), shape)

    @pl.when(jnp.max(cnt) > float(n_keep))
    def _():
        need = float(n_keep) - count_ge(base + 1)
        ones = jnp.ones((LANES, LANES), BF16)
        for q in range(n_q):
            k = keys_ref[q]
            eq = jnp.logical_and(k == base[q], cnt[q] > float(n_keep))
            eqb = jnp.where(eq, 1.0, 0.0).astype(BF16)
            within = jnp.dot(eqb, tri_ref[...], preferred_element_type=F32)
            rowcnt = jnp.dot(eqb, ones, preferred_element_type=F32).astype(BF16)
            before = jnp.dot(lin_ref[...], rowcnt, preferred_element_type=F32) - rowcnt.astype(F32)
            drop = jnp.logical_and(eq, within + before > need[q])
            keys_ref[q] = jnp.where(drop, KEY_INVALID, k)

    lane_c = lax.broadcasted_iota(I32, (ns, ncp), 1).astype(F32)
    slot_c = lax.broadcasted_iota(I32, (ns, ncp), 0).astype(F32)
    slot_l = lax.broadcasted_iota(I32, (ns, LANES), 0).astype(F32)
    ones8 = jnp.ones((SUBLANES, LANES), BF16)
    lane8 = lax.broadcasted_iota(I32, (SUBLANES, LANES), 1).astype(BF16)
    for q in range(n_q):
        selb = jnp.where(keys_ref[q] >= base[q], 1.0, 0.0).astype(BF16)
        seln_ref[0, q:q + 1, :] = selb[n_pages:n_pages + 1, :].astype(F32)
        within = jnp.dot(selb, tri_ref[...], preferred_element_type=F32)
        upto = jnp.dot(lin_ref[...], selb, preferred_element_type=F32).astype(BF16)
        pin_row = lax.dot_general(ones8, upto, NT_DIMS, preferred_element_type=F32)[0:1, :]
        chunk_of = jnp.sum(jnp.where(pin_row <= slot_c, 1.0, 0.0), axis=1, keepdims=True)
        onehot = jnp.where(lane_c == chunk_of, 1.0, 0.0).astype(BF16)
        g_sel = jnp.dot(onehot, selb, preferred_element_type=F32)
        g_within = jnp.dot(onehot, within.astype(BF16), preferred_element_type=F32)
        g_upto = jnp.dot(onehot, upto, preferred_element_type=F32)
        g_pin = jnp.sum(g_upto, axis=1, keepdims=True)
        g_cnt = jnp.sum(g_sel, axis=1, keepdims=True)
        target = slot_l - (g_pin - g_cnt) + 1.0
        match = jnp.logical_and(g_sel > 0.5, g_within == target)
        matchb = jnp.where(match, 1.0, 0.0).astype(BF16)
        row_l = lax.dot_general(lane8, matchb, NT_DIMS, preferred_element_type=F32)
        chunk_b = jnp.broadcast_to(chunk_of, (ns, LANES))
        row_c = lax.dot_general(ones8, jnp.where(match, chunk_b, 0.0).astype(BF16), NT_DIMS,
                                preferred_element_type=F32)
        idx_ref[0, q:q + 1, :] = (row_c * float(LANES) + row_l)[0:1, :].astype(I32)
    for q in range(n_q, SUBLANES):
        idx_ref[0, q:q + 1, :] = jnp.zeros((1, ns), I32)
        seln_ref[0, q:q + 1, :] = jnp.zeros((1, LANES), F32)


def _sample_index(page_table, qi_rows, wi_rows, ki_new, tri, cache_ik, n_keep):
    nb, n_pages = page_table.shape
    rows = qi_rows.shape[1]
    n_q = rows // IDX_HEADS
    ppc = 8 if n_pages % 8 == 0 else 1
    ncs = _ceil_to(n_pages + 1, SUBLANES)
    ncp = _ceil_to(n_pages + 1, LANES)
    ns = _ceil_to(n_keep, LANES)
    assert ncp <= 256 and ns <= 256, "prefix counts are carried in bf16, exact up to 256"
    ar = jnp.arange(ncp)
    lin = (ar[None, :] <= ar[:, None]).astype(BF16)
    kern = functools.partial(_sidx_kernel, n_keep=n_keep, n_pages=n_pages, n_q=n_q, ppc=ppc,
                             ncs=ncs, ncp=ncp, ns=ns)
    return pl.pallas_call(
        kern,
        grid_spec=pltpu.PrefetchScalarGridSpec(
            num_scalar_prefetch=1,
            grid=(nb,),
            in_specs=[
                pl.BlockSpec((1, rows, IDX_DIM), lambda b, pt: (b, 0, 0)),
                pl.BlockSpec((1, rows, 1), lambda b, pt: (b, 0, 0)),
                pl.BlockSpec((1, LANES, IDX_DIM), lambda b, pt: (b, 0, 0)),
                pl.BlockSpec((LANES, LANES), lambda b, pt: (0, 0)),
                pl.BlockSpec((ncp, ncp), lambda b, pt: (0, 0)),
                pl.BlockSpec(memory_space=pl.ANY),
            ],
            out_specs=[pl.BlockSpec((1, SUBLANES, ns), lambda b, pt: (b, 0, 0)),
                       pl.BlockSpec((1, SUBLANES, LANES), lambda b, pt: (b, 0, 0))],
            scratch_shapes=[
                pltpu.VMEM((2, n_pages, PAGE_SIZE, IDX_DIM), F32),
                pltpu.SemaphoreType.DMA((2,)),
                pltpu.VMEM((n_q, ncp, LANES), I32),
            ],
        ),
        out_shape=[jax.ShapeDtypeStruct((nb, SUBLANES, ns), I32),
                   jax.ShapeDtypeStruct((nb, SUBLANES, LANES), F32)],
        compiler_params=_cparams(("arbitrary",)),
        name="sample_index",
    )(page_table, qi_rows, wi_rows, ki_new, tri, lin, cache_ik)


def _sgather_kernel(pt_ref, idx_smem, idx_ref, seln_ref, q8_ref, kn_ref, vn_ref, rbt_ref, bnew_ref, ck_ref, cv_ref,
                    o_ref, kbuf, vbuf, sem, *, n_keep, n_q, ns, past):
    b = pl.program_id(0)
    nb = pl.num_programs(0)
    slot = b % 2

    def start_all(bb, sl):
        for q in range(n_q):
            def body(s, carry):
                pos = jnp.minimum(idx_smem[bb * n_q + q, s], past - 1)
                page = pt_ref[bb, lax.shift_right_logical(pos, PAGE_SHIFT)]
                off = lax.bitwise_and(pos, PAGE_SIZE - 1)
                pltpu.make_async_copy(ck_ref.at[0, page, off], kbuf.at[sl, :, q * ns + s], sem.at[0, sl]).start()
                pltpu.make_async_copy(cv_ref.at[0, page, off], vbuf.at[sl, :, q * ns + s], sem.at[1, sl]).start()
                return carry
            lax.fori_loop(0, n_keep, body, 0, unroll=GATHER_UNROLL if n_keep % GATHER_UNROLL == 0 else 1)

    def wait_all(sl):
        rows = pl.ds(0, n_q * n_keep)
        pltpu.make_async_copy(kbuf.at[sl, :, rows], kbuf.at[sl, :, rows], sem.at[0, sl]).wait()
        pltpu.make_async_copy(vbuf.at[sl, :, rows], vbuf.at[sl, :, rows], sem.at[1, sl]).wait()

    @pl.when(b == 0)
    def _():
        if ns > n_keep:
            kbuf[...] = jnp.zeros(kbuf.shape, F32)
            vbuf[...] = jnp.zeros(vbuf.shape, F32)
        start_all(b, slot)

    @pl.when(b + 1 < nb)
    def _():
        start_all(b + 1, 1 - slot)

    wait_all(slot)

    lane = lax.broadcasted_iota(I32, (1, ns), 1)
    for q in range(n_q):
        pos = idx_ref[0, q:q + 1, :]
        live = jnp.logical_and(lane < n_keep, pos < past)
        bias = _bias_of(jnp.broadcast_to(past + q - pos, (N_HEADS, ns)), rbt_ref[...])
        s = jnp.zeros((N_HEADS, ns), F32)
        s_new = jnp.zeros((N_HEADS, LANES), F32)
        for h in range(N_HEADS):
            qh = q8_ref[0, q, h]
            kh = kbuf[slot, h, pl.ds(q * ns, ns), :].astype(BF16)
            s = s + lax.dot_general(qh, kh, NT_DIMS, preferred_element_type=F32)
            s_new = s_new + lax.dot_general(qh, kn_ref[0, h], NT_DIMS, preferred_element_type=F32)
        s = jnp.where(live, s + bias, NEG_BIG)
        s_new = jnp.where(seln_ref[0, q:q + 1, :] > 0.5, s_new + bnew_ref[q], NEG_BIG)
        m = jnp.maximum(jnp.max(s, axis=1, keepdims=True), jnp.max(s_new, axis=1, keepdims=True))
        p = jnp.exp(s - m)
        p_new = jnp.exp(s_new - m)
        denom = jnp.sum(p, axis=1, keepdims=True) + jnp.sum(p_new, axis=1, keepdims=True)
        pb = p.astype(BF16)
        pnb = p_new.astype(BF16)
        for h in range(N_HEADS):
            vh = vbuf[slot, h, pl.ds(q * ns, ns), :].astype(BF16)
            o8 = (jnp.dot(pb, vh, preferred_element_type=F32)
                  + jnp.dot(pnb, vn_ref[0, h], preferred_element_type=F32)) / denom
            o_ref[0, q:q + 1, h * HEAD_DIM:(h + 1) * HEAD_DIM] = o8[h:h + 1, :]


def _sample_attention(page_table, idx, seln, q8, knh, vnh, rbt, bnew, ck, cv, n_keep, past):
    nb = page_table.shape[0]
    n_q = q8.shape[1]
    ns = idx.shape[2]
    kern = functools.partial(_sgather_kernel, n_keep=n_keep, n_q=n_q, ns=ns, past=past)
    return pl.pallas_call(
        kern,
        grid_spec=pltpu.PrefetchScalarGridSpec(
            num_scalar_prefetch=2,
            grid=(nb,),
            in_specs=[
                pl.BlockSpec((1, SUBLANES, ns), lambda b, pt, ix: (b, 0, 0)),
                pl.BlockSpec((1, SUBLANES, LANES), lambda b, pt, ix: (b, 0, 0)),
                pl.BlockSpec((1, n_q, N_HEADS, SUBLANES, HEAD_DIM), lambda b, pt, ix: (b, 0, 0, 0, 0)),
                pl.BlockSpec((1, N_HEADS, LANES, HEAD_DIM), lambda b, pt, ix: (b, 0, 0, 0)),
                pl.BlockSpec((1, N_HEADS, LANES, HEAD_DIM), lambda b, pt, ix: (b, 0, 0, 0)),
                pl.BlockSpec((N_HEADS, N_BUCKETS), lambda b, pt, ix: (0, 0)),
                pl.BlockSpec((n_q, N_HEADS, LANES), lambda b, pt, ix: (0, 0, 0)),
                pl.BlockSpec(memory_space=pl.ANY),
                pl.BlockSpec(memory_space=pl.ANY),
            ],
            out_specs=pl.BlockSpec((1, n_q, D_ATTN), lambda b, pt, ix: (b, 0, 0)),
            scratch_shapes=[
                pltpu.VMEM((2, N_HEADS, n_q * ns, HEAD_DIM), F32),
                pltpu.VMEM((2, N_HEADS, n_q * ns, HEAD_DIM), F32),
                pltpu.SemaphoreType.DMA((2, 2)),
            ],
        ),
        out_shape=jax.ShapeDtypeStruct((nb, n_q, D_ATTN), F32),
        compiler_params=_cparams(("arbitrary",)),
        name="sample_attention",
    )(page_table, idx[:, :n_q].reshape(nb * n_q, ns), idx, seln, q8, knh, vnh, rbt, bnew, ck, cv)


def _merge_kernel(o_ref, ya_ref, ga_ref, gb_ref, x_ref, wb_ref, wo_ref, x1_ref):
    yb = jnp.dot(o_ref[...].astype(BF16), wb_ref[...], preferred_element_type=F32)
    m = _sigmoid(ga_ref[...]) * ya_ref[...] + _sigmoid(gb_ref[...]) * yb
    x1_ref[...] = x_ref[...] + jnp.dot(m.astype(BF16), wo_ref[...], preferred_element_type=F32)


def _merge(o2d, ya, gates, x2d, wb, wo, tm):
    n = x2d.shape[0]
    return pl.pallas_call(
        _merge_kernel,
        grid=(n // tm,),
        in_specs=[
            pl.BlockSpec((tm, D_ATTN), lambda i: (i, 0)),
            pl.BlockSpec((tm, D_MODEL), lambda i: (i, 0)),
            pl.BlockSpec((tm, D_MODEL), lambda i: (i, 0)),
            pl.BlockSpec((tm, D_MODEL), lambda i: (i, 1)),
            pl.BlockSpec((tm, D_MODEL), lambda i: (i, 0)),
            pl.BlockSpec((D_ATTN, D_MODEL), lambda i: (0, 0)),
            pl.BlockSpec((D_MODEL, D_MODEL), lambda i: (0, 0)),
        ],
        out_specs=pl.BlockSpec((tm, D_MODEL), lambda i: (i, 0)),
        out_shape=jax.ShapeDtypeStruct((n, D_MODEL), F32),
        compiler_params=_cparams(("parallel",)),
        name="merge_out_proj",
    )(o2d, ya, gates, gates, x2d, wb, wo)


def _ffn_kernel(x_ref, g_ref, wua_ref, wub_ref, fcw_ref, fcb_ref, past_ref, wd_ref, y_ref, tail_ref,
                xn_ref, carry_ref, ext_ref, acc_ref, *, stride, tm, pad, tail_loc):
    t = pl.program_id(1)
    f = pl.program_id(2)
    nf = pl.num_programs(2)

    @pl.when(f == 0)
    def _():
        x = x_ref[...]
        ms = jnp.mean(x * x, axis=-1, keepdims=True)
        xn_ref[...] = (x * lax.rsqrt(ms + EPS) * g_ref[...]).astype(BF16)
        acc_ref[...] = jnp.zeros(acc_ref.shape, F32)

    a = jnp.dot(xn_ref[...], wua_ref[...], preferred_element_type=F32)
    bq = jnp.dot(xn_ref[...], wub_ref[...], preferred_element_type=F32)

    @pl.when(t == 0)
    def _():
        ext_ref[0:pad, :] = past_ref[0]

    @pl.when(t > 0)
    def _():
        ext_ref[0:pad, :] = carry_ref[f]

    ext_ref[pad:pad + tm, :] = a
    carry_ref[f] = ext_ref[tm:tm + pad, :]
    tail_ref[0, 0] = ext_ref[tail_loc:tail_loc + pad, :]

    conv = (ext_ref[pad - 2 * stride:pad - 2 * stride + tm, :] * fcw_ref[0:1, :]
            + ext_ref[pad - stride:pad - stride + tm, :] * fcw_ref[1:2, :]
            + a * fcw_ref[2:3, :] + fcb_ref[...])
    h = conv * _sigmoid(conv) * bq
    acc_ref[...] += jnp.dot(h.astype(BF16), wd_ref[...], preferred_element_type=F32)

    @pl.when(f == nf - 1)
    def _():
        y_ref[...] = x_ref[...] + acc_ref[...]


def _ffn(x1, g, wup, fcw, fcb, past, wd, *, nb, nt, tm, stride, tf, rows_real):
    pad = past.shape[1]
    nf = D_FF // tf
    tail_tile, tail_loc = _tail_position(rows_real, tm)
    kern = functools.partial(_ffn_kernel, stride=stride, tm=tm, pad=pad, tail_loc=tail_loc)
    y, tail = pl.pallas_call(
        kern,
        grid=(nb, nt, nf),
        in_specs=[
            pl.BlockSpec((tm, D_MODEL), lambda b, t, f: (b * nt + t, 0)),
            pl.BlockSpec((1, D_MODEL), lambda b, t, f: (0, 0)),
            pl.BlockSpec((D_MODEL, tf), lambda b, t, f: (0, f)),
            pl.BlockSpec((D_MODEL, tf), lambda b, t, f: (0, nf + f)),
            pl.BlockSpec((SUBLANES, tf), lambda b, t, f: (0, f)),
            pl.BlockSpec((1, tf), lambda b, t, f: (0, f)),
            pl.BlockSpec((1, pad, tf), lambda b, t, f: (b, 0, f)),
            pl.BlockSpec((tf, D_MODEL), lambda b, t, f: (f, 0)),
        ],
        out_specs=[
            pl.BlockSpec((tm, D_MODEL), lambda b, t, f: (b * nt + t, 0)),
            pl.BlockSpec((1, 1, pad, tf), lambda b, t, f: (b, t, 0, f)),
        ],
        out_shape=[jax.ShapeDtypeStruct((nb * nt * tm, D_MODEL), F32),
                   jax.ShapeDtypeStruct((nb, nt, pad, D_FF), F32)],
        scratch_shapes=[
            pltpu.VMEM((tm, D_MODEL), BF16),
            pltpu.VMEM((nf, pad, tf), F32),
            pltpu.VMEM((pad + tm, tf), F32),
            pltpu.VMEM((tm, D_MODEL), F32),
        ],
        compiler_params=_cparams(("parallel", "arbitrary", "arbitrary")),
        name="conv_ffn",
    )(x1, g, wup, wup, fcw, fcb, past, wd)
    return y, tail[:, tail_tile]


def _row_tile(n, cap):
    best = SUBLANES
    for cand in range(SUBLANES, cap + 1, SUBLANES):
        if n % cand == 0:
            best = cand
    return best


def kernel(x_prompt, x_sample, cache_k, cache_v, cache_idx_k, page_table, state_conv, state_ffn_conv, meta_tokens, g_attn_norm, w_in, conv_w, conv_b, conv_ln_g, conv_ln_b, w_a_out, q_norm_g, k_norm_g, rel_bias, w_b_out, w_o, g_ffn_norm, w_up, ffn_conv_w, ffn_conv_b, w_down):
    nbp, seq, _ = x_prompt.shape
    nbs, n_q, _ = x_sample.shape
    n_pages = page_table.shape[1]
    past_len = n_pages * PAGE_SIZE
    t_real = seq + N_META
    tp = _ceil_to(t_real, LANES)
    keep_p = min(TOP_K_MAX, t_real // 4)
    keep_s = min(TOP_K_MAX, (past_len + n_q) // 4)
    assert g_attn_norm.shape[0] == 1, "single trunk layer"

    w = w_in[0]
    c_q, c_v, c_qi, c_ki, c_wi, c_ga = 1024, 2048, 2560, 3072, 3136, 3144
    wm = jnp.concatenate([w[:, :c_ki], w[:, c_ga:]], axis=1).astype(BF16)
    wt = jnp.stack([w[:, c:c + D_ATTN].T for c in (c_q, c_v, c_qi)]).astype(BF16)
    ws = jnp.concatenate([w[:, c_ki:c_wi], jnp.zeros((D_MODEL, LANES - IDX_DIM), F32)], axis=1).astype(BF16)
    wst = w[:, c_wi:c_ga].T.astype(BF16)
    g_attn = g_attn_norm[0][None, :]
    gq = (jnp.tile(q_norm_g[0], N_HEADS) * ATTN_SCALE)[:, None]
    gk = jnp.tile(k_norm_g[0], N_HEADS)[None, :]
    hid = jnp.arange(D_ATTN) // HEAD_DIM
    seg = jnp.where(hid[:, None] == hid[None, :], 1.0 / HEAD_DIM, 0.0).astype(BF16)
    cw = jnp.concatenate([conv_w[0], jnp.zeros((32 - CONV_W, D_CONV), F32)], axis=0)
    cb, lg, lb = conv_b[0][None, :], conv_ln_g[0][None, :], conv_ln_b[0][None, :]
    wa = w_a_out[0].astype(BF16)
    wb = w_b_out[0].astype(BF16)
    wo = w_o[0].astype(BF16)
    g_ffn = g_ffn_norm[0][None, :]
    wup = w_up[0].astype(BF16)
    fcw = jnp.concatenate([ffn_conv_w[0], jnp.zeros((SUBLANES - FFN_CONV_W, D_FF), F32)], axis=0)
    fcb = ffn_conv_b[0][None, :]
    wd = w_down[0].astype(BF16)
    rbt = rel_bias.T
    ar = jnp.arange(LANES)
    tril = (ar[None, :] <= ar[:, None]).astype(BF16)
    tri = tril.T

    def project(x2d):
        return _in_proj(x2d, g_attn, wm, wt, ws, wst, gq, gk, seg, _row_tile(x2d.shape[0], 1024))

    xp = jnp.concatenate([jnp.broadcast_to(meta_tokens[None], (nbp, N_META, D_MODEL)), x_prompt,
                          jnp.zeros((nbp, tp - t_real, D_MODEL), F32)], axis=1).reshape(nbp * tp, D_MODEL)
    n_p = nbp * tp
    a_p, k_p, kb_p, v_p, qt_p, vt_p, qit_p, g2_p, ki_p, kib_p, wit_p = project(xp)

    tm_p = _row_tile(tp, 640)
    nt_p = tp // tm_p
    ya_p, ctail_p = _conv_branch(a_p, jnp.zeros((nbp, 32, D_CONV), F32), cw, cb, lg, lb, wa,
                                 nb=nbp, nt=nt_p, tm=tm_p, stride=1, rows_real=t_real)
    o_p = _prompt_attention(qt_p, kb_p, vt_p, qit_p, kib_p, wit_p, _prompt_bias(rbt), tril, keep_p, nbp, tp)
    x1_p = _merge(o_p, ya_p, g2_p, xp, wb, wo, _row_tile(n_p, 512))
    y_p, ftail_p = _ffn(x1_p, g_ffn, wup, fcw, fcb, jnp.zeros((nbp, SUBLANES, D_FF), F32), wd,
                        nb=nbp, nt=nt_p, tm=tm_p, stride=1, tf=FFN_TILE, rows_real=t_real)

    y_prompt = y_p.reshape(nbp, tp, D_MODEL)[:, N_META:t_real]
    p_k = k_p.reshape(nbp, tp, N_HEADS, HEAD_DIM)[None, :, :t_real]
    p_v = v_p.reshape(nbp, tp, N_HEADS, HEAD_DIM)[None, :, :t_real]
    p_ik = ki_p.reshape(nbp, tp, IDX_DIM)[None, :, :t_real]
    p_cv = ctail_p[None, :, 32 - (CONV_W - 1):]
    p_fc = ftail_p[None, :, SUBLANES - (FFN_CONV_W - 1):]

    n_s = nbs * n_q
    xs = x_sample.transpose(1, 0, 2).reshape(n_s, D_MODEL)
    a_s, k_s2, _, v_s2, qt_s, _, qit_s, g2_s, ki_s2, _, wit_s = project(xs)

    pad_c = _ceil_to((CONV_W - 1) * nbs, SUBLANES)
    past_c = state_conv[0].transpose(1, 0, 2).reshape(1, (CONV_W - 1) * nbs, D_CONV)
    past_c = jnp.pad(past_c, ((0, 0), (pad_c - (CONV_W - 1) * nbs, 0), (0, 0)))
    ya_s, ctail_s = _conv_branch(a_s, past_c, cw, cb, lg, lb, wa, nb=1, nt=1, tm=n_s, stride=nbs,
                                 rows_real=n_s)

    def batch_major(x2d, *tail):
        return x2d.reshape((n_q, nbs) + tail).transpose((1, 0) + tuple(range(2, 2 + len(tail))))

    q_s = batch_major(qt_s.T.astype(F32), N_HEADS, HEAD_DIM)
    k_s = batch_major(k_s2, N_HEADS, HEAD_DIM)
    v_s = batch_major(v_s2, N_HEADS, HEAD_DIM)
    qi_s = batch_major(qit_s.T, IDX_HEADS, IDX_DIM)
    ki_s = batch_major(ki_s2, IDX_DIM)
    wi_s = batch_major(wit_s.T, IDX_HEADS)
    rows = n_q * N_HEADS

    qi_rows = qi_s.reshape(nbs, rows, IDX_DIM)
    wi_rows = wi_s.reshape(nbs, rows, 1)
    ki_new = jnp.pad(ki_s, ((0, 0), (0, LANES - n_q), (0, 0))).astype(BF16)
    idx_s, seln = _sample_index(page_table, qi_rows, wi_rows, ki_new, tri, cache_idx_k, keep_s)

    eye = jnp.eye(N_HEADS, dtype=F32)
    q8 = (q_s[:, :, :, None, :] * eye[None, None, :, :, None]).astype(BF16)
    knh = jnp.pad(k_s.transpose(0, 2, 1, 3), ((0, 0), (0, 0), (0, LANES - n_q), (0, 0))).astype(BF16)
    vnh = jnp.pad(v_s.transpose(0, 2, 1, 3), ((0, 0), (0, 0), (0, LANES - n_q), (0, 0))).astype(BF16)
    o_s = _sample_attention(page_table, idx_s, seln, q8, knh, vnh, rbt, _sample_new_bias(rbt, n_q),
                            cache_k, cache_v, keep_s, past_len)
    o_s2d = o_s.transpose(1, 0, 2).reshape(n_s, D_ATTN)

    x1_s = _merge(o_s2d, ya_s, g2_s, xs, wb, wo, _row_tile(n_s, 512))
    pad_f = _ceil_to((FFN_CONV_W - 1) * nbs, SUBLANES)
    past_f = state_ffn_conv[0].transpose(1, 0, 2).reshape(1, (FFN_CONV_W - 1) * nbs, D_FF)
    past_f = jnp.pad(past_f, ((0, 0), (pad_f - (FFN_CONV_W - 1) * nbs, 0), (0, 0)))
    y_s, ftail_s = _ffn(x1_s, g_ffn, wup, fcw, fcb, past_f, wd, nb=1, nt=1, tm=n_s, stride=nbs, tf=FFN_TILE,
                        rows_real=n_s)

    y_sample = y_s.reshape(n_q, nbs, D_MODEL).transpose(1, 0, 2)
    s_cv = ctail_s[0, pad_c - (CONV_W - 1) * nbs:].reshape(CONV_W - 1, nbs, D_CONV).transpose(1, 0, 2)[None]
    s_fc = ftail_s[0, pad_f - (FFN_CONV_W - 1) * nbs:].reshape(FFN_CONV_W - 1, nbs, D_FF).transpose(1, 0, 2)[None]

    return (y_prompt, y_sample, p_k, p_v, p_ik, p_cv, p_fc,
            k_s[None], v_s[None], ki_s[None], s_cv, s_fc)
```

```python
import functools
import math

import jax
import jax.numpy as jnp
from jax import lax
from jax.experimental import pallas as pl
from jax.experimental.pallas import tpu as pltpu

F32 = jnp.float32
BF16 = jnp.bfloat16
I32 = jnp.int32

D_MODEL = 1024
D_CONV = D_MODEL // 2
CONV_W = 31
N_HEADS = 8
HEAD_DIM = 64
D_ATTN = N_HEADS * HEAD_DIM
IDX_HEADS = 8
IDX_DIM = 64
TOP_K_MAX = 256
N_BUCKETS = 32
MAX_EXACT = N_BUCKETS // 2
REL_MAX_DIST = 128
D_FF = 2816
FFN_CONV_W = 3
N_META = 16
PAGE_SIZE = 128
PAGE_SHIFT = 7
GATHER_UNROLL = 8
FFN_TILE = D_FF // 2
EPS = 1e-6
ATTN_SCALE = HEAD_DIM ** -0.5
IDX_SCALE = (IDX_HEADS * IDX_DIM) ** -0.5
D_MAIN = 2 * D_CONV + 4 * D_ATTN + 2 * D_MODEL
JB_Q, JB_K, JB_V, JB_QI, JB_G = 2, 3, 4, 5, 6

LANES = 128
SUBLANES = 8
KEY_INVALID = -2 ** 31
NEG_BIG = -1e30
VMEM_LIMIT = 56 * 1024 * 1024
NT_DIMS = (((1,), (1,)), ((), ()))


def _cparams(sem):
    return pltpu.CompilerParams(dimension_semantics=sem, vmem_limit_bytes=VMEM_LIMIT)


def _sigmoid(x):
    return 1.0 / (1.0 + jnp.exp(-x))


def _ceil_to(x, m):
    return -(-x // m) * m


def _split_bf16(x):
    hi = x.astype(BF16)
    return hi, (x - hi.astype(F32)).astype(BF16)


def _in_proj_kernel(x_ref, g_ref, wm_ref, wt_ref, ws_ref, wst_ref, gq_ref, gk_ref, seg_ref,
                    a_ref, k_ref, kb_ref, v_ref, qt_ref, vt_ref, qit_ref, g2_ref, ki_ref, kib_ref, wit_ref,
                    xn_ref):
    j = pl.program_id(1)

    @pl.when(j == 0)
    def _():
        x = x_ref[...]
        ms = jnp.mean(x * x, axis=-1, keepdims=True)
        xn_ref[...] = (x * lax.rsqrt(ms + EPS) * g_ref[...]).astype(BF16)
        ki = jnp.dot(xn_ref[...], ws_ref[...], preferred_element_type=F32)[:, :IDX_DIM]
        ki_ref[...] = ki
        kib_ref[...] = ki.astype(BF16)
        wit_ref[...] = lax.dot_general(wst_ref[...], xn_ref[...], NT_DIMS, preferred_element_type=F32)

    @pl.when(j < JB_Q)
    def _():
        a_ref[...] = jnp.dot(xn_ref[...], wm_ref[...], preferred_element_type=F32)

    @pl.when(j >= JB_G)
    def _():
        g2_ref[...] = jnp.dot(xn_ref[...], wm_ref[...], preferred_element_type=F32)

    @pl.when(j == JB_Q)
    def _():
        yt = lax.dot_general(wt_ref[0], xn_ref[...], NT_DIMS, preferred_element_type=F32)
        hi, lo = _split_bf16(yt * yt)
        ms = (jnp.dot(seg_ref[...], hi, preferred_element_type=F32)
              + jnp.dot(seg_ref[...], lo, preferred_element_type=F32))
        qt_ref[...] = (yt * lax.rsqrt(ms + EPS) * gq_ref[...]).astype(BF16)

    @pl.when(j == JB_K)
    def _():
        y = jnp.dot(xn_ref[...], wm_ref[...], preferred_element_type=F32)
        hi, lo = _split_bf16(y * y)
        ms = (jnp.dot(hi, seg_ref[...], preferred_element_type=F32)
              + jnp.dot(lo, seg_ref[...], preferred_element_type=F32))
        k = y * lax.rsqrt(ms + EPS) * gk_ref[...]
        k_ref[...] = k
        kb_ref[...] = k.astype(BF16)

    @pl.when(j == JB_V)
    def _():
        v_ref[...] = jnp.dot(xn_ref[...], wm_ref[...], preferred_element_type=F32)
        vt_ref[...] = lax.dot_general(wt_ref[1], xn_ref[...], NT_DIMS, preferred_element_type=F32).astype(BF16)

    @pl.when(j == JB_QI)
    def _():
        qit_ref[...] = lax.dot_general(wt_ref[2], xn_ref[...], NT_DIMS, preferred_element_type=F32).astype(BF16)


def _in_proj(x2d, g, wm, wt, ws, wst, gq, gk, seg, tm):
    n = x2d.shape[0]
    tn = D_ATTN
    nj = D_MAIN // tn
    row = lambda i, j: (i, 0)
    colt = lambda i, j: (0, i)
    const2 = lambda i, j: (0, 0)
    out_shape = [
        jax.ShapeDtypeStruct((n, 2 * D_CONV), F32),
        jax.ShapeDtypeStruct((n, D_ATTN), F32),
        jax.ShapeDtypeStruct((n, D_ATTN), BF16),
        jax.ShapeDtypeStruct((n, D_ATTN), F32),
        jax.ShapeDtypeStruct((D_ATTN, n), BF16),
        jax.ShapeDtypeStruct((D_ATTN, n), BF16),
        jax.ShapeDtypeStruct((D_ATTN, n), BF16),
        jax.ShapeDtypeStruct((n, 2 * D_MODEL), F32),
        jax.ShapeDtypeStruct((n, IDX_DIM), F32),
        jax.ShapeDtypeStruct((n, IDX_DIM), BF16),
        jax.ShapeDtypeStruct((IDX_HEADS, n), F32),
    ]
    out_specs = [
        pl.BlockSpec((tm, tn), lambda i, j: (i, jnp.minimum(j, 1))),
        pl.BlockSpec((tm, tn), row),
        pl.BlockSpec((tm, tn), row),
        pl.BlockSpec((tm, tn), row),
        pl.BlockSpec((tn, tm), colt),
        pl.BlockSpec((tn, tm), colt),
        pl.BlockSpec((tn, tm), colt),
        pl.BlockSpec((tm, tn), lambda i, j: (i, jnp.clip(j - JB_G, 0, nj - JB_G - 1))),
        pl.BlockSpec((tm, IDX_DIM), row),
        pl.BlockSpec((tm, IDX_DIM), row),
        pl.BlockSpec((IDX_HEADS, tm), colt),
    ]
    return pl.pallas_call(
        _in_proj_kernel,
        grid=(n // tm, nj),
        in_specs=[
            pl.BlockSpec((tm, D_MODEL), row),
            pl.BlockSpec((1, D_MODEL), const2),
            pl.BlockSpec((D_MODEL, tn), lambda i, j: (0, j)),
            pl.BlockSpec((3, tn, D_MODEL), lambda i, j: (0, 0, 0)),
            pl.BlockSpec((D_MODEL, LANES), const2),
            pl.BlockSpec((IDX_HEADS, D_MODEL), const2),
            pl.BlockSpec((D_ATTN, 1), const2),
            pl.BlockSpec((1, D_ATTN), const2),
            pl.BlockSpec((D_ATTN, D_ATTN), const2),
        ],
        out_specs=out_specs,
        out_shape=out_shape,
        scratch_shapes=[pltpu.VMEM((tm, D_MODEL), BF16)],
        compiler_params=_cparams(("parallel", "arbitrary")),
        name="in_proj",
    )(x2d, g, wm, wt, ws, wst, gq, gk, seg)


def _conv_kernel(a_ref, past_ref, cw_ref, cb_ref, lg_ref, lb_ref, wa_ref, ya_ref, tail_ref, ext_ref, h_ref,
                 *, stride, tm, pad, rc, tail_tile, tail_loc):
    t = pl.program_id(1)

    @pl.when(t == 0)
    def _():
        ext_ref[0:pad, :] = past_ref[0]

    @pl.when(t > 0)
    def _():
        ext_ref[0:pad, :] = ext_ref[tm:tm + pad, :]

    a = a_ref[...]
    ext_ref[pad:pad + tm, :] = a[:, :D_CONV] * _sigmoid(a[:, D_CONV:])

    @pl.when(t <= tail_tile)
    def _():
        tail_ref[0] = ext_ref[tail_loc:tail_loc + pad, :]

    cb = cb_ref[...]
    lg = lg_ref[...]
    lb = lb_ref[...]
    for r0 in range(0, tm, rc):
        acc = jnp.zeros((rc, D_CONV), F32) + cb
        for w in range(CONV_W):
            off = pad - (CONV_W - 1 - w) * stride + r0
            acc = acc + ext_ref[off:off + rc, :] * cw_ref[w:w + 1, :]
        mu = jnp.mean(acc, axis=-1, keepdims=True)
        d = acc - mu
        var = jnp.mean(d * d, axis=-1, keepdims=True)
        h = d * lax.rsqrt(var + EPS) * lg + lb
        h_ref[r0:r0 + rc, :] = (h * _sigmoid(h)).astype(BF16)
    ya_ref[...] = jnp.dot(h_ref[...], wa_ref[...], preferred_element_type=F32)


def _tail_position(rows_real, tm):
    tail_tile = (rows_real - 1) // tm
    return tail_tile, rows_real - tail_tile * tm


def _conv_branch(a_in, past, cw, cb, lg, lb, wa, *, nb, nt, tm, stride, rows_real):
    pad = past.shape[1]
    rc = 32 if tm % 32 == 0 else SUBLANES
    tail_tile, tail_loc = _tail_position(rows_real, tm)
    kern = functools.partial(_conv_kernel, stride=stride, tm=tm, pad=pad, rc=rc,
                             tail_tile=tail_tile, tail_loc=tail_loc)
    return pl.pallas_call(
        kern,
        grid=(nb, nt),
        in_specs=[
            pl.BlockSpec((tm, 2 * D_CONV), lambda b, t: (b * nt + t, 0)),
            pl.BlockSpec((1, pad, D_CONV), lambda b, t: (b, 0, 0)),
            pl.BlockSpec((32, D_CONV), lambda b, t: (0, 0)),
            pl.BlockSpec((1, D_CONV), lambda b, t: (0, 0)),
            pl.BlockSpec((1, D_CONV), lambda b, t: (0, 0)),
            pl.BlockSpec((1, D_CONV), lambda b, t: (0, 0)),
            pl.BlockSpec((D_CONV, D_MODEL), lambda b, t: (0, 0)),
        ],
        out_specs=[
            pl.BlockSpec((tm, D_MODEL), lambda b, t: (b * nt + t, 0)),
            pl.BlockSpec((1, pad, D_CONV), lambda b, t: (b, 0, 0)),
        ],
        out_shape=[jax.ShapeDtypeStruct((nb * nt * tm, D_MODEL), F32),
                   jax.ShapeDtypeStruct((nb, pad, D_CONV), F32)],
        scratch_shapes=[pltpu.VMEM((pad + tm, D_CONV), F32), pltpu.VMEM((tm, D_CONV), BF16)],
        compiler_params=_cparams(("parallel", "arbitrary")),
        name="conv_branch",
    )(a_in, past, cw, cb, lg, lb, wa)


def _rel_bucket(rel):
    n = jnp.maximum(rel, 0)
    nf = jnp.maximum(n, 1).astype(F32)
    large = MAX_EXACT + (jnp.log(nf / MAX_EXACT) / math.log(REL_MAX_DIST / MAX_EXACT)
                         * (N_BUCKETS - MAX_EXACT)).astype(I32)
    large = jnp.minimum(large, N_BUCKETS - 1)
    return jnp.where(n < MAX_EXACT, n, large)


def _bias_of(rel, rb_rows):
    bucket = _rel_bucket(rel)
    out = jnp.zeros(rel.shape, F32)
    for b in range(N_BUCKETS):
        out = jnp.where(bucket == b, rb_rows[:, b:b + 1], out)
    return out


def _prompt_bias_kernel(rbt_ref, b1_ref):
    key = lax.broadcasted_iota(I32, (LANES, LANES), 0)
    qry = lax.broadcasted_iota(I32, (LANES, LANES), 1)
    for h in range(N_HEADS):
        for d in range(3):
            b1_ref[h * 3 + d] = _bias_of(d * LANES + qry - key, rbt_ref[h:h + 1, :])


def _prompt_bias(rbt):
    return pl.pallas_call(
        _prompt_bias_kernel,
        out_shape=jax.ShapeDtypeStruct((N_HEADS * 3, LANES, LANES), F32),
        name="prompt_bias",
    )(rbt)


def _sample_new_bias_kernel(rbt_ref, o_ref, *, n_q):
    col = lax.broadcasted_iota(I32, (N_HEADS, LANES), 1)
    for q in range(n_q):
        o_ref[q] = _bias_of(q - col, rbt_ref[...])


def _sample_new_bias(rbt, n_q):
    return pl.pallas_call(
        functools.partial(_sample_new_bias_kernel, n_q=n_q),
        out_shape=jax.ShapeDtypeStruct((n_q, N_HEADS, LANES), F32),
        name="sample_bias",
    )(rbt)


def _score_to_key(s, valid):
    bits = pltpu.bitcast(s, I32)
    key = bits ^ ((bits >> 31) & 0x7FFFFFFF)
    key = jnp.where(bits == KEY_INVALID, 0, key)
    return jnp.where(valid, key, KEY_INVALID)


def _threshold_search(count_ge, n_keep, shape):
    cnt_valid = count_ge(jnp.full(shape, KEY_INVALID + 1, I32))
    c0 = count_ge(jnp.zeros(shape, I32))
    ok0 = c0 >= n_keep
    base = jnp.where(ok0, 0, KEY_INVALID).astype(I32)
    cnt = jnp.where(ok0, c0, cnt_valid)

    def body(t, carry):
        base, cnt = carry
        cand = base | lax.shift_left(jnp.int32(1), 30 - t)
        c = count_ge(cand)
        ok = c >= n_keep
        return jnp.where(ok, cand, base), jnp.where(ok, c, cnt)

    base, cnt = lax.fori_loop(0, 31, body, (base, cnt))
    return jnp.maximum(base, KEY_INVALID + 1), cnt


def _pattn_kernel(qt_ref, k_ref, vt_ref, qit_ref, ki_ref, wit_ref, b1_ref, tril_ref, o_ref,
                  keys_ref, qiw_ref, qbd_ref, m_ref, l_ref, acc_ref, ot_ref, *, n_keep):
    i = pl.program_id(1)
    tq = LANES
    row1 = (1, tq)
    lead = (i + 1) % 2
    npair = (i + 1) // 2
    qpos = i * tq + lax.broadcasted_iota(I32, row1, 1)

    def for_chunks(fn):
        @pl.when(lead == 1)
        def _():
            fn(0, LANES)

        def body(j, carry):
            fn(pl.multiple_of(lead * LANES + j * 2 * LANES, LANES), 2 * LANES)
            return carry
        lax.fori_loop(0, npair, body, 0)

    for h in range(IDX_HEADS):
        qiw_ref[:, h * tq:(h + 1) * tq] = qit_ref[h * IDX_DIM:(h + 1) * IDX_DIM, :]
    qbd_ref[...] = jnp.zeros(qbd_ref.shape, BF16)
    for h in range(N_HEADS):
        hp, e = divmod(h, 2)
        qbd_ref[hp, e * HEAD_DIM:(e + 1) * HEAD_DIM, e * tq:(e + 1) * tq] = qt_ref[h * HEAD_DIM:(h + 1) * HEAD_DIM, :]
    wit = wit_ref[...] * IDX_SCALE

    def idx_chunk(off, ck):
        sall = jnp.dot(ki_ref[pl.ds(off, ck), :], qiw_ref[...], preferred_element_type=F32)
        s = jnp.zeros((ck, tq), F32)
        for h in range(IDX_HEADS):
            s = s + wit[h:h + 1, :] * jnp.maximum(sall[:, h * tq:(h + 1) * tq], 0.0)
        kpos = off + lax.broadcasted_iota(I32, (ck, tq), 0)
        keys_ref[pl.ds(off, ck), :] = _score_to_key(s, kpos <= qpos)

    for_chunks(idx_chunk)

    blk = (LANES, tq)
    nblk = i + 1

    def count_ge(cand):
        cb = jnp.broadcast_to(cand, blk)

        def body(j, acc):
            off = pl.multiple_of(j * LANES, LANES)
            return acc + jnp.where(keys_ref[pl.ds(off, LANES), :] >= cb, 1.0, 0.0)
        acc = lax.fori_loop(0, nblk, body, jnp.zeros(blk, F32))
        return jnp.sum(acc, axis=0, keepdims=True)

    base, cnt = _threshold_search(count_ge, float(n_keep), row1)

    @pl.when(jnp.max(cnt) > float(n_keep))
    def _():
        need = float(n_keep) - count_ge(base + 1)
        tie_q = cnt > float(n_keep)

        def body(j, seen):
            off = pl.multiple_of(j * LANES, LANES)
            k = keys_ref[pl.ds(off, LANES), :]
            eq = jnp.logical_and(k == base, tie_q)
            eqf = jnp.where(eq, 1.0, 0.0)
            pref = seen + jnp.dot(tril_ref[...], eqf.astype(BF16), preferred_element_type=F32)
            keys_ref[pl.ds(off, LANES), :] = jnp.where(jnp.logical_and(eq, pref > need), KEY_INVALID, k)
            return seen + jnp.sum(eqf, axis=0, keepdims=True)
        lax.fori_loop(0, nblk, body, jnp.zeros(row1, F32))

    m_ref[...] = jnp.full(m_ref.shape, NEG_BIG, F32)
    l_ref[...] = jnp.zeros(l_ref.shape, F32)
    acc_ref[...] = jnp.zeros(acc_ref.shape, F32)
    base_b = jnp.broadcast_to(base, blk)

    def att_body(j, carry):
        off = pl.multiple_of(j * LANES, LANES)
        sel = keys_ref[pl.ds(off, LANES), :] >= base_b
        d = jnp.minimum(i - j, 2)
        for hp in range(N_HEADS // 2):
            s2 = jnp.dot(k_ref[pl.ds(off, LANES), hp * LANES:(hp + 1) * LANES], qbd_ref[hp],
                         preferred_element_type=F32)
            for e in range(2):
                h = 2 * hp + e
                s = jnp.where(sel, s2[:, e * tq:(e + 1) * tq] + b1_ref[h * 3 + d], NEG_BIG)
                m_old = m_ref[h]
                m_new = jnp.maximum(m_old, jnp.max(s, axis=0, keepdims=True))
                p = jnp.exp(s - m_new)
                alpha = jnp.exp(m_old - m_new)
                l_ref[h] = alpha * l_ref[h] + jnp.sum(p, axis=0, keepdims=True)
                pv = jnp.dot(vt_ref[h * HEAD_DIM:(h + 1) * HEAD_DIM, pl.ds(off, LANES)], p.astype(BF16),
                             preferred_element_type=F32)
                acc_ref[h] = alpha * acc_ref[h] + pv
                m_ref[h] = m_new
        return carry

    lax.fori_loop(0, nblk, att_body, 0)
    for h in range(N_HEADS):
        ot_ref[h * HEAD_DIM:(h + 1) * HEAD_DIM, :] = acc_ref[h] / l_ref[h]
    o_ref[...] = ot_ref[...].T


def _prompt_attention(qt, kb, vt, qit, kib, wit, b1, tril, n_keep, nb, tp):
    tq = LANES
    nq = tp // tq
    n = nb * tp
    qcol = lambda b, i: (0, b * nq + i)
    return pl.pallas_call(
        functools.partial(_pattn_kernel, n_keep=n_keep),
        grid=(nb, nq),
        in_specs=[
            pl.BlockSpec((D_ATTN, tq), qcol),
            pl.BlockSpec((tp, D_ATTN), lambda b, i: (b, 0)),
            pl.BlockSpec((D_ATTN, tp), lambda b, i: (0, b)),
            pl.BlockSpec((IDX_HEADS * IDX_DIM, tq), qcol),
            pl.BlockSpec((tp, IDX_DIM), lambda b, i: (b, 0)),
            pl.BlockSpec((IDX_HEADS, tq), qcol),
            pl.BlockSpec((N_HEADS * 3, LANES, LANES), lambda b, i: (0, 0, 0)),
            pl.BlockSpec((LANES, LANES), lambda b, i: (0, 0)),
        ],
        out_specs=pl.BlockSpec((tq, D_ATTN), lambda b, i: (b * nq + i, 0)),
        out_shape=jax.ShapeDtypeStruct((n, D_ATTN), F32),
        scratch_shapes=[
            pltpu.VMEM((tp, tq), I32),
            pltpu.VMEM((IDX_DIM, IDX_HEADS * tq), BF16),
            pltpu.VMEM((N_HEADS // 2, LANES, 2 * tq), BF16),
            pltpu.VMEM((N_HEADS, 1, tq), F32),
            pltpu.VMEM((N_HEADS, 1, tq), F32),
            pltpu.VMEM((N_HEADS, HEAD_DIM, tq), F32),
            pltpu.VMEM((D_ATTN, tq), F32),
        ],
        compiler_params=_cparams(("parallel", "arbitrary")),
        name="prompt_attention",
    )(qt, kb, vt, qit, kib, wit, b1, tril)


def _sidx_kernel(pt_ref, qi_ref, wi_ref, kin_ref, tri_ref, lin_ref, cache_ref, idx_ref, seln_ref,
                 kibuf, sem, keys_ref, *, n_keep, n_pages, n_q, ppc, ncs, ncp, ns):
    b = pl.program_id(0)
    nb = pl.num_programs(0)
    slot = b % 2

    def page_copy(bb, p, sl):
        return pltpu.make_async_copy(cache_ref.at[0, pt_ref[bb, p]], kibuf.at[sl, p], sem.at[sl])

    def start_all(bb, sl):
        def body(p, c):
            page_copy(bb, p, sl).start()
            return c
        lax.fori_loop(0, n_pages, body, 0)

    def wait_all(bb, sl):
        def body(p, c):
            page_copy(bb, p, sl).wait()
            return c
        lax.fori_loop(0, n_pages, body, 0)

    @pl.when(b == 0)
    def _():
        start_all(b, slot)

    @pl.when(b + 1 < nb)
    def _():
        start_all(b + 1, 1 - slot)

    wait_all(b, slot)

    qi = qi_ref[0]
    w = wi_ref[0] * IDX_SCALE
    ck = ppc * PAGE_SIZE
    keys_ref[...] = jnp.full(keys_ref.shape, KEY_INVALID, I32)

    def score_rows(kc):
        s = lax.dot_general(qi, kc, NT_DIMS, preferred_element_type=F32)
        s = jnp.maximum(s, 0.0) * w
        return jnp.sum(s.reshape(n_q, IDX_HEADS, s.shape[-1]), axis=1)

    def chunk_body(c, carry):
        p0 = pl.multiple_of(c * ppc, ppc)
        kc = kibuf[slot, pl.ds(p0, ppc)].reshape(ck, IDX_DIM).astype(BF16)
        key = _score_to_key(score_rows(kc), jnp.full((n_q, ck), True))
        for q in range(n_q):
            for p in range(ppc):
                keys_ref[q, pl.ds(p0 + p, 1), :] = key[q:q + 1, p * LANES:(p + 1) * LANES]
        return carry

    lax.fori_loop(0, n_pages // ppc, chunk_body, 0)
    s_new = score_rows(kin_ref[0])
    qrow = lax.broadcasted_iota(I32, (n_q, LANES), 0)
    jcol = lax.broadcasted_iota(I32, (n_q, LANES), 1)
    key_new = _score_to_key(s_new, jcol <= qrow)
    for q in range(n_q):
        keys_ref[q, n_pages:n_pages + 1, :] = key_new[q:q + 1, :]

    shape = (n_q, 1, LANES)

    def count_ge(cand):
        hit = jnp.where(keys_ref[:, 0:ncs, :] >= cand, 1.0, 0.0)
        part = jnp.sum(hit, axis=1, keepdims=True)
        return jnp.broadcast_to(jnp.sum(part, axis=2, keepdims=True), shape)

    base, cnt = _threshold_search(count_ge, float(n_keep), shape)

    @pl.when(jnp.max(cnt) > float(n_keep))
    def _():
        need = float(n_keep) - count_ge(base + 1)
        ones = jnp.ones((LANES, LANES), BF16)
        for q in range(n_q):
            k = keys_ref[q]
            eq = jnp.logical_and(k == base[q], cnt[q] > float(n_keep))
            eqb = jnp.where(eq, 1.0, 0.0).astype(BF16)
            within = jnp.dot(eqb, tri_ref[...], preferred_element_type=F32)
            rowcnt = jnp.dot(eqb, ones, preferred_element_type=F32).astype(BF16)
            before = jnp.dot(lin_ref[...], rowcnt, preferred_element_type=F32) - rowcnt.astype(F32)
            drop = jnp.logical_and(eq, within + before > need[q])
            keys_ref[q] = jnp.where(drop, KEY_INVALID, k)

    lane_c = lax.broadcasted_iota(I32, (ns, ncp), 1).astype(F32)
    slot_c = lax.broadcasted_iota(I32, (ns, ncp), 0).astype(F32)
    slot_l = lax.broadcasted_iota(I32, (ns, LANES), 0).astype(F32)
    ones8 = jnp.ones((SUBLANES, LANES), BF16)
    lane8 = lax.broadcasted_iota(I32, (SUBLANES, LANES), 1).astype(BF16)
    for q in range(n_q):
        selb = jnp.where(keys_ref[q] >= base[q], 1.0, 0.0).astype(BF16)
        seln_ref[0, q:q + 1, :] = selb[n_pages:n_pages + 1, :].astype(F32)
        within = jnp.dot(selb, tri_ref[...], preferred_element_type=F32)
        upto = jnp.dot(lin_ref[...], selb, preferred_element_type=F32).astype(BF16)
        pin_row = lax.dot_general(ones8, upto, NT_DIMS, preferred_element_type=F32)[0:1, :]
        chunk_of = jnp.sum(jnp.where(pin_row <= slot_c, 1.0, 0.0), axis=1, keepdims=True)
        onehot = jnp.where(lane_c == chunk_of, 1.0, 0.0).astype(BF16)
        g_sel = jnp.dot(onehot, selb, preferred_element_type=F32)
        g_within = jnp.dot(onehot, within.astype(BF16), preferred_element_type=F32)
        g_upto = jnp.dot(onehot, upto, preferred_element_type=F32)
        g_pin = jnp.sum(g_upto, axis=1, keepdims=True)
        g_cnt = jnp.sum(g_sel, axis=1, keepdims=True)
        target = slot_l - (g_pin - g_cnt) + 1.0
        match = jnp.logical_and(g_sel > 0.5, g_within == target)
        matchb = jnp.where(match, 1.0, 0.0).astype(BF16)
        row_l = lax.dot_general(lane8, matchb, NT_DIMS, preferred_element_type=F32)
        chunk_b = jnp.broadcast_to(chunk_of, (ns, LANES))
        row_c = lax.dot_general(ones8, jnp.where(match, chunk_b, 0.0).astype(BF16), NT_DIMS,
                                preferred_element_type=F32)
        idx_ref[0, q:q + 1, :] = (row_c * float(LANES) + row_l)[0:1, :].astype(I32)
    for q in range(n_q, SUBLANES):
        idx_ref[0, q:q + 1, :] = jnp.zeros((1, ns), I32)
        seln_ref[0, q:q + 1, :] = jnp.zeros((1, LANES), F32)


def _sample_index(page_table, qi_rows, wi_rows, ki_new, tri, cache_ik, n_keep):
    nb, n_pages = page_table.shape
    rows = qi_rows.shape[1]
    n_q = rows // IDX_HEADS
    ppc = 8 if n_pages % 8 == 0 else 1
    ncs = _ceil_to(n_pages + 1, SUBLANES)
    ncp = _ceil_to(n_pages + 1, LANES)
    ns = _ceil_to(n_keep, LANES)
    assert ncp <= 256 and ns <= 256, "prefix counts are carried in bf16, exact up to 256"
    ar = jnp.arange(ncp)
    lin = (ar[None, :] <= ar[:, None]).astype(BF16)
    kern = functools.partial(_sidx_kernel, n_keep=n_keep, n_pages=n_pages, n_q=n_q, ppc=ppc,
                             ncs=ncs, ncp=ncp, ns=ns)
    return pl.pallas_call(
        kern,
        grid_spec=pltpu.PrefetchScalarGridSpec(
            num_scalar_prefetch=1,
            grid=(nb,),
            in_specs=[
                pl.BlockSpec((1, rows, IDX_DIM), lambda b, pt: (b, 0, 0)),
                pl.BlockSpec((1, rows, 1), lambda b, pt: (b, 0, 0)),
                pl.BlockSpec((1, LANES, IDX_DIM), lambda b, pt: (b, 0, 0)),
                pl.BlockSpec((LANES, LANES), lambda b, pt: (0, 0)),
                pl.BlockSpec((ncp, ncp), lambda b, pt: (0, 0)),
                pl.BlockSpec(memory_space=pl.ANY),
            ],
            out_specs=[pl.BlockSpec((1, SUBLANES, ns), lambda b, pt: (b, 0, 0)),
                       pl.BlockSpec((1, SUBLANES, LANES), lambda b, pt: (b, 0, 0))],
            scratch_shapes=[
                pltpu.VMEM((2, n_pages, PAGE_SIZE, IDX_DIM), F32),
                pltpu.SemaphoreType.DMA((2,)),
                pltpu.VMEM((n_q, ncp, LANES), I32),
            ],
        ),
        out_shape=[jax.ShapeDtypeStruct((nb, SUBLANES, ns), I32),
                   jax.ShapeDtypeStruct((nb, SUBLANES, LANES), F32)],
        compiler_params=_cparams(("arbitrary",)),
        name="sample_index",
    )(page_table, qi_rows, wi_rows, ki_new, tri, lin, cache_ik)


def _sgather_kernel(pt_ref, idx_smem, idx_ref, seln_ref, q8_ref, kn_ref, vn_ref, rbt_ref, bnew_ref, ck_ref, cv_ref,
                    o_ref, kbuf, vbuf, sem, *, n_keep, n_q, ns, past):
    b = pl.program_id(0)
    nb = pl.num_programs(0)
    slot = b % 2

    def start_all(bb, sl):
        for q in range(n_q):
            def body(s, carry):
                pos = jnp.minimum(idx_smem[bb * n_q + q, s], past - 1)
                page = pt_ref[bb, lax.shift_right_logical(pos, PAGE_SHIFT)]
                off = lax.bitwise_and(pos, PAGE_SIZE - 1)
                pltpu.make_async_copy(ck_ref.at[0, page, off], kbuf.at[sl, :, q * ns + s], sem.at[0, sl]).start()
                pltpu.make_async_copy(cv_ref.at[0, page, off], vbuf.at[sl, :, q * ns + s], sem.at[1, sl]).start()
                return carry
            lax.fori_loop(0, n_keep, body, 0, unroll=GATHER_UNROLL if n_keep % GATHER_UNROLL == 0 else 1)

    def wait_all(sl):
        rows = pl.ds(0, n_q * n_keep)
        pltpu.make_async_copy(kbuf.at[sl, :, rows], kbuf.at[sl, :, rows], sem.at[0, sl]).wait()
        pltpu.make_async_copy(vbuf.at[sl, :, rows], vbuf.at[sl, :, rows], sem.at[1, sl]).wait()

    @pl.when(b == 0)
    def _():
        if ns > n_keep:
            kbuf[...] = jnp.zeros(kbuf.shape, F32)
            vbuf[...] = jnp.zeros(vbuf.shape, F32)
        start_all(b, slot)

    @pl.when(b + 1 < nb)
    def _():
        start_all(b + 1, 1 - slot)

    wait_all(slot)

    lane = lax.broadcasted_iota(I32, (1, ns), 1)
    for q in range(n_q):
        pos = idx_ref[0, q:q + 1, :]
        live = jnp.logical_and(lane < n_keep, pos < past)
        bias = _bias_of(jnp.broadcast_to(past + q - pos, (N_HEADS, ns)), rbt_ref[...])
        s = jnp.zeros((N_HEADS, ns), F32)
        s_new = jnp.zeros((N_HEADS, LANES), F32)
        for h in range(N_HEADS):
            qh = q8_ref[0, q, h]
            kh = kbuf[slot, h, pl.ds(q * ns, ns), :].astype(BF16)
            s = s + lax.dot_general(qh, kh, NT_DIMS, preferred_element_type=F32)
            s_new = s_new + lax.dot_general(qh, kn_ref[0, h], NT_DIMS, preferred_element_type=F32)
        s = jnp.where(live, s + bias, NEG_BIG)
        s_new = jnp.where(seln_ref[0, q:q + 1, :] > 0.5, s_new + bnew_ref[q], NEG_BIG)
        m = jnp.maximum(jnp.max(s, axis=1, keepdims=True), jnp.max(s_new, axis=1, keepdims=True))
        p = jnp.exp(s - m)
        p_new = jnp.exp(s_new - m)
        denom = jnp.sum(p, axis=1, keepdims=True) + jnp.sum(p_new, axis=1, keepdims=True)
        pb = p.astype(BF16)
        pnb = p_new.astype(BF16)
        for h in range(N_HEADS):
            vh = vbuf[slot, h, pl.ds(q * ns, ns), :].astype(BF16)
            o8 = (jnp.dot(pb, vh, preferred_element_type=F32)
                  + jnp.dot(pnb, vn_ref[0, h], preferred_element_type=F32)) / denom
            o_ref[0, q:q + 1, h * HEAD_DIM:(h + 1) * HEAD_DIM] = o8[h:h + 1, :]


def _sample_attention(page_table, idx, seln, q8, knh, vnh, rbt, bnew, ck, cv, n_keep, past):
    nb = page_table.shape[0]
    n_q = q8.shape[1]
    ns = idx.shape[2]
    kern = functools.partial(_sgather_kernel, n_keep=n_keep, n_q=n_q, ns=ns, past=past)
    return pl.pallas_call(
        kern,
        grid_spec=pltpu.PrefetchScalarGridSpec(
            num_scalar_prefetch=2,
            grid=(nb,),
            in_specs=[
                pl.BlockSpec((1, SUBLANES, ns), lambda b, pt, ix: (b, 0, 0)),
                pl.BlockSpec((1, SUBLANES, LANES), lambda b, pt, ix: (b, 0, 0)),
                pl.BlockSpec((1, n_q, N_HEADS, SUBLANES, HEAD_DIM), lambda b, pt, ix: (b, 0, 0, 0, 0)),
                pl.BlockSpec((1, N_HEADS, LANES, HEAD_DIM), lambda b, pt, ix: (b, 0, 0, 0)),
                pl.BlockSpec((1, N_HEADS, LANES, HEAD_DIM), lambda b, pt, ix: (b, 0, 0, 0)),
                pl.BlockSpec((N_HEADS, N_BUCKETS), lambda b, pt, ix: (0, 0)),
                pl.BlockSpec((n_q, N_HEADS, LANES), lambda b, pt, ix: (0, 0, 0)),
                pl.BlockSpec(memory_space=pl.ANY),
                pl.BlockSpec(memory_space=pl.ANY),
            ],
            out_specs=pl.BlockSpec((1, n_q, D_ATTN), lambda b, pt, ix: (b, 0, 0)),
            scratch_shapes=[
                pltpu.VMEM((2, N_HEADS, n_q * ns, HEAD_DIM), F32),
                pltpu.VMEM((2, N_HEADS, n_q * ns, HEAD_DIM), F32),
                pltpu.SemaphoreType.DMA((2, 2)),
            ],
        ),
        out_shape=jax.ShapeDtypeStruct((nb, n_q, D_ATTN), F32),
        compiler_params=_cparams(("arbitrary",)),
        name="sample_attention",
    )(page_table, idx[:, :n_q].reshape(nb * n_q, ns), idx, seln, q8, knh, vnh, rbt, bnew, ck, cv)


def _sidx_t_kernel(pt_ref, qi_ref, wi_ref, kin_ref, tri_ref, lin_ref, cache_ref, sel_ref,
                   kibuf, sem, keys_ref, *, n_keep, n_pages, n_q, ppc, ncs, ncp):
    b = pl.program_id(0)
    nb = pl.num_programs(0)
    slot = b % 2

    def page_copy(bb, p, sl):
        return pltpu.make_async_copy(cache_ref.at[pt_ref[bb, p]], kibuf.at[sl, p], sem.at[sl])

    def start_all(bb, sl):
        def body(p, c):
            page_copy(bb, p, sl).start()
            return c
        lax.fori_loop(0, n_pages, body, 0, unroll=ppc)

    def wait_all(bb, sl):
        def body(p, c):
            page_copy(bb, p, sl).wait()
            return c
        lax.fori_loop(0, n_pages, body, 0, unroll=ppc)

    @pl.when(b == 0)
    def _():
        start_all(b, slot)

    @pl.when(b + 1 < nb)
    def _():
        start_all(b + 1, 1 - slot)

    wait_all(b, slot)

    qi = qi_ref[0]
    w = wi_ref[0] * IDX_SCALE
    keys_ref[...] = jnp.full(keys_ref.shape, KEY_INVALID, I32)

    def score_rows(s):
        s = jnp.maximum(s, 0.0) * w
        return jnp.sum(s.reshape(n_q, IDX_HEADS, s.shape[-1]), axis=1)

    def chunk_body(c, carry):
        p0 = pl.multiple_of(c * ppc, ppc)
        kt = jnp.concatenate([kibuf[slot, p0 + p] for p in range(ppc)], axis=1).astype(BF16)
        s = score_rows(jnp.dot(qi, kt, preferred_element_type=F32))
        key = _score_to_key(s, jnp.full(s.shape, True))
        for q in range(n_q):
            for p in range(ppc):
                keys_ref[q, pl.ds(p0 + p, 1), :] = key[q:q + 1, p * LANES:(p + 1) * LANES]
        return carry

    lax.fori_loop(0, n_pages // ppc, chunk_body, 0)
    s_new = score_rows(lax.dot_general(qi, kin_ref[0], NT_DIMS, preferred_element_type=F32))
    qrow = lax.broadcasted_iota(I32, (n_q, LANES), 0)
    jcol = lax.broadcasted_iota(I32, (n_q, LANES), 1)
    key_new = _score_to_key(s_new, jcol <= qrow)
    for q in range(n_q):
        keys_ref[q, n_pages:n_pages + 1, :] = key_new[q:q + 1, :]

    shape = (n_q, 1, LANES)

    def count_ge(cand):
        hit = jnp.where(keys_ref[:, 0:ncs, :] >= cand, 1.0, 0.0)
        part = jnp.sum(hit, axis=1, keepdims=True)
        return jnp.broadcast_to(jnp.sum(part, axis=2, keepdims=True), shape)

    base, cnt = _threshold_search(count_ge, float(n_keep), shape)

    @pl.when(jnp.max(cnt) > float(n_keep))
    def _():
        need = float(n_keep) - count_ge(base + 1)
        ones = jnp.ones((LANES, LANES), BF16)
        for q in range(n_q):
            k = keys_ref[q]
            eq = jnp.logical_and(k == base[q], cnt[q] > float(n_keep))
            eqb = jnp.where(eq, 1.0, 0.0).astype(BF16)
            within = jnp.dot(eqb, tri_ref[...], preferred_element_type=F32)
            rowcnt = jnp.dot(eqb, ones, preferred_element_type=F32).astype(BF16)
            before = jnp.dot(lin_ref[...], rowcnt, preferred_element_type=F32) - rowcnt.astype(F32)
            drop = jnp.logical_and(eq, within + before > need[q])
            keys_ref[q] = jnp.where(drop, KEY_INVALID, k)

    sel_ref[0] = jnp.where(keys_ref[:, 0:ncs, :] >= base, 1.0, 0.0)


def _sample_index_t(page_table, qi_rows, wi_rows, ki_new, tri, cache_ikt, n_keep):
    nb, n_pages = page_table.shape
    rows = qi_rows.shape[1]
    n_q = rows // IDX_HEADS
    ppc = 8 if n_pages % 8 == 0 else 1
    ncs = _ceil_to(n_pages + 1, SUBLANES)
    ncp = _ceil_to(n_pages + 1, LANES)
    assert ncp <= 256, "tie-break prefix counts are carried in bf16, exact up to 256"
    ar = jnp.arange(ncp)
    lin = (ar[None, :] <= ar[:, None]).astype(BF16)
    kern = functools.partial(_sidx_t_kernel, n_keep=n_keep, n_pages=n_pages, n_q=n_q, ppc=ppc,
                             ncs=ncs, ncp=ncp)
    return pl.pallas_call(
        kern,
        grid_spec=pltpu.PrefetchScalarGridSpec(
            num_scalar_prefetch=1,
            grid=(nb,),
            in_specs=[
                pl.BlockSpec((1, rows, IDX_DIM), lambda b, pt: (b, 0, 0)),
                pl.BlockSpec((1, rows, 1), lambda b, pt: (b, 0, 0)),
                pl.BlockSpec((1, LANES, IDX_DIM), lambda b, pt: (b, 0, 0)),
                pl.BlockSpec((LANES, LANES), lambda b, pt: (0, 0)),
                pl.BlockSpec((ncp, ncp), lambda b, pt: (0, 0)),
                pl.BlockSpec(memory_space=pl.ANY),
            ],
            out_specs=pl.BlockSpec((1, n_q, ncs, LANES), lambda b, pt: (b, 0, 0, 0)),
            scratch_shapes=[
                pltpu.VMEM((2, n_pages, IDX_DIM, PAGE_SIZE), F32),
                pltpu.SemaphoreType.DMA((2,)),
                pltpu.VMEM((n_q, ncp, LANES), I32),
            ],
        ),
        out_shape=jax.ShapeDtypeStruct((nb, n_q, ncs, LANES), F32),
        compiler_params=_cparams(("arbitrary",)),
        name="sample_index",
    )(page_table, qi_rows, wi_rows, ki_new, tri, lin, cache_ikt)


def _sdense_kernel(pt_ref, qbd_ref, sel_ref, seln_ref, kn_ref, vn_ref, bias_ref, hsel_ref, ck_ref, cv_ref, o_ref,
                   kbuf, vbuf, sem, bias_buf, m_ref, l_ref, acc_ref, *, n_q, n_pages, ppc):
    b = pl.program_id(0)
    c = pl.program_id(1)
    nb = pl.num_programs(0)
    n_chunks = n_pages // ppc
    step = b * n_chunks + c
    slot = step % 2
    rows = n_q * N_HEADS
    ck = ppc * PAGE_SIZE

    def copies(bb, cc, sl, p):
        page = pt_ref[bb, cc * ppc + p]
        return (pltpu.make_async_copy(ck_ref.at[page], kbuf.at[sl, p], sem.at[0, sl]),
                pltpu.make_async_copy(cv_ref.at[page], vbuf.at[sl, p], sem.at[1, sl]))

    def start_all(bb, cc, sl):
        for p in range(ppc):
            kc, vc = copies(bb, cc, sl, p)
            kc.start()
            vc.start()

    def wait_all(bb, cc, sl):
        for p in range(ppc):
            kc, vc = copies(bb, cc, sl, p)
            kc.wait()
            vc.wait()

    @pl.when(step == 0)
    def _():
        start_all(b, c, slot)

    @pl.when(step + 1 < nb * n_chunks)
    def _():
        nxt = step + 1
        start_all(nxt // n_chunks, nxt % n_chunks, 1 - slot)

    @pl.when(c == 0)
    def _():
        m_ref[...] = jnp.full(m_ref.shape, NEG_BIG, F32)
        l_ref[...] = jnp.zeros(l_ref.shape, F32)
        acc_ref[...] = jnp.zeros(acc_ref.shape, F32)
        bias_buf[...] = jnp.broadcast_to(bias_ref[2][:, 0:1], bias_buf.shape)

    @pl.when(c == n_chunks - 1)
    def _():
        bias_buf[:, ck - PAGE_SIZE:ck] = bias_ref[0]

    wait_all(b, c, slot)
    qbd = qbd_ref[0]

    def expand(sel):
        n = sel.shape[-1]
        return jnp.broadcast_to(sel[:, None, :], (n_q, N_HEADS, n)).reshape(rows, n)

    def update(s, live, pv_fn):
        s = jnp.where(live, s, NEG_BIG)
        m_old = m_ref[...]
        m_new = jnp.maximum(m_old, jnp.max(s, axis=1, keepdims=True))
        p = jnp.exp(s - m_new)
        alpha = jnp.exp(m_old - m_new)
        l_ref[...] = alpha * l_ref[...] + jnp.sum(p, axis=1, keepdims=True)
        acc_ref[...] = alpha * acc_ref[...] + pv_fn(p.astype(BF16))
        m_ref[...] = m_new

    s = jnp.concatenate([jnp.dot(qbd, kbuf[slot, p].astype(BF16), preferred_element_type=F32)
                         for p in range(ppc)], axis=1) + bias_buf[...]
    live = jnp.concatenate([expand(sel_ref[0, :, p, :]) for p in range(ppc)], axis=1) > 0.5

    def pv_pages(pb):
        out = jnp.zeros((rows, D_ATTN), F32)
        for p in range(ppc):
            out = out + lax.dot_general(pb[:, p * PAGE_SIZE:(p + 1) * PAGE_SIZE], vbuf[slot, p].astype(BF16),
                                        NT_DIMS, preferred_element_type=F32)
        return out

    update(s, live, pv_pages)

    @pl.when(c == n_chunks - 1)
    def _():
        s_new = lax.dot_general(qbd, kn_ref[0], NT_DIMS, preferred_element_type=F32) + bias_ref[1]
        update(s_new, expand(seln_ref[0, :, 0, :]) > 0.5,
               lambda pb: jnp.dot(pb, vn_ref[0], preferred_element_type=F32))
        full = acc_ref[...] / l_ref[...] * hsel_ref[...]
        o_ref[0] = jnp.sum(full.reshape(n_q, N_HEADS, D_ATTN), axis=1)


def _sample_attention_dense(page_table, qbd, sel, k_new, v_new, sbias, hsel, ckt, cvt):
    nb, n_pages = page_table.shape
    rows = qbd.shape[1]
    n_q = rows // N_HEADS
    ppc = 16
    assert n_pages % ppc == 0, "cached pages are attended in chunks of 16"
    n_chunks = n_pages // ppc
    ck = ppc * PAGE_SIZE
    kern = functools.partial(_sdense_kernel, n_q=n_q, n_pages=n_pages, ppc=ppc)
    return pl.pallas_call(
        kern,
        grid_spec=pltpu.PrefetchScalarGridSpec(
            num_scalar_prefetch=1,
            grid=(nb, n_chunks),
            in_specs=[
                pl.BlockSpec((1, rows, D_ATTN), lambda b, c, pt: (b, 0, 0)),
                pl.BlockSpec((1, n_q, ppc, LANES), lambda b, c, pt: (b, 0, c, 0)),
                pl.BlockSpec((1, n_q, SUBLANES, LANES), lambda b, c, pt: (b, 0, n_pages // SUBLANES, 0)),
                pl.BlockSpec((1, LANES, D_ATTN), lambda b, c, pt: (b, 0, 0)),
                pl.BlockSpec((1, LANES, D_ATTN), lambda b, c, pt: (b, 0, 0)),
                pl.BlockSpec((3, rows, LANES), lambda b, c, pt: (0, 0, 0)),
                pl.BlockSpec((rows, D_ATTN), lambda b, c, pt: (0, 0)),
                pl.BlockSpec(memory_space=pl.ANY),
                pl.BlockSpec(memory_space=pl.ANY),
            ],
            out_specs=pl.BlockSpec((1, n_q, D_ATTN), lambda b, c, pt: (b, 0, 0)),
            scratch_shapes=[
                pltpu.VMEM((2, ppc, D_ATTN, PAGE_SIZE), F32),
                pltpu.VMEM((2, ppc, D_ATTN, PAGE_SIZE), F32),
                pltpu.SemaphoreType.DMA((2, 2)),
                pltpu.VMEM((rows, ck), F32),
                pltpu.VMEM((rows, 1), F32),
                pltpu.VMEM((rows, 1), F32),
                pltpu.VMEM((rows, D_ATTN), F32),
            ],
        ),
        out_shape=jax.ShapeDtypeStruct((nb, n_q, D_ATTN), F32),
        compiler_params=_cparams(("arbitrary", "arbitrary")),
        name="sample_attention",
    )(page_table, qbd, sel, sel, k_new, v_new, sbias, hsel, ckt, cvt)


def _sample_bias_kernel(rbt_ref, o_ref, *, n_q):
    rows = n_q * N_HEADS
    q = lax.broadcasted_iota(I32, (rows, LANES), 0) // N_HEADS
    col = lax.broadcasted_iota(I32, (rows, LANES), 1)
    rb = rbt_ref[...]
    o_ref[0] = _bias_of(PAGE_SIZE + q - col, rb)
    o_ref[1] = _bias_of(q - col, rb)
    o_ref[2] = _bias_of(jnp.full((rows, LANES), REL_MAX_DIST, I32), rb)


def _sample_bias(rbt_rows, n_q):
    rows = n_q * N_HEADS
    return pl.pallas_call(
        functools.partial(_sample_bias_kernel, n_q=n_q),
        out_shape=jax.ShapeDtypeStruct((3, rows, LANES), F32),
        name="sample_bias",
    )(rbt_rows)


def _merge_kernel(o_ref, ya_ref, ga_ref, gb_ref, x_ref, wb_ref, wo_ref, x1_ref):
    yb = jnp.dot(o_ref[...].astype(BF16), wb_ref[...], preferred_element_type=F32)
    m = _sigmoid(ga_ref[...]) * ya_ref[...] + _sigmoid(gb_ref[...]) * yb
    x1_ref[...] = x_ref[...] + jnp.dot(m.astype(BF16), wo_ref[...], preferred_element_type=F32)


def _merge(o2d, ya, gates, x2d, wb, wo, tm):
    n = x2d.shape[0]
    return pl.pallas_call(
        _merge_kernel,
        grid=(n // tm,),
        in_specs=[
            pl.BlockSpec((tm, D_ATTN), lambda i: (i, 0)),
            pl.BlockSpec((tm, D_MODEL), lambda i: (i, 0)),
            pl.BlockSpec((tm, D_MODEL), lambda i: (i, 0)),
            pl.BlockSpec((tm, D_MODEL), lambda i: (i, 1)),
            pl.BlockSpec((tm, D_MODEL), lambda i: (i, 0)),
            pl.BlockSpec((D_ATTN, D_MODEL), lambda i: (0, 0)),
            pl.BlockSpec((D_MODEL, D_MODEL), lambda i: (0, 0)),
        ],
        out_specs=pl.BlockSpec((tm, D_MODEL), lambda i: (i, 0)),
        out_shape=jax.ShapeDtypeStruct((n, D_MODEL), F32),
        compiler_params=_cparams(("parallel",)),
        name="merge_out_proj",
    )(o2d, ya, gates, gates, x2d, wb, wo)


def _ffn_kernel(x_ref, g_ref, wua_ref, wub_ref, fcw_ref, fcb_ref, past_ref, wd_ref, y_ref, tail_ref,
                xn_ref, carry_ref, ext_ref, acc_ref, *, stride, tm, pad, tail_loc):
    t = pl.program_id(1)
    f = pl.program_id(2)
    nf = pl.num_programs(2)

    @pl.when(f == 0)
    def _():
        x = x_ref[...]
        ms = jnp.mean(x * x, axis=-1, keepdims=True)
        xn_ref[...] = (x * lax.rsqrt(ms + EPS) * g_ref[...]).astype(BF16)
        acc_ref[...] = jnp.zeros(acc_ref.shape, F32)

    a = jnp.dot(xn_ref[...], wua_ref[...], preferred_element_type=F32)
    bq = jnp.dot(xn_ref[...], wub_ref[...], preferred_element_type=F32)

    @pl.when(t == 0)
    def _():
        ext_ref[0:pad, :] = past_ref[0]

    @pl.when(t > 0)
    def _():
        ext_ref[0:pad, :] = carry_ref[f]

    ext_ref[pad:pad + tm, :] = a
    carry_ref[f] = ext_ref[tm:tm + pad, :]
    tail_ref[0, 0] = ext_ref[tail_loc:tail_loc + pad, :]

    conv = (ext_ref[pad - 2 * stride:pad - 2 * stride + tm, :] * fcw_ref[0:1, :]
            + ext_ref[pad - stride:pad - stride + tm, :] * fcw_ref[1:2, :]
            + a * fcw_ref[2:3, :] + fcb_ref[...])
    h = conv * _sigmoid(conv) * bq
    acc_ref[...] += jnp.dot(h.astype(BF16), wd_ref[...], preferred_element_type=F32)

    @pl.when(f == nf - 1)
    def _():
        y_ref[...] = x_ref[...] + acc_ref[...]


def _ffn(x1, g, wup, fcw, fcb, past, wd, *, nb, nt, tm, stride, tf, rows_real):
    pad = past.shape[1]
    nf = D_FF // tf
    tail_tile, tail_loc = _tail_position(rows_real, tm)
    kern = functools.partial(_ffn_kernel, stride=stride, tm=tm, pad=pad, tail_loc=tail_loc)
    y, tail = pl.pallas_call(
        kern,
        grid=(nb, nt, nf),
        in_specs=[
            pl.BlockSpec((tm, D_MODEL), lambda b, t, f: (b * nt + t, 0)),
            pl.BlockSpec((1, D_MODEL), lambda b, t, f: (0, 0)),
            pl.BlockSpec((D_MODEL, tf), lambda b, t, f: (0, f)),
            pl.BlockSpec((D_MODEL, tf), lambda b, t, f: (0, nf + f)),
            pl.BlockSpec((SUBLANES, tf), lambda b, t, f: (0, f)),
            pl.BlockSpec((1, tf), lambda b, t, f: (0, f)),
            pl.BlockSpec((1, pad, tf), lambda b, t, f: (b, 0, f)),
            pl.BlockSpec((tf, D_MODEL), lambda b, t, f: (f, 0)),
        ],
        out_specs=[
            pl.BlockSpec((tm, D_MODEL), lambda b, t, f: (b * nt + t, 0)),
            pl.BlockSpec((1, 1, pad, tf), lambda b, t, f: (b, t, 0, f)),
        ],
        out_shape=[jax.ShapeDtypeStruct((nb * nt * tm, D_MODEL), F32),
                   jax.ShapeDtypeStruct((nb, nt, pad, D_FF), F32)],
        scratch_shapes=[
            pltpu.VMEM((tm, D_MODEL), BF16),
            pltpu.VMEM((nf, pad, tf), F32),
            pltpu.VMEM((pad + tm, tf), F32),
            pltpu.VMEM((tm, D_MODEL), F32),
        ],
        compiler_params=_cparams(("parallel", "arbitrary", "arbitrary")),
        name="conv_ffn",
    )(x1, g, wup, wup, fcw, fcb, past, wd)
    return y, tail[:, tail_tile]


def _row_tile(n, cap):
    best = SUBLANES
    for cand in range(SUBLANES, cap + 1, SUBLANES):
        if n % cand == 0:
            best = cand
    return best


def kernel(x_prompt, x_sample, cache_k, cache_v, cache_idx_k, page_table, state_conv, state_ffn_conv, meta_tokens, g_attn_norm, w_in, conv_w, conv_b, conv_ln_g, conv_ln_b, w_a_out, q_norm_g, k_norm_g, rel_bias, w_b_out, w_o, g_ffn_norm, w_up, ffn_conv_w, ffn_conv_b, w_down):
    nbp, seq, _ = x_prompt.shape
    nbs, n_q, _ = x_sample.shape
    n_pages = page_table.shape[1]
    past_len = n_pages * PAGE_SIZE
    t_real = seq + N_META
    tp = _ceil_to(t_real, LANES)
    keep_p = min(TOP_K_MAX, t_real // 4)
    keep_s = min(TOP_K_MAX, (past_len + n_q) // 4)
    assert g_attn_norm.shape[0] == 1, "single trunk layer"

    w = w_in[0]
    c_q, c_v, c_qi, c_ki, c_wi, c_ga = 1024, 2048, 2560, 3072, 3136, 3144
    wm = jnp.concatenate([w[:, :c_ki], w[:, c_ga:]], axis=1).astype(BF16)
    wt = jnp.stack([w[:, c:c + D_ATTN].T for c in (c_q, c_v, c_qi)]).astype(BF16)
    ws = jnp.concatenate([w[:, c_ki:c_wi], jnp.zeros((D_MODEL, LANES - IDX_DIM), F32)], axis=1).astype(BF16)
    wst = w[:, c_wi:c_ga].T.astype(BF16)
    g_attn = g_attn_norm[0][None, :]
    gq = (jnp.tile(q_norm_g[0], N_HEADS) * ATTN_SCALE)[:, None]
    gk = jnp.tile(k_norm_g[0], N_HEADS)[None, :]
    hid = jnp.arange(D_ATTN) // HEAD_DIM
    seg = jnp.where(hid[:, None] == hid[None, :], 1.0 / HEAD_DIM, 0.0).astype(BF16)
    cw = jnp.concatenate([conv_w[0], jnp.zeros((32 - CONV_W, D_CONV), F32)], axis=0)
    cb, lg, lb = conv_b[0][None, :], conv_ln_g[0][None, :], conv_ln_b[0][None, :]
    wa = w_a_out[0].astype(BF16)
    wb = w_b_out[0].astype(BF16)
    wo = w_o[0].astype(BF16)
    g_ffn = g_ffn_norm[0][None, :]
    wup = w_up[0].astype(BF16)
    fcw = jnp.concatenate([ffn_conv_w[0], jnp.zeros((SUBLANES - FFN_CONV_W, D_FF), F32)], axis=0)
    fcb = ffn_conv_b[0][None, :]
    wd = w_down[0].astype(BF16)
    rbt = rel_bias.T
    ar = jnp.arange(LANES)
    tril = (ar[None, :] <= ar[:, None]).astype(BF16)
    tri = tril.T

    def project(x2d):
        return _in_proj(x2d, g_attn, wm, wt, ws, wst, gq, gk, seg, _row_tile(x2d.shape[0], 1024))

    xp = jnp.concatenate([jnp.broadcast_to(meta_tokens[None], (nbp, N_META, D_MODEL)), x_prompt,
                          jnp.zeros((nbp, tp - t_real, D_MODEL), F32)], axis=1).reshape(nbp * tp, D_MODEL)
    n_p = nbp * tp
    a_p, k_p, kb_p, v_p, qt_p, vt_p, qit_p, g2_p, ki_p, kib_p, wit_p = project(xp)

    tm_p = _row_tile(tp, 640)
    nt_p = tp // tm_p
    ya_p, ctail_p = _conv_branch(a_p, jnp.zeros((nbp, 32, D_CONV), F32), cw, cb, lg, lb, wa,
                                 nb=nbp, nt=nt_p, tm=tm_p, stride=1, rows_real=t_real)
    o_p = _prompt_attention(qt_p, kb_p, vt_p, qit_p, kib_p, wit_p, _prompt_bias(rbt), tril, keep_p, nbp, tp)
    x1_p = _merge(o_p, ya_p, g2_p, xp, wb, wo, _row_tile(n_p, 512))
    y_p, ftail_p = _ffn(x1_p, g_ffn, wup, fcw, fcb, jnp.zeros((nbp, SUBLANES, D_FF), F32), wd,
                        nb=nbp, nt=nt_p, tm=tm_p, stride=1, tf=FFN_TILE, rows_real=t_real)

    y_prompt = y_p.reshape(nbp, tp, D_MODEL)[:, N_META:t_real]
    p_k = k_p.reshape(nbp, tp, N_HEADS, HEAD_DIM)[None, :, :t_real]
    p_v = v_p.reshape(nbp, tp, N_HEADS, HEAD_DIM)[None, :, :t_real]
    p_ik = ki_p.reshape(nbp, tp, IDX_DIM)[None, :, :t_real]
    p_cv = ctail_p[None, :, 32 - (CONV_W - 1):]
    p_fc = ftail_p[None, :, SUBLANES - (FFN_CONV_W - 1):]

    n_s = nbs * n_q
    xs = x_sample.transpose(1, 0, 2).reshape(n_s, D_MODEL)
    a_s, k_s2, _, v_s2, qt_s, _, qit_s, g2_s, ki_s2, _, wit_s = project(xs)

    pad_c = _ceil_to((CONV_W - 1) * nbs, SUBLANES)
    past_c = state_conv[0].transpose(1, 0, 2).reshape(1, (CONV_W - 1) * nbs, D_CONV)
    past_c = jnp.pad(past_c, ((0, 0), (pad_c - (CONV_W - 1) * nbs, 0), (0, 0)))
    ya_s, ctail_s = _conv_branch(a_s, past_c, cw, cb, lg, lb, wa, nb=1, nt=1, tm=n_s, stride=nbs,
                                 rows_real=n_s)

    def batch_major(x2d, *tail):
        return x2d.reshape((n_q, nbs) + tail).transpose((1, 0) + tuple(range(2, 2 + len(tail))))

    q_s = batch_major(qt_s.T.astype(F32), N_HEADS, HEAD_DIM)
    k_s = batch_major(k_s2, N_HEADS, HEAD_DIM)
    v_s = batch_major(v_s2, N_HEADS, HEAD_DIM)
    qi_s = batch_major(qit_s.T, IDX_HEADS, IDX_DIM)
    ki_s = batch_major(ki_s2, IDX_DIM)
    wi_s = batch_major(wit_s.T, IDX_HEADS)
    rows = n_q * N_HEADS

    qi_rows = qi_s.reshape(nbs, rows, IDX_DIM)
    wi_rows = wi_s.reshape(nbs, rows, 1)
    ki_new = jnp.pad(ki_s, ((0, 0), (0, LANES - n_q), (0, 0))).astype(BF16)
    n_pool = cache_k.shape[1]
    ckt = cache_k.transpose(0, 1, 3, 4, 2).reshape(n_pool, D_ATTN, PAGE_SIZE)
    cvt = cache_v.transpose(0, 1, 3, 4, 2).reshape(n_pool, D_ATTN, PAGE_SIZE)
    cikt = cache_idx_k.transpose(0, 1, 3, 2).reshape(n_pool, IDX_DIM, PAGE_SIZE)
    sel = _sample_index_t(page_table, qi_rows, wi_rows, ki_new, tri, cikt, keep_s)

    eye = jnp.eye(N_HEADS, dtype=F32)
    qbd = (q_s[:, :, :, None, :] * eye[None, None, :, :, None]).reshape(nbs, rows, D_ATTN).astype(BF16)
    k_new = jnp.pad(k_s.reshape(nbs, n_q, D_ATTN), ((0, 0), (0, LANES - n_q), (0, 0))).astype(BF16)
    v_new = jnp.pad(v_s.reshape(nbs, n_q, D_ATTN), ((0, 0), (0, LANES - n_q), (0, 0))).astype(BF16)
    hsel = (jnp.arange(rows)[:, None] % N_HEADS == hid[None, :]).astype(F32)
    o_s = _sample_attention_dense(page_table, qbd, sel, k_new, v_new, _sample_bias(jnp.tile(rbt, (n_q, 1)), n_q),
                                  hsel, ckt, cvt)
    o_s2d = o_s.transpose(1, 0, 2).reshape(n_s, D_ATTN)

    x1_s = _merge(o_s2d, ya_s, g2_s, xs, wb, wo, _row_tile(n_s, 512))
    pad_f = _ceil_to((FFN_CONV_W - 1) * nbs, SUBLANES)
    past_f = state_ffn_conv[0].transpose(1, 0, 2).reshape(1, (FFN_CONV_W - 1) * nbs, D_FF)
    past_f = jnp.pad(past_f, ((0, 0), (pad_f - (FFN_CONV_W - 1) * nbs, 0), (0, 0)))
    y_s, ftail_s = _ffn(x1_s, g_ffn, wup, fcw, fcb, past_f, wd, nb=1, nt=1, tm=n_s, stride=nbs, tf=FFN_TILE,
                        rows_real=n_s)

    y_sample = y_s.reshape(n_q, nbs, D_MODEL).transpose(1, 0, 2)
    s_cv = ctail_s[0, pad_c - (CONV_W - 1) * nbs:].reshape(CONV_W - 1, nbs, D_CONV).transpose(1, 0, 2)[None]
    s_fc = ftail_s[0, pad_f - (FFN_CONV_W - 1) * nbs:].reshape(FFN_CONV_W - 1, nbs, D_FF).transpose(1, 0, 2)[None]

    return (y_prompt, y_sample, p_k, p_v, p_ik, p_cv, p_fc,
            k_s[None], v_s[None], ki_s[None], s_cv, s_fc)
```

```python
import functools
import math

import jax
import jax.numpy as jnp
from jax import lax
from jax.experimental import pallas as pl
from jax.experimental.pallas import tpu as pltpu

F32 = jnp.float32
BF16 = jnp.bfloat16
I32 = jnp.int32

D_MODEL = 1024
D_CONV = D_MODEL // 2
CONV_W = 31
N_HEADS = 8
HEAD_DIM = 64
D_ATTN = N_HEADS * HEAD_DIM
IDX_HEADS = 8
IDX_DIM = 64
TOP_K_MAX = 256
N_BUCKETS = 32
MAX_EXACT = N_BUCKETS // 2
REL_MAX_DIST = 128
D_FF = 2816
FFN_CONV_W = 3
N_META = 16
PAGE_SIZE = 128
PAGE_SHIFT = 7
GATHER_UNROLL = 8
FFN_TILE = D_FF // 2
EPS = 1e-6
ATTN_SCALE = HEAD_DIM ** -0.5
IDX_SCALE = (IDX_HEADS * IDX_DIM) ** -0.5
D_MAIN = 2 * D_CONV + 4 * D_ATTN + 2 * D_MODEL
JB_Q, JB_K, JB_V, JB_QI, JB_G = 2, 3, 4, 5, 6

LANES = 128
SUBLANES = 8
KEY_INVALID = -2 ** 31
NEG_BIG = -1e30
VMEM_LIMIT = 56 * 1024 * 1024
NT_DIMS = (((1,), (1,)), ((), ()))


def _cparams(sem):
    return pltpu.CompilerParams(dimension_semantics=sem, vmem_limit_bytes=VMEM_LIMIT)


def _sigmoid(x):
    return 1.0 / (1.0 + jnp.exp(-x))


def _ceil_to(x, m):
    return -(-x // m) * m


def _split_bf16(x):
    hi = x.astype(BF16)
    return hi, (x - hi.astype(F32)).astype(BF16)


def _in_proj_kernel(x_ref, g_ref, wm_ref, wt_ref, ws_ref, wst_ref, gq_ref, gk_ref, seg_ref,
                    a_ref, k_ref, kb_ref, v_ref, qt_ref, vt_ref, qit_ref, g2_ref, ki_ref, kib_ref, wit_ref,
                    xn_ref):
    j = pl.program_id(1)

    @pl.when(j == 0)
    def _():
        x = x_ref[...]
        ms = jnp.mean(x * x, axis=-1, keepdims=True)
        xn_ref[...] = (x * lax.rsqrt(ms + EPS) * g_ref[...]).astype(BF16)
        ki = jnp.dot(xn_ref[...], ws_ref[...], preferred_element_type=F32)[:, :IDX_DIM]
        ki_ref[...] = ki
        kib_ref[...] = ki.astype(BF16)
        wit_ref[...] = lax.dot_general(wst_ref[...], xn_ref[...], NT_DIMS, preferred_element_type=F32)

    @pl.when(j < JB_Q)
    def _():
        a_ref[...] = jnp.dot(xn_ref[...], wm_ref[...], preferred_element_type=F32)

    @pl.when(j >= JB_G)
    def _():
        g2_ref[...] = jnp.dot(xn_ref[...], wm_ref[...], preferred_element_type=F32)

    @pl.when(j == JB_Q)
    def _():
        yt = lax.dot_general(wt_ref[0], xn_ref[...], NT_DIMS, preferred_element_type=F32)
        hi, lo = _split_bf16(yt * yt)
        ms = (jnp.dot(seg_ref[...], hi, preferred_element_type=F32)
              + jnp.dot(seg_ref[...], lo, preferred_element_type=F32))
        qt_ref[...] = (yt * lax.rsqrt(ms + EPS) * gq_ref[...]).astype(BF16)

    @pl.when(j == JB_K)
    def _():
        y = jnp.dot(xn_ref[...], wm_ref[...], preferred_element_type=F32)
        hi, lo = _split_bf16(y * y)
        ms = (jnp.dot(hi, seg_ref[...], preferred_element_type=F32)
              + jnp.dot(lo, seg_ref[...], preferred_element_type=F32))
        k = y * lax.rsqrt(ms + EPS) * gk_ref[...]
        k_ref[...] = k
        kb_ref[...] = k.astype(BF16)

    @pl.when(j == JB_V)
    def _():
        v_ref[...] = jnp.dot(xn_ref[...], wm_ref[...], preferred_element_type=F32)
        vt_ref[...] = lax.dot_general(wt_ref[1], xn_ref[...], NT_DIMS, preferred_element_type=F32).astype(BF16)

    @pl.when(j == JB_QI)
    def _():
        qit_ref[...] = lax.dot_general(wt_ref[2], xn_ref[...], NT_DIMS, preferred_element_type=F32).astype(BF16)


def _in_proj(x2d, g, wm, wt, ws, wst, gq, gk, seg, tm):
    n = x2d.shape[0]
    tn = D_ATTN
    nj = D_MAIN // tn
    row = lambda i, j: (i, 0)
    colt = lambda i, j: (0, i)
    const2 = lambda i, j: (0, 0)
    out_shape = [
        jax.ShapeDtypeStruct((n, 2 * D_CONV), F32),
        jax.ShapeDtypeStruct((n, D_ATTN), F32),
        jax.ShapeDtypeStruct((n, D_ATTN), BF16),
        jax.ShapeDtypeStruct((n, D_ATTN), F32),
        jax.ShapeDtypeStruct((D_ATTN, n), BF16),
        jax.ShapeDtypeStruct((D_ATTN, n), BF16),
        jax.ShapeDtypeStruct((D_ATTN, n), BF16),
        jax.ShapeDtypeStruct((n, 2 * D_MODEL), F32),
        jax.ShapeDtypeStruct((n, IDX_DIM), F32),
        jax.ShapeDtypeStruct((n, IDX_DIM), BF16),
        jax.ShapeDtypeStruct((IDX_HEADS, n), F32),
    ]
    out_specs = [
        pl.BlockSpec((tm, tn), lambda i, j: (i, jnp.minimum(j, 1))),
        pl.BlockSpec((tm, tn), row),
        pl.BlockSpec((tm, tn), row),
        pl.BlockSpec((tm, tn), row),
        pl.BlockSpec((tn, tm), colt),
        pl.BlockSpec((tn, tm), colt),
        pl.BlockSpec((tn, tm), colt),
        pl.BlockSpec((tm, tn), lambda i, j: (i, jnp.clip(j - JB_G, 0, nj - JB_G - 1))),
        pl.BlockSpec((tm, IDX_DIM), row),
        pl.BlockSpec((tm, IDX_DIM), row),
        pl.BlockSpec((IDX_HEADS, tm), colt),
    ]
    return pl.pallas_call(
        _in_proj_kernel,
        grid=(n // tm, nj),
        in_specs=[
            pl.BlockSpec((tm, D_MODEL), row),
            pl.BlockSpec((1, D_MODEL), const2),
            pl.BlockSpec((D_MODEL, tn), lambda i, j: (0, j)),
            pl.BlockSpec((3, tn, D_MODEL), lambda i, j: (0, 0, 0)),
            pl.BlockSpec((D_MODEL, LANES), const2),
            pl.BlockSpec((IDX_HEADS, D_MODEL), const2),
            pl.BlockSpec((D_ATTN, 1), const2),
            pl.BlockSpec((1, D_ATTN), const2),
            pl.BlockSpec((D_ATTN, D_ATTN), const2),
        ],
        out_specs=out_specs,
        out_shape=out_shape,
        scratch_shapes=[pltpu.VMEM((tm, D_MODEL), BF16)],
        compiler_params=_cparams(("parallel", "arbitrary")),
        name="in_proj",
    )(x2d, g, wm, wt, ws, wst, gq, gk, seg)


def _conv_kernel(a_ref, past_ref, cw_ref, cb_ref, lg_ref, lb_ref, wa_ref, ya_ref, tail_ref, ext_ref, h_ref,
                 *, stride, tm, pad, rc, tail_tile, tail_loc):
    t = pl.program_id(1)

    @pl.when(t == 0)
    def _():
        ext_ref[0, 0:pad, :] = past_ref[0]

    @pl.when(t > 0)
    def _():
        ext_ref[0, 0:pad, :] = ext_ref[0, tm:tm + pad, :]

    a = a_ref[...]
    ext_ref[0, pad:pad + tm, :] = a[:, :D_CONV] * _sigmoid(a[:, D_CONV:])

    @pl.when(t <= tail_tile)
    def _():
        tail_ref[0] = ext_ref[0, tail_loc:tail_loc + pad, :]

    n_shift = ext_ref.shape[0]
    length = pad + tm
    for r in range(1, n_shift):
        ext_ref[r, 0:length - SUBLANES, :] = ext_ref[0, r:r + length - SUBLANES, :]

    cb = cb_ref[...]
    lg = lg_ref[...]
    lb = lb_ref[...]
    for r0 in range(0, tm, rc):
        acc = jnp.zeros((rc, D_CONV), F32) + cb
        for w in range(CONV_W):
            off = pad - (CONV_W - 1 - w) * stride + r0
            r = off % n_shift
            acc = acc + ext_ref[r, off - r:off - r + rc, :] * cw_ref[w:w + 1, :]
        mu = jnp.mean(acc, axis=-1, keepdims=True)
        d = acc - mu
        var = jnp.mean(d * d, axis=-1, keepdims=True)
        h = d * lax.rsqrt(var + EPS) * lg + lb
        h_ref[r0:r0 + rc, :] = (h * _sigmoid(h)).astype(BF16)
    ya_ref[...] = jnp.dot(h_ref[...], wa_ref[...], preferred_element_type=F32)


def _tail_position(rows_real, tm):
    tail_tile = (rows_real - 1) // tm
    return tail_tile, rows_real - tail_tile * tm


def _conv_branch(a_in, past, cw, cb, lg, lb, wa, *, nb, nt, tm, stride, rows_real):
    pad = past.shape[1]
    rc = 32 if tm % 32 == 0 else SUBLANES
    tail_tile, tail_loc = _tail_position(rows_real, tm)
    kern = functools.partial(_conv_kernel, stride=stride, tm=tm, pad=pad, rc=rc,
                             tail_tile=tail_tile, tail_loc=tail_loc)
    return pl.pallas_call(
        kern,
        grid=(nb, nt),
        in_specs=[
            pl.BlockSpec((tm, 2 * D_CONV), lambda b, t: (b * nt + t, 0)),
            pl.BlockSpec((1, pad, D_CONV), lambda b, t: (b, 0, 0)),
            pl.BlockSpec((32, D_CONV), lambda b, t: (0, 0)),
            pl.BlockSpec((1, D_CONV), lambda b, t: (0, 0)),
            pl.BlockSpec((1, D_CONV), lambda b, t: (0, 0)),
            pl.BlockSpec((1, D_CONV), lambda b, t: (0, 0)),
            pl.BlockSpec((D_CONV, D_MODEL), lambda b, t: (0, 0)),
        ],
        out_specs=[
            pl.BlockSpec((tm, D_MODEL), lambda b, t: (b * nt + t, 0)),
            pl.BlockSpec((1, pad, D_CONV), lambda b, t: (b, 0, 0)),
        ],
        out_shape=[jax.ShapeDtypeStruct((nb * nt * tm, D_MODEL), F32),
                   jax.ShapeDtypeStruct((nb, pad, D_CONV), F32)],
        scratch_shapes=[pltpu.VMEM((1 if stride % SUBLANES == 0 else SUBLANES, pad + tm, D_CONV), F32),
                        pltpu.VMEM((tm, D_CONV), BF16)],
        compiler_params=_cparams(("parallel", "arbitrary")),
        name="conv_branch",
    )(a_in, past, cw, cb, lg, lb, wa)


def _rel_bucket(rel):
    n = jnp.maximum(rel, 0)
    nf = jnp.maximum(n, 1).astype(F32)
    large = MAX_EXACT + (jnp.log(nf / MAX_EXACT) / math.log(REL_MAX_DIST / MAX_EXACT)
                         * (N_BUCKETS - MAX_EXACT)).astype(I32)
    large = jnp.minimum(large, N_BUCKETS - 1)
    return jnp.where(n < MAX_EXACT, n, large)


def _bias_of(rel, rb_rows):
    bucket = _rel_bucket(rel)
    out = jnp.zeros(rel.shape, F32)
    for b in range(N_BUCKETS):
        out = jnp.where(bucket == b, rb_rows[:, b:b + 1], out)
    return out


def _prompt_bias_kernel(rbt_ref, b1_ref):
    key = lax.broadcasted_iota(I32, (LANES, LANES), 0)
    qry = lax.broadcasted_iota(I32, (LANES, LANES), 1)
    for h in range(N_HEADS):
        for d in range(3):
            b1_ref[h * 3 + d] = _bias_of(d * LANES + qry - key, rbt_ref[h:h + 1, :])


def _prompt_bias(rbt):
    return pl.pallas_call(
        _prompt_bias_kernel,
        out_shape=jax.ShapeDtypeStruct((N_HEADS * 3, LANES, LANES), F32),
        name="prompt_bias",
    )(rbt)


def _sample_new_bias_kernel(rbt_ref, o_ref, *, n_q):
    col = lax.broadcasted_iota(I32, (N_HEADS, LANES), 1)
    for q in range(n_q):
        o_ref[q] = _bias_of(q - col, rbt_ref[...])


def _sample_new_bias(rbt, n_q):
    return pl.pallas_call(
        functools.partial(_sample_new_bias_kernel, n_q=n_q),
        out_shape=jax.ShapeDtypeStruct((n_q, N_HEADS, LANES), F32),
        name="sample_bias",
    )(rbt)


def _score_to_key(s, valid):
    bits = pltpu.bitcast(s, I32)
    key = bits ^ ((bits >> 31) & 0x7FFFFFFF)
    key = jnp.where(bits == KEY_INVALID, 0, key)
    return jnp.where(valid, key, KEY_INVALID)


def _threshold_search(count_ge, n_keep, shape):
    cnt_valid = count_ge(jnp.full(shape, KEY_INVALID + 1, I32))
    c0 = count_ge(jnp.zeros(shape, I32))
    ok0 = c0 >= n_keep
    base = jnp.where(ok0, 0, KEY_INVALID).astype(I32)
    cnt = jnp.where(ok0, c0, cnt_valid)

    def body(t, carry):
        base, cnt = carry
        cand = base | lax.shift_left(jnp.int32(1), 30 - t)
        c = count_ge(cand)
        ok = c >= n_keep
        return jnp.where(ok, cand, base), jnp.where(ok, c, cnt)

    base, cnt = lax.fori_loop(0, 31, body, (base, cnt))
    return jnp.maximum(base, KEY_INVALID + 1), cnt


def _pattn_kernel(qt_ref, k_ref, vt_ref, qit_ref, ki_ref, wit_ref, b1_ref, tril_ref, o_ref,
                  keys_ref, qiw_ref, qbd_ref, m_ref, l_ref, acc_ref, ot_ref, p_ref, alpha_ref, *, n_keep):
    i = pl.program_id(1)
    tq = LANES
    row1 = (1, tq)
    lead = (i + 1) % 2
    npair = (i + 1) // 2
    qpos = i * tq + lax.broadcasted_iota(I32, row1, 1)

    def for_chunks(fn):
        @pl.when(lead == 1)
        def _():
            fn(0, LANES)

        def body(j, carry):
            fn(pl.multiple_of(lead * LANES + j * 2 * LANES, LANES), 2 * LANES)
            return carry
        lax.fori_loop(0, npair, body, 0)

    for h in range(IDX_HEADS):
        qiw_ref[:, h * tq:(h + 1) * tq] = qit_ref[h * IDX_DIM:(h + 1) * IDX_DIM, :]
    qbd_ref[...] = jnp.zeros(qbd_ref.shape, BF16)
    for h in range(N_HEADS):
        hp, e = divmod(h, 2)
        qbd_ref[hp, e * HEAD_DIM:(e + 1) * HEAD_DIM, e * tq:(e + 1) * tq] = qt_ref[h * HEAD_DIM:(h + 1) * HEAD_DIM, :]
    wit = wit_ref[...] * IDX_SCALE

    def idx_chunk(off, ck):
        sall = jnp.dot(ki_ref[pl.ds(off, ck), :], qiw_ref[...], preferred_element_type=F32)
        s = jnp.zeros((ck, tq), F32)
        for h in range(IDX_HEADS):
            s = s + wit[h:h + 1, :] * jnp.maximum(sall[:, h * tq:(h + 1) * tq], 0.0)
        kpos = off + lax.broadcasted_iota(I32, (ck, tq), 0)
        keys_ref[pl.ds(off, ck), :] = _score_to_key(s, kpos <= qpos)

    for_chunks(idx_chunk)

    blk = (LANES, tq)
    nblk = i + 1

    def count_ge(cand):
        cb = jnp.broadcast_to(cand, blk)

        def hit(off):
            return jnp.where(keys_ref[pl.ds(off, LANES), :] >= cb, 1.0, 0.0)

        def body(j, acc):
            off = pl.multiple_of(lead * LANES + j * 2 * LANES, LANES)
            return acc + hit(off) + hit(off + LANES)
        acc = lax.fori_loop(0, npair, body, jnp.where(lead == 1, hit(0), 0.0))
        return jnp.sum(acc, axis=0, keepdims=True)

    base, cnt = _threshold_search(count_ge, float(n_keep), row1)

    @pl.when(jnp.max(cnt) > float(n_keep))
    def _():
        need = float(n_keep) - count_ge(base + 1)
        tie_q = cnt > float(n_keep)

        def body(j, seen):
            off = pl.multiple_of(j * LANES, LANES)
            k = keys_ref[pl.ds(off, LANES), :]
            eq = jnp.logical_and(k == base, tie_q)
            eqf = jnp.where(eq, 1.0, 0.0)
            pref = seen + jnp.dot(tril_ref[...], eqf.astype(BF16), preferred_element_type=F32)
            keys_ref[pl.ds(off, LANES), :] = jnp.where(jnp.logical_and(eq, pref > need), KEY_INVALID, k)
            return seen + jnp.sum(eqf, axis=0, keepdims=True)
        lax.fori_loop(0, nblk, body, jnp.zeros(row1, F32))

    m_ref[...] = jnp.full(m_ref.shape, NEG_BIG, F32)
    l_ref[...] = jnp.zeros(l_ref.shape, F32)
    acc_ref[...] = jnp.zeros(acc_ref.shape, F32)
    p_ref[...] = jnp.zeros(p_ref.shape, BF16)
    alpha_ref[...] = jnp.ones(alpha_ref.shape, F32)
    base_b = jnp.broadcast_to(base, blk)

    def apply_pv(jb):
        off = pl.multiple_of(jb * LANES, LANES)
        for h in range(N_HEADS):
            pv = jnp.dot(vt_ref[h * HEAD_DIM:(h + 1) * HEAD_DIM, pl.ds(off, LANES)], p_ref[h],
                         preferred_element_type=F32)
            acc_ref[h] = alpha_ref[h] * acc_ref[h] + pv

    def att_body(j, carry):
        apply_pv(jnp.maximum(j - 1, 0))
        off = pl.multiple_of(j * LANES, LANES)
        sel = keys_ref[pl.ds(off, LANES), :] >= base_b
        d = jnp.minimum(i - j, 2)
        for hp in range(N_HEADS // 2):
            for e in range(2):
                h = 2 * hp + e
                s1 = jnp.dot(k_ref[pl.ds(off, LANES), hp * LANES:(hp + 1) * LANES],
                             qbd_ref[hp, :, e * tq:(e + 1) * tq], preferred_element_type=F32)
                s = jnp.where(sel, s1 + b1_ref[h * 3 + d], NEG_BIG)
                m_old = m_ref[h]
                m_new = jnp.maximum(m_old, jnp.max(s, axis=0, keepdims=True))
                p = jnp.exp(s - m_new)
                alpha = jnp.exp(m_old - m_new)
                l_ref[h] = alpha * l_ref[h] + jnp.sum(p, axis=0, keepdims=True)
                p_ref[h] = p.astype(BF16)
                alpha_ref[h] = alpha
                m_ref[h] = m_new
        return carry

    lax.fori_loop(0, nblk, att_body, 0)
    apply_pv(nblk - 1)
    for h in range(N_HEADS):
        ot_ref[h * HEAD_DIM:(h + 1) * HEAD_DIM, :] = acc_ref[h] / l_ref[h]
    o_ref[...] = ot_ref[...].T


def _prompt_attention(qt, kb, vt, qit, kib, wit, b1, tril, n_keep, nb, tp):
    tq = LANES
    nq = tp // tq
    n = nb * tp
    qcol = lambda b, i: (0, b * nq + i)
    return pl.pallas_call(
        functools.partial(_pattn_kernel, n_keep=n_keep),
        grid=(nb, nq),
        in_specs=[
            pl.BlockSpec((D_ATTN, tq), qcol),
            pl.BlockSpec((tp, D_ATTN), lambda b, i: (b, 0)),
            pl.BlockSpec((D_ATTN, tp), lambda b, i: (0, b)),
            pl.BlockSpec((IDX_HEADS * IDX_DIM, tq), qcol),
            pl.BlockSpec((tp, IDX_DIM), lambda b, i: (b, 0)),
            pl.BlockSpec((IDX_HEADS, tq), qcol),
            pl.BlockSpec((N_HEADS * 3, LANES, LANES), lambda b, i: (0, 0, 0)),
            pl.BlockSpec((LANES, LANES), lambda b, i: (0, 0)),
        ],
        out_specs=pl.BlockSpec((tq, D_ATTN), lambda b, i: (b * nq + i, 0)),
        out_shape=jax.ShapeDtypeStruct((n, D_ATTN), F32),
        scratch_shapes=[
            pltpu.VMEM((tp, tq), I32),
            pltpu.VMEM((IDX_DIM, IDX_HEADS * tq), BF16),
            pltpu.VMEM((N_HEADS // 2, LANES, 2 * tq), BF16),
            pltpu.VMEM((N_HEADS, 1, tq), F32),
            pltpu.VMEM((N_HEADS, 1, tq), F32),
            pltpu.VMEM((N_HEADS, HEAD_DIM, tq), F32),
            pltpu.VMEM((D_ATTN, tq), F32),
            pltpu.VMEM((N_HEADS, LANES, tq), BF16),
            pltpu.VMEM((N_HEADS, 1, tq), F32),
        ],
        compiler_params=_cparams(("parallel", "arbitrary")),
        name="prompt_attention",
    )(qt, kb, vt, qit, kib, wit, b1, tril)


def _sidx_kernel(pt_ref, qi_ref, wi_ref, kin_ref, tri_ref, lin_ref, cache_ref, idx_ref, seln_ref,
                 kibuf, sem, keys_ref, *, n_keep, n_pages, n_q, ppc, ncs, ncp, ns):
    b = pl.program_id(0)
    nb = pl.num_programs(0)
    slot = b % 2

    def page_copy(bb, p, sl):
        return pltpu.make_async_copy(cache_ref.at[0, pt_ref[bb, p]], kibuf.at[sl, p], sem.at[sl])

    def start_all(bb, sl):
        def body(p, c):
            page_copy(bb, p, sl).start()
            return c
        lax.fori_loop(0, n_pages, body, 0)

    def wait_all(bb, sl):
        def body(p, c):
            page_copy(bb, p, sl).wait()
            return c
        lax.fori_loop(0, n_pages, body, 0)

    @pl.when(b == 0)
    def _():
        start_all(b, slot)

    @pl.when(b + 1 < nb)
    def _():
        start_all(b + 1, 1 - slot)

    wait_all(b, slot)

    qi = qi_ref[0]
    w = wi_ref[0] * IDX_SCALE
    ck = ppc * PAGE_SIZE
    keys_ref[...] = jnp.full(keys_ref.shape, KEY_INVALID, I32)

    def score_rows(kc):
        s = lax.dot_general(qi, kc, NT_DIMS, preferred_element_type=F32)
        s = jnp.maximum(s, 0.0) * w
        return jnp.sum(s.reshape(n_q, IDX_HEADS, s.shape[-1]), axis=1)

    def chunk_body(c, carry):
        p0 = pl.multiple_of(c * ppc, ppc)
        kc = kibuf[slot, pl.ds(p0, ppc)].reshape(ck, IDX_DIM).astype(BF16)
        key = _score_to_key(score_rows(kc), jnp.full((n_q, ck), True))
        for q in range(n_q):
            for p in range(ppc):
                keys_ref[q, pl.ds(p0 + p, 1), :] = key[q:q + 1, p * LANES:(p + 1) * LANES]
        return carry

    lax.fori_loop(0, n_pages // ppc, chunk_body, 0)
    s_new = score_rows(kin_ref[0])
    qrow = lax.broadcasted_iota(I32, (n_q, LANES), 0)
    jcol = lax.broadcasted_iota(I32, (n_q, LANES), 1)
    key_new = _score_to_key(s_new, jcol <= qrow)
    for q in range(n_q):
        keys_ref[q, n_pages:n_pages + 1, :] = key_new[q:q + 1, :]

    shape = (n_q, 1, LANES)

    def count_ge(cand):
        hit = jnp.where(keys_ref[:, 0:ncs, :] >= cand, 1.0, 0.0)
        part = jnp.sum(hit, axis=1, keepdims=True)
        return jnp.broadcast_to(jnp.sum(part, axis=2, keepdims=True), shape)

    base, cnt = _threshold_search(count_ge, float(n_keep), shape)

    @pl.when(jnp.max(cnt) > float(n_keep))
    def _():
        need = float(n_keep) - count_ge(base + 1)
        ones = jnp.ones((LANES, LANES), BF16)
        for q in range(n_q):
            k = keys_ref[q]
            eq = jnp.logical_and(k == base[q], cnt[q] > float(n_keep))
            eqb = jnp.where(eq, 1.0, 0.0).astype(BF16)
            within = jnp.dot(eqb, tri_ref[...], preferred_element_type=F32)
            rowcnt = jnp.dot(eqb, ones, preferred_element_type=F32).astype(BF16)
            before = jnp.dot(lin_ref[...], rowcnt, preferred_element_type=F32) - rowcnt.astype(F32)
            drop = jnp.logical_and(eq, within + before > need[q])
            keys_ref[q] = jnp.where(drop, KEY_INVALID, k)

    lane_c = lax.broadcasted_iota(I32, (ns, ncp), 1).astype(F32)
    slot_c = lax.broadcasted_iota(I32, (ns, ncp), 0).astype(F32)
    slot_l = lax.broadcasted_iota(I32, (ns, LANES), 0).astype(F32)
    ones8 = jnp.ones((SUBLANES, LANES), BF16)
    lane8 = lax.broadcasted_iota(I32, (SUBLANES, LANES), 1).astype(BF16)
    for q in range(n_q):
        selb = jnp.where(keys_ref[q] >= base[q], 1.0, 0.0).astype(BF16)
        seln_ref[0, q:q + 1, :] = selb[n_pages:n_pages + 1, :].astype(F32)
        within = jnp.dot(selb, tri_ref[...], preferred_element_type=F32)
        upto = jnp.dot(lin_ref[...], selb, preferred_element_type=F32).astype(BF16)
        pin_row = lax.dot_general(ones8, upto, NT_DIMS, preferred_element_type=F32)[0:1, :]
        chunk_of = jnp.sum(jnp.where(pin_row <= slot_c, 1.0, 0.0), axis=1, keepdims=True)
        onehot = jnp.where(lane_c == chunk_of, 1.0, 0.0).astype(BF16)
        g_sel = jnp.dot(onehot, selb, preferred_element_type=F32)
        g_within = jnp.dot(onehot, within.astype(BF16), preferred_element_type=F32)
        g_upto = jnp.dot(onehot, upto, preferred_element_type=F32)
        g_pin = jnp.sum(g_upto, axis=1, keepdims=True)
        g_cnt = jnp.sum(g_sel, axis=1, keepdims=True)
        target = slot_l - (g_pin - g_cnt) + 1.0
        match = jnp.logical_and(g_sel > 0.5, g_within == target)
        matchb = jnp.where(match, 1.0, 0.0).astype(BF16)
        row_l = lax.dot_general(lane8, matchb, NT_DIMS, preferred_element_type=F32)
        chunk_b = jnp.broadcast_to(chunk_of, (ns, LANES))
        row_c = lax.dot_general(ones8, jnp.where(match, chunk_b, 0.0).astype(BF16), NT_DIMS,
                                preferred_element_type=F32)
        idx_ref[0, q:q + 1, :] = (row_c * float(LANES) + row_l)[0:1, :].astype(I32)
    for q in range(n_q, SUBLANES):
        idx_ref[0, q:q + 1, :] = jnp.zeros((1, ns), I32)
        seln_ref[0, q:q + 1, :] = jnp.zeros((1, LANES), F32)


def _sample_index(page_table, qi_rows, wi_rows, ki_new, tri, cache_ik, n_keep):
    nb, n_pages = page_table.shape
    rows = qi_rows.shape[1]
    n_q = rows // IDX_HEADS
    ppc = 8 if n_pages % 8 == 0 else 1
    ncs = _ceil_to(n_pages + 1, SUBLANES)
    ncp = _ceil_to(n_pages + 1, LANES)
    ns = _ceil_to(n_keep, LANES)
    assert ncp <= 256 and ns <= 256, "prefix counts are carried in bf16, exact up to 256"
    ar = jnp.arange(ncp)
    lin = (ar[None, :] <= ar[:, None]).astype(BF16)
    kern = functools.partial(_sidx_kernel, n_keep=n_keep, n_pages=n_pages, n_q=n_q, ppc=ppc,
                             ncs=ncs, ncp=ncp, ns=ns)
    return pl.pallas_call(
        kern,
        grid_spec=pltpu.PrefetchScalarGridSpec(
            num_scalar_prefetch=1,
            grid=(nb,),
            in_specs=[
                pl.BlockSpec((1, rows, IDX_DIM), lambda b, pt: (b, 0, 0)),
                pl.BlockSpec((1, rows, 1), lambda b, pt: (b, 0, 0)),
                pl.BlockSpec((1, LANES, IDX_DIM), lambda b, pt: (b, 0, 0)),
                pl.BlockSpec((LANES, LANES), lambda b, pt: (0, 0)),
                pl.BlockSpec((ncp, ncp), lambda b, pt: (0, 0)),
                pl.BlockSpec(memory_space=pl.ANY),
            ],
            out_specs=[pl.BlockSpec((1, SUBLANES, ns), lambda b, pt: (b, 0, 0)),
                       pl.BlockSpec((1, SUBLANES, LANES), lambda b, pt: (b, 0, 0))],
            scratch_shapes=[
                pltpu.VMEM((2, n_pages, PAGE_SIZE, IDX_DIM), F32),
                pltpu.SemaphoreType.DMA((2,)),
                pltpu.VMEM((n_q, ncp, LANES), I32),
            ],
        ),
        out_shape=[jax.ShapeDtypeStruct((nb, SUBLANES, ns), I32),
                   jax.ShapeDtypeStruct((nb, SUBLANES, LANES), F32)],
        compiler_params=_cparams(("arbitrary",)),
        name="sample_index",
    )(page_table, qi_rows, wi_rows, ki_new, tri, lin, cache_ik)


def _sgather_kernel(pt_ref, idx_smem, idx_ref, seln_ref, q8_ref, kn_ref, vn_ref, rbt_ref, bnew_ref, ck_ref, cv_ref,
                    o_ref, kbuf, vbuf, sem, *, n_keep, n_q, ns, past):
    b = pl.program_id(0)
    nb = pl.num_programs(0)
    slot = b % 2

    def start_all(bb, sl):
        for q in range(n_q):
            def body(s, carry):
                pos = jnp.minimum(idx_smem[bb * n_q + q, s], past - 1)
                page = pt_ref[bb, lax.shift_right_logical(pos, PAGE_SHIFT)]
                off = lax.bitwise_and(pos, PAGE_SIZE - 1)
                pltpu.make_async_copy(ck_ref.at[0, page, off], kbuf.at[sl, :, q * ns + s], sem.at[0, sl]).start()
                pltpu.make_async_copy(cv_ref.at[0, page, off], vbuf.at[sl, :, q * ns + s], sem.at[1, sl]).start()
                return carry
            lax.fori_loop(0, n_keep, body, 0, unroll=GATHER_UNROLL if n_keep % GATHER_UNROLL == 0 else 1)

    def wait_all(sl):
        rows = pl.ds(0, n_q * n_keep)
        pltpu.make_async_copy(kbuf.at[sl, :, rows], kbuf.at[sl, :, rows], sem.at[0, sl]).wait()
        pltpu.make_async_copy(vbuf.at[sl, :, rows], vbuf.at[sl, :, rows], sem.at[1, sl]).wait()

    @pl.when(b == 0)
    def _():
        if ns > n_keep:
            kbuf[...] = jnp.zeros(kbuf.shape, F32)
            vbuf[...] = jnp.zeros(vbuf.shape, F32)
        start_all(b, slot)

    @pl.when(b + 1 < nb)
    def _():
        start_all(b + 1, 1 - slot)

    wait_all(slot)

    lane = lax.broadcasted_iota(I32, (1, ns), 1)
    for q in range(n_q):
        pos = idx_ref[0, q:q + 1, :]
        live = jnp.logical_and(lane < n_keep, pos < past)
        bias = _bias_of(jnp.broadcast_to(past + q - pos, (N_HEADS, ns)), rbt_ref[...])
        s = jnp.zeros((N_HEADS, ns), F32)
        s_new = jnp.zeros((N_HEADS, LANES), F32)
        for h in range(N_HEADS):
            qh = q8_ref[0, q, h]
            kh = kbuf[slot, h, pl.ds(q * ns, ns), :].astype(BF16)
            s = s + lax.dot_general(qh, kh, NT_DIMS, preferred_element_type=F32)
            s_new = s_new + lax.dot_general(qh, kn_ref[0, h], NT_DIMS, preferred_element_type=F32)
        s = jnp.where(live, s + bias, NEG_BIG)
        s_new = jnp.where(seln_ref[0, q:q + 1, :] > 0.5, s_new + bnew_ref[q], NEG_BIG)
        m = jnp.maximum(jnp.max(s, axis=1, keepdims=True), jnp.max(s_new, axis=1, keepdims=True))
        p = jnp.exp(s - m)
        p_new = jnp.exp(s_new - m)
        denom = jnp.sum(p, axis=1, keepdims=True) + jnp.sum(p_new, axis=1, keepdims=True)
        pb = p.astype(BF16)
        pnb = p_new.astype(BF16)
        for h in range(N_HEADS):
            vh = vbuf[slot, h, pl.ds(q * ns, ns), :].astype(BF16)
            o8 = (jnp.dot(pb, vh, preferred_element_type=F32)
                  + jnp.dot(pnb, vn_ref[0, h], preferred_element_type=F32)) / denom
            o_ref[0, q:q + 1, h * HEAD_DIM:(h + 1) * HEAD_DIM] = o8[h:h + 1, :]


def _sample_attention(page_table, idx, seln, q8, knh, vnh, rbt, bnew, ck, cv, n_keep, past):
    nb = page_table.shape[0]
    n_q = q8.shape[1]
    ns = idx.shape[2]
    kern = functools.partial(_sgather_kernel, n_keep=n_keep, n_q=n_q, ns=ns, past=past)
    return pl.pallas_call(
        kern,
        grid_spec=pltpu.PrefetchScalarGridSpec(
            num_scalar_prefetch=2,
            grid=(nb,),
            in_specs=[
                pl.BlockSpec((1, SUBLANES, ns), lambda b, pt, ix: (b, 0, 0)),
                pl.BlockSpec((1, SUBLANES, LANES), lambda b, pt, ix: (b, 0, 0)),
                pl.BlockSpec((1, n_q, N_HEADS, SUBLANES, HEAD_DIM), lambda b, pt, ix: (b, 0, 0, 0, 0)),
                pl.BlockSpec((1, N_HEADS, LANES, HEAD_DIM), lambda b, pt, ix: (b, 0, 0, 0)),
                pl.BlockSpec((1, N_HEADS, LANES, HEAD_DIM), lambda b, pt, ix: (b, 0, 0, 0)),
                pl.BlockSpec((N_HEADS, N_BUCKETS), lambda b, pt, ix: (0, 0)),
                pl.BlockSpec((n_q, N_HEADS, LANES), lambda b, pt, ix: (0, 0, 0)),
                pl.BlockSpec(memory_space=pl.ANY),
                pl.BlockSpec(memory_space=pl.ANY),
            ],
            out_specs=pl.BlockSpec((1, n_q, D_ATTN), lambda b, pt, ix: (b, 0, 0)),
            scratch_shapes=[
                pltpu.VMEM((2, N_HEADS, n_q * ns, HEAD_DIM), F32),
                pltpu.VMEM((2, N_HEADS, n_q * ns, HEAD_DIM), F32),
                pltpu.SemaphoreType.DMA((2, 2)),
            ],
        ),
        out_shape=jax.ShapeDtypeStruct((nb, n_q, D_ATTN), F32),
        compiler_params=_cparams(("arbitrary",)),
        name="sample_attention",
    )(page_table, idx[:, :n_q].reshape(nb * n_q, ns), idx, seln, q8, knh, vnh, rbt, bnew, ck, cv)


def _sidx_t_kernel(pt_ref, qi_ref, wi_ref, kin_ref, tri_ref, lin_ref, cache_ref, sel_ref,
                   kibuf, sem, keys_ref, *, n_keep, n_pages, n_q, ppc, ncs, ncp):
    b = pl.program_id(0)
    nb = pl.num_programs(0)
    slot = b % 2

    def page_copy(bb, p, sl):
        return pltpu.make_async_copy(cache_ref.at[pt_ref[bb, p]], kibuf.at[sl, p], sem.at[sl])

    def start_all(bb, sl):
        def body(p, c):
            page_copy(bb, p, sl).start()
            return c
        lax.fori_loop(0, n_pages, body, 0, unroll=ppc)

    def wait_all(bb, sl):
        def body(p, c):
            page_copy(bb, p, sl).wait()
            return c
        lax.fori_loop(0, n_pages, body, 0, unroll=ppc)

    @pl.when(b == 0)
    def _():
        start_all(b, slot)

    @pl.when(b + 1 < nb)
    def _():
        start_all(b + 1, 1 - slot)

    wait_all(b, slot)

    qi = qi_ref[0]
    w = wi_ref[0] * IDX_SCALE
    keys_ref[...] = jnp.full(keys_ref.shape, KEY_INVALID, I32)

    def score_rows(s):
        s = jnp.maximum(s, 0.0) * w
        return jnp.sum(s.reshape(n_q, IDX_HEADS, s.shape[-1]), axis=1)

    def chunk_body(c, carry):
        p0 = pl.multiple_of(c * ppc, ppc)
        kt = jnp.concatenate([kibuf[slot, p0 + p] for p in range(ppc)], axis=1).astype(BF16)
        s = score_rows(jnp.dot(qi, kt, preferred_element_type=F32))
        key = _score_to_key(s, jnp.full(s.shape, True))
        for q in range(n_q):
            for p in range(ppc):
                keys_ref[q, pl.ds(p0 + p, 1), :] = key[q:q + 1, p * LANES:(p + 1) * LANES]
        return carry

    lax.fori_loop(0, n_pages // ppc, chunk_body, 0)
    s_new = score_rows(lax.dot_general(qi, kin_ref[0], NT_DIMS, preferred_element_type=F32))
    qrow = lax.broadcasted_iota(I32, (n_q, LANES), 0)
    jcol = lax.broadcasted_iota(I32, (n_q, LANES), 1)
    key_new = _score_to_key(s_new, jcol <= qrow)
    for q in range(n_q):
        keys_ref[q, n_pages:n_pages + 1, :] = key_new[q:q + 1, :]

    shape = (n_q, 1, LANES)

    def count_ge(cand):
        hit = jnp.where(keys_ref[:, 0:ncs, :] >= cand, 1.0, 0.0)
        part = jnp.sum(hit, axis=1, keepdims=True)
        return jnp.broadcast_to(jnp.sum(part, axis=2, keepdims=True), shape)

    base, cnt = _threshold_search(count_ge, float(n_keep), shape)

    @pl.when(jnp.max(cnt) > float(n_keep))
    def _():
        need = float(n_keep) - count_ge(base + 1)
        ones = jnp.ones((LANES, LANES), BF16)
        for q in range(n_q):
            k = keys_ref[q]
            eq = jnp.logical_and(k == base[q], cnt[q] > float(n_keep))
            eqb = jnp.where(eq, 1.0, 0.0).astype(BF16)
            within = jnp.dot(eqb, tri_ref[...], preferred_element_type=F32)
            rowcnt = jnp.dot(eqb, ones, preferred_element_type=F32).astype(BF16)
            before = jnp.dot(lin_ref[...], rowcnt, preferred_element_type=F32) - rowcnt.astype(F32)
            drop = jnp.logical_and(eq, within + before > need[q])
            keys_ref[q] = jnp.where(drop, KEY_INVALID, k)

    sel_ref[0] = jnp.where(keys_ref[:, 0:ncs, :] >= base, 1.0, 0.0)


def _sample_index_t(page_table, qi_rows, wi_rows, ki_new, tri, cache_ikt, n_keep):
    nb, n_pages = page_table.shape
    rows = qi_rows.shape[1]
    n_q = rows // IDX_HEADS
    ppc = 8 if n_pages % 8 == 0 else 1
    ncs = _ceil_to(n_pages + 1, SUBLANES)
    ncp = _ceil_to(n_pages + 1, LANES)
    assert ncp <= 256, "tie-break prefix counts are carried in bf16, exact up to 256"
    ar = jnp.arange(ncp)
    lin = (ar[None, :] <= ar[:, None]).astype(BF16)
    kern = functools.partial(_sidx_t_kernel, n_keep=n_keep, n_pages=n_pages, n_q=n_q, ppc=ppc,
                             ncs=ncs, ncp=ncp)
    return pl.pallas_call(
        kern,
        grid_spec=pltpu.PrefetchScalarGridSpec(
            num_scalar_prefetch=1,
            grid=(nb,),
            in_specs=[
                pl.BlockSpec((1, rows, IDX_DIM), lambda b, pt: (b, 0, 0)),
                pl.BlockSpec((1, rows, 1), lambda b, pt: (b, 0, 0)),
                pl.BlockSpec((1, LANES, IDX_DIM), lambda b, pt: (b, 0, 0)),
                pl.BlockSpec((LANES, LANES), lambda b, pt: (0, 0)),
                pl.BlockSpec((ncp, ncp), lambda b, pt: (0, 0)),
                pl.BlockSpec(memory_space=pl.ANY),
            ],
            out_specs=pl.BlockSpec((1, n_q, ncs, LANES), lambda b, pt: (b, 0, 0, 0)),
            scratch_shapes=[
                pltpu.VMEM((2, n_pages, IDX_DIM, PAGE_SIZE), F32),
                pltpu.SemaphoreType.DMA((2,)),
                pltpu.VMEM((n_q, ncp, LANES), I32),
            ],
        ),
        out_shape=jax.ShapeDtypeStruct((nb, n_q, ncs, LANES), F32),
        compiler_params=_cparams(("arbitrary",)),
        name="sample_index",
    )(page_table, qi_rows, wi_rows, ki_new, tri, lin, cache_ikt)


def _sdense_kernel(pt_ref, qbd_ref, sel_ref, seln_ref, kn_ref, vn_ref, bias_ref, hsel_ref, ck_ref, cv_ref, o_ref,
                   kbuf, vbuf, sem, bias_buf, m_ref, l_ref, acc_ref, *, n_q, n_pages, ppc):
    b = pl.program_id(0)
    c = pl.program_id(1)
    nb = pl.num_programs(0)
    n_chunks = n_pages // ppc
    step = b * n_chunks + c
    slot = step % 2
    rows = n_q * N_HEADS
    ck = ppc * PAGE_SIZE

    def copies(bb, cc, sl, p):
        page = pt_ref[bb, cc * ppc + p]
        return (pltpu.make_async_copy(ck_ref.at[page], kbuf.at[sl, p], sem.at[0, sl]),
                pltpu.make_async_copy(cv_ref.at[page], vbuf.at[sl, p], sem.at[1, sl]))

    def start_all(bb, cc, sl):
        for p in range(ppc):
            kc, vc = copies(bb, cc, sl, p)
            kc.start()
            vc.start()

    def wait_all(bb, cc, sl):
        for p in range(ppc):
            kc, vc = copies(bb, cc, sl, p)
            kc.wait()
            vc.wait()

    @pl.when(step == 0)
    def _():
        start_all(b, c, slot)

    @pl.when(step + 1 < nb * n_chunks)
    def _():
        nxt = step + 1
        start_all(nxt // n_chunks, nxt % n_chunks, 1 - slot)

    @pl.when(c == 0)
    def _():
        m_ref[...] = jnp.full(m_ref.shape, NEG_BIG, F32)
        l_ref[...] = jnp.zeros(l_ref.shape, F32)
        acc_ref[...] = jnp.zeros(acc_ref.shape, F32)
        bias_buf[...] = jnp.broadcast_to(bias_ref[2][:, 0:1], bias_buf.shape)

    @pl.when(c == n_chunks - 1)
    def _():
        bias_buf[:, ck - PAGE_SIZE:ck] = bias_ref[0]

    wait_all(b, c, slot)
    qbd = qbd_ref[0]

    def expand(sel):
        n = sel.shape[-1]
        return jnp.broadcast_to(sel[:, None, :], (n_q, N_HEADS, n)).reshape(rows, n)

    def update(s, live, pv_fn):
        s = jnp.where(live, s, NEG_BIG)
        m_old = m_ref[...]
        m_new = jnp.maximum(m_old, jnp.max(s, axis=1, keepdims=True))
        p = jnp.exp(s - m_new)
        alpha = jnp.exp(m_old - m_new)
        l_ref[...] = alpha * l_ref[...] + jnp.sum(p, axis=1, keepdims=True)
        acc_ref[...] = alpha * acc_ref[...] + pv_fn(p.astype(BF16))
        m_ref[...] = m_new

    s = jnp.concatenate([jnp.dot(qbd, kbuf[slot, p].astype(BF16), preferred_element_type=F32)
                         for p in range(ppc)], axis=1) + bias_buf[...]
    live = jnp.concatenate([expand(sel_ref[0, :, p, :]) for p in range(ppc)], axis=1) > 0.5

    def pv_pages(pb):
        out = jnp.zeros((rows, D_ATTN), F32)
        for p in range(ppc):
            out = out + lax.dot_general(pb[:, p * PAGE_SIZE:(p + 1) * PAGE_SIZE], vbuf[slot, p].astype(BF16),
                                        NT_DIMS, preferred_element_type=F32)
        return out

    update(s, live, pv_pages)

    @pl.when(c == n_chunks - 1)
    def _():
        s_new = lax.dot_general(qbd, kn_ref[0], NT_DIMS, preferred_element_type=F32) + bias_ref[1]
        update(s_new, expand(seln_ref[0, :, 0, :]) > 0.5,
               lambda pb: jnp.dot(pb, vn_ref[0], preferred_element_type=F32))
        full = acc_ref[...] / l_ref[...] * hsel_ref[...]
        o_ref[0] = jnp.sum(full.reshape(n_q, N_HEADS, D_ATTN), axis=1)


def _sample_attention_dense(page_table, qbd, sel, k_new, v_new, sbias, hsel, ckt, cvt):
    nb, n_pages = page_table.shape
    rows = qbd.shape[1]
    n_q = rows // N_HEADS
    ppc = 16
    assert n_pages % ppc == 0, "cached pages are attended in chunks of 16"
    n_chunks = n_pages // ppc
    ck = ppc * PAGE_SIZE
    kern = functools.partial(_sdense_kernel, n_q=n_q, n_pages=n_pages, ppc=ppc)
    return pl.pallas_call(
        kern,
        grid_spec=pltpu.PrefetchScalarGridSpec(
            num_scalar_prefetch=1,
            grid=(nb, n_chunks),
            in_specs=[
                pl.BlockSpec((1, rows, D_ATTN), lambda b, c, pt: (b, 0, 0)),
                pl.BlockSpec((1, n_q, ppc, LANES), lambda b, c, pt: (b, 0, c, 0)),
                pl.BlockSpec((1, n_q, SUBLANES, LANES), lambda b, c, pt: (b, 0, n_pages // SUBLANES, 0)),
                pl.BlockSpec((1, LANES, D_ATTN), lambda b, c, pt: (b, 0, 0)),
                pl.BlockSpec((1, LANES, D_ATTN), lambda b, c, pt: (b, 0, 0)),
                pl.BlockSpec((3, rows, LANES), lambda b, c, pt: (0, 0, 0)),
                pl.BlockSpec((rows, D_ATTN), lambda b, c, pt: (0, 0)),
                pl.BlockSpec(memory_space=pl.ANY),
                pl.BlockSpec(memory_space=pl.ANY),
            ],
            out_specs=pl.BlockSpec((1, n_q, D_ATTN), lambda b, c, pt: (b, 0, 0)),
            scratch_shapes=[
                pltpu.VMEM((2, ppc, D_ATTN, PAGE_SIZE), F32),
                pltpu.VMEM((2, ppc, D_ATTN, PAGE_SIZE), F32),
                pltpu.SemaphoreType.DMA((2, 2)),
                pltpu.VMEM((rows, ck), F32),
                pltpu.VMEM((rows, 1), F32),
                pltpu.VMEM((rows, 1), F32),
                pltpu.VMEM((rows, D_ATTN), F32),
            ],
        ),
        out_shape=jax.ShapeDtypeStruct((nb, n_q, D_ATTN), F32),
        compiler_params=_cparams(("arbitrary", "arbitrary")),
        name="sample_attention",
    )(page_table, qbd, sel, sel, k_new, v_new, sbias, hsel, ckt, cvt)


def _sample_bias_kernel(rbt_ref, o_ref, *, n_q):
    rows = n_q * N_HEADS
    q = lax.broadcasted_iota(I32, (rows, LANES), 0) // N_HEADS
    col = lax.broadcasted_iota(I32, (rows, LANES), 1)
    rb = rbt_ref[...]
    o_ref[0] = _bias_of(PAGE_SIZE + q - col, rb)
    o_ref[1] = _bias_of(q - col, rb)
    o_ref[2] = _bias_of(jnp.full((rows, LANES), REL_MAX_DIST, I32), rb)


def _sample_bias(rbt_rows, n_q):
    rows = n_q * N_HEADS
    return pl.pallas_call(
        functools.partial(_sample_bias_kernel, n_q=n_q),
        out_shape=jax.ShapeDtypeStruct((3, rows, LANES), F32),
        name="sample_bias",
    )(rbt_rows)


def _merge_kernel(o_ref, ya_ref, ga_ref, gb_ref, x_ref, wb_ref, wo_ref, x1_ref):
    yb = jnp.dot(o_ref[...].astype(BF16), wb_ref[...], preferred_element_type=F32)
    m = _sigmoid(ga_ref[...]) * ya_ref[...] + _sigmoid(gb_ref[...]) * yb
    x1_ref[...] = x_ref[...] + jnp.dot(m.astype(BF16), wo_ref[...], preferred_element_type=F32)


def _merge(o2d, ya, gates, x2d, wb, wo, tm):
    n = x2d.shape[0]
    return pl.pallas_call(
        _merge_kernel,
        grid=(n // tm,),
        in_specs=[
            pl.BlockSpec((tm, D_ATTN), lambda i: (i, 0)),
            pl.BlockSpec((tm, D_MODEL), lambda i: (i, 0)),
            pl.BlockSpec((tm, D_MODEL), lambda i: (i, 0)),
            pl.BlockSpec((tm, D_MODEL), lambda i: (i, 1)),
            pl.BlockSpec((tm, D_MODEL), lambda i: (i, 0)),
            pl.BlockSpec((D_ATTN, D_MODEL), lambda i: (0, 0)),
            pl.BlockSpec((D_MODEL, D_MODEL), lambda i: (0, 0)),
        ],
        out_specs=pl.BlockSpec((tm, D_MODEL), lambda i: (i, 0)),
        out_shape=jax.ShapeDtypeStruct((n, D_MODEL), F32),
        compiler_params=_cparams(("parallel",)),
        name="merge_out_proj",
    )(o2d, ya, gates, gates, x2d, wb, wo)


def _ffn_kernel(x_ref, g_ref, wua_ref, wub_ref, fcw_ref, fcb_ref, past_ref, wd_ref, y_ref, tail_ref,
                xn_ref, carry_ref, ext_ref, acc_ref, *, stride, tm, pad, tail_loc):
    t = pl.program_id(1)
    f = pl.program_id(2)
    nf = pl.num_programs(2)

    @pl.when(f == 0)
    def _():
        x = x_ref[...]
        ms = jnp.mean(x * x, axis=-1, keepdims=True)
        xn_ref[...] = (x * lax.rsqrt(ms + EPS) * g_ref[...]).astype(BF16)
        acc_ref[...] = jnp.zeros(acc_ref.shape, F32)

    a = jnp.dot(xn_ref[...], wua_ref[...], preferred_element_type=F32)
    bq = jnp.dot(xn_ref[...], wub_ref[...], preferred_element_type=F32)

    @pl.when(t == 0)
    def _():
        ext_ref[0:pad, :] = past_ref[0]

    @pl.when(t > 0)
    def _():
        ext_ref[0:pad, :] = carry_ref[f]

    ext_ref[pad:pad + tm, :] = a
    carry_ref[f] = ext_ref[tm:tm + pad, :]
    tail_ref[0, 0] = ext_ref[tail_loc:tail_loc + pad, :]

    conv = (ext_ref[pad - 2 * stride:pad - 2 * stride + tm, :] * fcw_ref[0:1, :]
            + ext_ref[pad - stride:pad - stride + tm, :] * fcw_ref[1:2, :]
            + a * fcw_ref[2:3, :] + fcb_ref[...])
    h = conv * _sigmoid(conv) * bq
    acc_ref[...] += jnp.dot(h.astype(BF16), wd_ref[...], preferred_element_type=F32)

    @pl.when(f == nf - 1)
    def _():
        y_ref[...] = x_ref[...] + acc_ref[...]


def _ffn(x1, g, wup, fcw, fcb, past, wd, *, nb, nt, tm, stride, tf, rows_real):
    pad = past.shape[1]
    nf = D_FF // tf
    tail_tile, tail_loc = _tail_position(rows_real, tm)
    kern = functools.partial(_ffn_kernel, stride=stride, tm=tm, pad=pad, tail_loc=tail_loc)
    y, tail = pl.pallas_call(
        kern,
        grid=(nb, nt, nf),
        in_specs=[
            pl.BlockSpec((tm, D_MODEL), lambda b, t, f: (b * nt + t, 0)),
            pl.BlockSpec((1, D_MODEL), lambda b, t, f: (0, 0)),
            pl.BlockSpec((D_MODEL, tf), lambda b, t, f: (0, f)),
            pl.BlockSpec((D_MODEL, tf), lambda b, t, f: (0, nf + f)),
            pl.BlockSpec((SUBLANES, tf), lambda b, t, f: (0, f)),
            pl.BlockSpec((1, tf), lambda b, t, f: (0, f)),
            pl.BlockSpec((1, pad, tf), lambda b, t, f: (b, 0, f)),
            pl.BlockSpec((tf, D_MODEL), lambda b, t, f: (f, 0)),
        ],
        out_specs=[
            pl.BlockSpec((tm, D_MODEL), lambda b, t, f: (b * nt + t, 0)),
            pl.BlockSpec((1, 1, pad, tf), lambda b, t, f: (b, t, 0, f)),
        ],
        out_shape=[jax.ShapeDtypeStruct((nb * nt * tm, D_MODEL), F32),
                   jax.ShapeDtypeStruct((nb, nt, pad, D_FF), F32)],
        scratch_shapes=[
            pltpu.VMEM((tm, D_MODEL), BF16),
            pltpu.VMEM((nf, pad, tf), F32),
            pltpu.VMEM((pad + tm, tf), F32),
            pltpu.VMEM((tm, D_MODEL), F32),
        ],
        compiler_params=_cparams(("parallel", "arbitrary", "arbitrary")),
        name="conv_ffn",
    )(x1, g, wup, wup, fcw, fcb, past, wd)
    return y, tail[:, tail_tile]


def _row_tile(n, cap):
    best = SUBLANES
    for cand in range(SUBLANES, cap + 1, SUBLANES):
        if n % cand == 0:
            best = cand
    return best


def kernel(x_prompt, x_sample, cache_k, cache_v, cache_idx_k, page_table, state_conv, state_ffn_conv, meta_tokens, g_attn_norm, w_in, conv_w, conv_b, conv_ln_g, conv_ln_b, w_a_out, q_norm_g, k_norm_g, rel_bias, w_b_out, w_o, g_ffn_norm, w_up, ffn_conv_w, ffn_conv_b, w_down):
    nbp, seq, _ = x_prompt.shape
    nbs, n_q, _ = x_sample.shape
    n_pages = page_table.shape[1]
    past_len = n_pages * PAGE_SIZE
    t_real = seq + N_META
    tp = _ceil_to(t_real, LANES)
    keep_p = min(TOP_K_MAX, t_real // 4)
    keep_s = min(TOP_K_MAX, (past_len + n_q) // 4)
    assert g_attn_norm.shape[0] == 1, "single trunk layer"

    w = w_in[0]
    c_q, c_v, c_qi, c_ki, c_wi, c_ga = 1024, 2048, 2560, 3072, 3136, 3144
    wm = jnp.concatenate([w[:, :c_ki], w[:, c_ga:]], axis=1).astype(BF16)
    wt = jnp.stack([w[:, c:c + D_ATTN].T for c in (c_q, c_v, c_qi)]).astype(BF16)
    ws = jnp.concatenate([w[:, c_ki:c_wi], jnp.zeros((D_MODEL, LANES - IDX_DIM), F32)], axis=1).astype(BF16)
    wst = w[:, c_wi:c_ga].T.astype(BF16)
    g_attn = g_attn_norm[0][None, :]
    gq = (jnp.tile(q_norm_g[0], N_HEADS) * ATTN_SCALE)[:, None]
    gk = jnp.tile(k_norm_g[0], N_HEADS)[None, :]
    hid = jnp.arange(D_ATTN) // HEAD_DIM
    seg = jnp.where(hid[:, None] == hid[None, :], 1.0 / HEAD_DIM, 0.0).astype(BF16)
    cw = jnp.concatenate([conv_w[0], jnp.zeros((32 - CONV_W, D_CONV), F32)], axis=0)
    cb, lg, lb = conv_b[0][None, :], conv_ln_g[0][None, :], conv_ln_b[0][None, :]
    wa = w_a_out[0].astype(BF16)
    wb = w_b_out[0].astype(BF16)
    wo = w_o[0].astype(BF16)
    g_ffn = g_ffn_norm[0][None, :]
    wup = w_up[0].astype(BF16)
    fcw = jnp.concatenate([ffn_conv_w[0], jnp.zeros((SUBLANES - FFN_CONV_W, D_FF), F32)], axis=0)
    fcb = ffn_conv_b[0][None, :]
    wd = w_down[0].astype(BF16)
    rbt = rel_bias.T
    ar = jnp.arange(LANES)
    tril = (ar[None, :] <= ar[:, None]).astype(BF16)
    tri = tril.T

    def project(x2d):
        return _in_proj(x2d, g_attn, wm, wt, ws, wst, gq, gk, seg, _row_tile(x2d.shape[0], 1024))

    xp = jnp.concatenate([jnp.broadcast_to(meta_tokens[None], (nbp, N_META, D_MODEL)), x_prompt,
                          jnp.zeros((nbp, tp - t_real, D_MODEL), F32)], axis=1).reshape(nbp * tp, D_MODEL)
    n_p = nbp * tp
    a_p, k_p, kb_p, v_p, qt_p, vt_p, qit_p, g2_p, ki_p, kib_p, wit_p = project(xp)

    tm_p = _row_tile(tp, 640)
    nt_p = tp // tm_p
    ya_p, ctail_p = _conv_branch(a_p, jnp.zeros((nbp, 32, D_CONV), F32), cw, cb, lg, lb, wa,
                                 nb=nbp, nt=nt_p, tm=tm_p, stride=1, rows_real=t_real)
    o_p = _prompt_attention(qt_p, kb_p, vt_p, qit_p, kib_p, wit_p, _prompt_bias(rbt), tril, keep_p, nbp, tp)
    x1_p = _merge(o_p, ya_p, g2_p, xp, wb, wo, _row_tile(n_p, 512))
    y_p, ftail_p = _ffn(x1_p, g_ffn, wup, fcw, fcb, jnp.zeros((nbp, SUBLANES, D_FF), F32), wd,
                        nb=nbp, nt=nt_p, tm=tm_p, stride=1, tf=FFN_TILE, rows_real=t_real)

    y_prompt = y_p.reshape(nbp, tp, D_MODEL)[:, N_META:t_real]
    p_k = k_p.reshape(nbp, tp, N_HEADS, HEAD_DIM)[None, :, :t_real]
    p_v = v_p.reshape(nbp, tp, N_HEADS, HEAD_DIM)[None, :, :t_real]
    p_ik = ki_p.reshape(nbp, tp, IDX_DIM)[None, :, :t_real]
    p_cv = ctail_p[None, :, 32 - (CONV_W - 1):]
    p_fc = ftail_p[None, :, SUBLANES - (FFN_CONV_W - 1):]

    n_s = nbs * n_q
    xs = x_sample.transpose(1, 0, 2).reshape(n_s, D_MODEL)
    a_s, k_s2, _, v_s2, qt_s, _, qit_s, g2_s, ki_s2, _, wit_s = project(xs)

    pad_c = _ceil_to((CONV_W - 1) * nbs, SUBLANES)
    past_c = state_conv[0].transpose(1, 0, 2).reshape(1, (CONV_W - 1) * nbs, D_CONV)
    past_c = jnp.pad(past_c, ((0, 0), (pad_c - (CONV_W - 1) * nbs, 0), (0, 0)))
    ya_s, ctail_s = _conv_branch(a_s, past_c, cw, cb, lg, lb, wa, nb=1, nt=1, tm=n_s, stride=nbs,
                                 rows_real=n_s)

    def batch_major(x2d, *tail):
        return x2d.reshape((n_q, nbs) + tail).transpose((1, 0) + tuple(range(2, 2 + len(tail))))

    q_s = batch_major(qt_s.T.astype(F32), N_HEADS, HEAD_DIM)
    k_s = batch_major(k_s2, N_HEADS, HEAD_DIM)
    v_s = batch_major(v_s2, N_HEADS, HEAD_DIM)
    qi_s = batch_major(qit_s.T, IDX_HEADS, IDX_DIM)
    ki_s = batch_major(ki_s2, IDX_DIM)
    wi_s = batch_major(wit_s.T, IDX_HEADS)
    rows = n_q * N_HEADS

    qi_rows = qi_s.reshape(nbs, rows, IDX_DIM)
    wi_rows = wi_s.reshape(nbs, rows, 1)
    ki_new = jnp.pad(ki_s, ((0, 0), (0, LANES - n_q), (0, 0))).astype(BF16)
    n_pool = cache_k.shape[1]
    ckt = cache_k.transpose(0, 1, 3, 4, 2).reshape(n_pool, D_ATTN, PAGE_SIZE)
    cvt = cache_v.transpose(0, 1, 3, 4, 2).reshape(n_pool, D_ATTN, PAGE_SIZE)
    cikt = cache_idx_k.transpose(0, 1, 3, 2).reshape(n_pool, IDX_DIM, PAGE_SIZE)
    sel = _sample_index_t(page_table, qi_rows, wi_rows, ki_new, tri, cikt, keep_s)

    eye = jnp.eye(N_HEADS, dtype=F32)
    qbd = (q_s[:, :, :, None, :] * eye[None, None, :, :, None]).reshape(nbs, rows, D_ATTN).astype(BF16)
    k_new = jnp.pad(k_s.reshape(nbs, n_q, D_ATTN), ((0, 0), (0, LANES - n_q), (0, 0))).astype(BF16)
    v_new = jnp.pad(v_s.reshape(nbs, n_q, D_ATTN), ((0, 0), (0, LANES - n_q), (0, 0))).astype(BF16)
    hsel = (jnp.arange(rows)[:, None] % N_HEADS == hid[None, :]).astype(F32)
    o_s = _sample_attention_dense(page_table, qbd, sel, k_new, v_new, _sample_bias(jnp.tile(rbt, (n_q, 1)), n_q),
                                  hsel, ckt, cvt)
    o_s2d = o_s.transpose(1, 0, 2).reshape(n_s, D_ATTN)

    x1_s = _merge(o_s2d, ya_s, g2_s, xs, wb, wo, _row_tile(n_s, 512))
    pad_f = _ceil_to((FFN_CONV_W - 1) * nbs, SUBLANES)
    past_f = state_ffn_conv[0].transpose(1, 0, 2).reshape(1, (FFN_CONV_W - 1) * nbs, D_FF)
    past_f = jnp.pad(past_f, ((0, 0), (pad_f - (FFN_CONV_W - 1) * nbs, 0), (0, 0)))
    y_s, ftail_s = _ffn(x1_s, g_ffn, wup, fcw, fcb, past_f, wd, nb=1, nt=1, tm=n_s, stride=nbs, tf=FFN_TILE,
                        rows_real=n_s)

    y_sample = y_s.reshape(n_q, nbs, D_MODEL).transpose(1, 0, 2)
    s_cv = ctail_s[0, pad_c - (CONV_W - 1) * nbs:].reshape(CONV_W - 1, nbs, D_CONV).transpose(1, 0, 2)[None]
    s_fc = ftail_s[0, pad_f - (FFN_CONV_W - 1) * nbs:].reshape(FFN_CONV_W - 1, nbs, D_FF).transpose(1, 0, 2)[None]

    return (y_prompt, y_sample, p_k, p_v, p_ik, p_cv, p_fc,
            k_s[None], v_s[None], ki_s[None], s_cv, s_fc)
```

```python
import functools
import math

import jax
import jax.numpy as jnp
from jax import lax
from jax.experimental import pallas as pl
from jax.experimental.pallas import tpu as pltpu

F32 = jnp.float32
BF16 = jnp.bfloat16
I32 = jnp.int32

D_MODEL = 1024
D_CONV = D_MODEL // 2
CONV_W = 31
N_HEADS = 8
HEAD_DIM = 64
D_ATTN = N_HEADS * HEAD_DIM
IDX_HEADS = 8
IDX_DIM = 64
TOP_K_MAX = 256
N_BUCKETS = 32
MAX_EXACT = N_BUCKETS // 2
REL_MAX_DIST = 128
D_FF = 2816
FFN_CONV_W = 3
N_META = 16
PAGE_SIZE = 128
PAGE_SHIFT = 7
GATHER_UNROLL = 8
FFN_TILE = D_FF // 2
SEARCH_HEAD_BITS = 3
SEARCH_GROUP = 4
EPS = 1e-6
ATTN_SCALE = HEAD_DIM ** -0.5
IDX_SCALE = (IDX_HEADS * IDX_DIM) ** -0.5
D_MAIN = 2 * D_CONV + 4 * D_ATTN + 2 * D_MODEL
JB_Q, JB_K, JB_V, JB_QI, JB_G = 2, 3, 4, 5, 6

LANES = 128
SUBLANES = 8
KEY_INVALID = -2 ** 31
NEG_BIG = -1e30
VMEM_LIMIT = 56 * 1024 * 1024
NT_DIMS = (((1,), (1,)), ((), ()))


def _cparams(sem):
    return pltpu.CompilerParams(dimension_semantics=sem, vmem_limit_bytes=VMEM_LIMIT)


def _sigmoid(x):
    return 1.0 / (1.0 + jnp.exp(-x))


def _ceil_to(x, m):
    return -(-x // m) * m


def _split_bf16(x):
    hi = x.astype(BF16)
    return hi, (x - hi.astype(F32)).astype(BF16)


def _in_proj_kernel(x_ref, g_ref, wm_ref, wt_ref, ws_ref, wst_ref, gq_ref, gk_ref, seg_ref,
                    a_ref, k_ref, kb_ref, v_ref, qt_ref, vt_ref, qit_ref, g2_ref, ki_ref, kib_ref, wit_ref,
                    xn_ref):
    j = pl.program_id(1)

    @pl.when(j == 0)
    def _():
        x = x_ref[...]
        ms = jnp.mean(x * x, axis=-1, keepdims=True)
        xn_ref[...] = (x * lax.rsqrt(ms + EPS) * g_ref[...]).astype(BF16)
        ki = jnp.dot(xn_ref[...], ws_ref[...], preferred_element_type=F32)[:, :IDX_DIM]
        ki_ref[...] = ki
        kib_ref[...] = ki.astype(BF16)
        wit_ref[...] = lax.dot_general(wst_ref[...], xn_ref[...], NT_DIMS, preferred_element_type=F32)

    @pl.when(j < JB_Q)
    def _():
        a_ref[...] = jnp.dot(xn_ref[...], wm_ref[...], preferred_element_type=F32)

    @pl.when(j >= JB_G)
    def _():
        g2_ref[...] = jnp.dot(xn_ref[...], wm_ref[...], preferred_element_type=F32)

    @pl.when(j == JB_Q)
    def _():
        yt = lax.dot_general(wt_ref[0], xn_ref[...], NT_DIMS, preferred_element_type=F32)
        hi, lo = _split_bf16(yt * yt)
        seg = seg_ref[...]
        ms = jnp.concatenate(
            [jnp.dot(seg, hi[r:r + LANES], preferred_element_type=F32)
             + jnp.dot(seg, lo[r:r + LANES], preferred_element_type=F32) for r in range(0, D_ATTN, LANES)],
            axis=0)
        qt_ref[...] = (yt * lax.rsqrt(ms + EPS) * gq_ref[...]).astype(BF16)

    @pl.when(j == JB_K)
    def _():
        y = jnp.dot(xn_ref[...], wm_ref[...], preferred_element_type=F32)
        hi, lo = _split_bf16(y * y)
        seg = seg_ref[...]
        ms = jnp.concatenate(
            [jnp.dot(hi[:, c:c + LANES], seg, preferred_element_type=F32)
             + jnp.dot(lo[:, c:c + LANES], seg, preferred_element_type=F32) for c in range(0, D_ATTN, LANES)],
            axis=1)
        k = y * lax.rsqrt(ms + EPS) * gk_ref[...]
        k_ref[...] = k
        kb_ref[...] = k.astype(BF16)

    @pl.when(j == JB_V)
    def _():
        v_ref[...] = jnp.dot(xn_ref[...], wm_ref[...], preferred_element_type=F32)
        vt_ref[...] = lax.dot_general(wt_ref[1], xn_ref[...], NT_DIMS, preferred_element_type=F32).astype(BF16)

    @pl.when(j == JB_QI)
    def _():
        qit_ref[...] = lax.dot_general(wt_ref[2], xn_ref[...], NT_DIMS, preferred_element_type=F32).astype(BF16)


def _in_proj(x2d, g, wm, wt, ws, wst, gq, gk, seg, tm):
    n = x2d.shape[0]
    tn = D_ATTN
    nj = D_MAIN // tn
    row = lambda i, j: (i, 0)
    colt = lambda i, j: (0, i)
    const2 = lambda i, j: (0, 0)
    out_shape = [
        jax.ShapeDtypeStruct((n, 2 * D_CONV), F32),
        jax.ShapeDtypeStruct((n, D_ATTN), F32),
        jax.ShapeDtypeStruct((n, D_ATTN), BF16),
        jax.ShapeDtypeStruct((n, D_ATTN), F32),
        jax.ShapeDtypeStruct((D_ATTN, n), BF16),
        jax.ShapeDtypeStruct((D_ATTN, n), BF16),
        jax.ShapeDtypeStruct((D_ATTN, n), BF16),
        jax.ShapeDtypeStruct((n, 2 * D_MODEL), F32),
        jax.ShapeDtypeStruct((n, IDX_DIM), F32),
        jax.ShapeDtypeStruct((n, IDX_DIM), BF16),
        jax.ShapeDtypeStruct((IDX_HEADS, n), F32),
    ]
    out_specs = [
        pl.BlockSpec((tm, tn), lambda i, j: (i, jnp.minimum(j, 1))),
        pl.BlockSpec((tm, tn), row),
        pl.BlockSpec((tm, tn), row),
        pl.BlockSpec((tm, tn), row),
        pl.BlockSpec((tn, tm), colt),
        pl.BlockSpec((tn, tm), colt),
        pl.BlockSpec((tn, tm), colt),
        pl.BlockSpec((tm, tn), lambda i, j: (i, jnp.clip(j - JB_G, 0, nj - JB_G - 1))),
        pl.BlockSpec((tm, IDX_DIM), row),
        pl.BlockSpec((tm, IDX_DIM), row),
        pl.BlockSpec((IDX_HEADS, tm), colt),
    ]
    return pl.pallas_call(
        _in_proj_kernel,
        grid=(n // tm, nj),
        in_specs=[
            pl.BlockSpec((tm, D_MODEL), row),
            pl.BlockSpec((1, D_MODEL), const2),
            pl.BlockSpec((D_MODEL, tn), lambda i, j: (0, j)),
            pl.BlockSpec((3, tn, D_MODEL), lambda i, j: (0, 0, 0)),
            pl.BlockSpec((D_MODEL, LANES), const2),
            pl.BlockSpec((IDX_HEADS, D_MODEL), const2),
            pl.BlockSpec((D_ATTN, 1), const2),
            pl.BlockSpec((1, D_ATTN), const2),
            pl.BlockSpec((LANES, LANES), const2),
        ],
        out_specs=out_specs,
        out_shape=out_shape,
        scratch_shapes=[pltpu.VMEM((tm, D_MODEL), BF16)],
        compiler_params=_cparams(("parallel", "arbitrary")),
        name="in_proj",
    )(x2d, g, wm, wt, ws, wst, gq, gk, seg)


def _conv_kernel(a_ref, past_ref, cw_ref, cb_ref, lg_ref, lb_ref, wa_ref, ya_ref, tail_ref, ext_ref, h_ref,
                 *, stride, tm, pad, rc, tail_tile, tail_loc):
    t = pl.program_id(1)

    @pl.when(t == 0)
    def _():
        ext_ref[0, 0:pad, :] = past_ref[0]

    @pl.when(t > 0)
    def _():
        ext_ref[0, 0:pad, :] = ext_ref[0, tm:tm + pad, :]

    a = a_ref[...]
    ext_ref[0, pad:pad + tm, :] = a[:, :D_CONV] * _sigmoid(a[:, D_CONV:])

    @pl.when(t <= tail_tile)
    def _():
        tail_ref[0] = ext_ref[0, tail_loc:tail_loc + pad, :]

    n_shift = ext_ref.shape[0]
    length = pad + tm
    for r in range(1, n_shift):
        ext_ref[r, 0:length - SUBLANES, :] = ext_ref[0, r:r + length - SUBLANES, :]

    cb = cb_ref[...]
    lg = lg_ref[...]
    lb = lb_ref[...]
    for r0 in range(0, tm, rc):
        acc = jnp.zeros((rc, D_CONV), F32) + cb
        for w in range(CONV_W):
            off = pad - (CONV_W - 1 - w) * stride + r0
            r = off % n_shift
            acc = acc + ext_ref[r, off - r:off - r + rc, :] * cw_ref[w:w + 1, :]
        mu = jnp.mean(acc, axis=-1, keepdims=True)
        d = acc - mu
        var = jnp.mean(d * d, axis=-1, keepdims=True)
        h = d * lax.rsqrt(var + EPS) * lg + lb
        h_ref[r0:r0 + rc, :] = (h * _sigmoid(h)).astype(BF16)
    ya_ref[...] = jnp.dot(h_ref[...], wa_ref[...], preferred_element_type=F32)


def _tail_position(rows_real, tm):
    tail_tile = (rows_real - 1) // tm
    return tail_tile, rows_real - tail_tile * tm


def _conv_branch(a_in, past, cw, cb, lg, lb, wa, *, nb, nt, tm, stride, rows_real):
    pad = past.shape[1]
    rc = 32 if tm % 32 == 0 else SUBLANES
    tail_tile, tail_loc = _tail_position(rows_real, tm)
    kern = functools.partial(_conv_kernel, stride=stride, tm=tm, pad=pad, rc=rc,
                             tail_tile=tail_tile, tail_loc=tail_loc)
    return pl.pallas_call(
        kern,
        grid=(nb, nt),
        in_specs=[
            pl.BlockSpec((tm, 2 * D_CONV), lambda b, t: (b * nt + t, 0)),
            pl.BlockSpec((1, pad, D_CONV), lambda b, t: (b, 0, 0)),
            pl.BlockSpec((32, D_CONV), lambda b, t: (0, 0)),
            pl.BlockSpec((1, D_CONV), lambda b, t: (0, 0)),
            pl.BlockSpec((1, D_CONV), lambda b, t: (0, 0)),
            pl.BlockSpec((1, D_CONV), lambda b, t: (0, 0)),
            pl.BlockSpec((D_CONV, D_MODEL), lambda b, t: (0, 0)),
        ],
        out_specs=[
            pl.BlockSpec((tm, D_MODEL), lambda b, t: (b * nt + t, 0)),
            pl.BlockSpec((1, pad, D_CONV), lambda b, t: (b, 0, 0)),
        ],
        out_shape=[jax.ShapeDtypeStruct((nb * nt * tm, D_MODEL), F32),
                   jax.ShapeDtypeStruct((nb, pad, D_CONV), F32)],
        scratch_shapes=[pltpu.VMEM((1 if stride % SUBLANES == 0 else SUBLANES, pad + tm, D_CONV), F32),
                        pltpu.VMEM((tm, D_CONV), BF16)],
        compiler_params=_cparams(("parallel", "arbitrary")),
        name="conv_branch",
    )(a_in, past, cw, cb, lg, lb, wa)


def _rel_bucket(rel):
    n = jnp.maximum(rel, 0)
    nf = jnp.maximum(n, 1).astype(F32)
    large = MAX_EXACT + (jnp.log(nf / MAX_EXACT) / math.log(REL_MAX_DIST / MAX_EXACT)
                         * (N_BUCKETS - MAX_EXACT)).astype(I32)
    large = jnp.minimum(large, N_BUCKETS - 1)
    return jnp.where(n < MAX_EXACT, n, large)


def _bias_of(rel, rb_rows):
    bucket = _rel_bucket(rel)
    out = jnp.zeros(rel.shape, F32)
    for b in range(N_BUCKETS):
        out = jnp.where(bucket == b, rb_rows[:, b:b + 1], out)
    return out


def _prompt_bias_kernel(rbt_ref, b1_ref):
    key = lax.broadcasted_iota(I32, (LANES, LANES), 0)
    qry = lax.broadcasted_iota(I32, (LANES, LANES), 1)
    for h in range(N_HEADS):
        for d in range(3):
            b1_ref[h * 3 + d] = _bias_of(d * LANES + qry - key, rbt_ref[h:h + 1, :])


def _prompt_bias(rbt):
    return pl.pallas_call(
        _prompt_bias_kernel,
        out_shape=jax.ShapeDtypeStruct((N_HEADS * 3, LANES, LANES), F32),
        name="prompt_bias",
    )(rbt)


def _sample_new_bias_kernel(rbt_ref, o_ref, *, n_q):
    col = lax.broadcasted_iota(I32, (N_HEADS, LANES), 1)
    for q in range(n_q):
        o_ref[q] = _bias_of(q - col, rbt_ref[...])


def _sample_new_bias(rbt, n_q):
    return pl.pallas_call(
        functools.partial(_sample_new_bias_kernel, n_q=n_q),
        out_shape=jax.ShapeDtypeStruct((n_q, N_HEADS, LANES), F32),
        name="sample_bias",
    )(rbt)


def _score_to_key(s, valid):
    bits = pltpu.bitcast(s, I32)
    key = bits ^ ((bits >> 31) & 0x7FFFFFFF)
    key = jnp.where(bits == KEY_INVALID, 0, key)
    return jnp.where(valid, key, KEY_INVALID)


def _threshold_search(count_ge, n_keep, shape):
    cnt_valid = count_ge(jnp.full(shape, KEY_INVALID + 1, I32))
    c0 = count_ge(jnp.zeros(shape, I32))
    ok0 = c0 >= n_keep
    base = jnp.where(ok0, 0, KEY_INVALID).astype(I32)
    cnt = jnp.where(ok0, c0, cnt_valid)

    def step(t, base, cnt):
        cand = base | lax.shift_left(jnp.int32(1), 30 - t)
        c = count_ge(cand)
        ok = c >= n_keep
        return jnp.where(ok, cand, base), jnp.where(ok, c, cnt)

    for t in range(SEARCH_HEAD_BITS):
        base, cnt = step(t, base, cnt)

    def cond(state):
        t, _, _, unsettled = state
        return jnp.logical_and(t < 31, unsettled)

    def body(state):
        t, base, cnt, _ = state
        for u in range(SEARCH_GROUP):
            base, cnt = step(t + u, base, cnt)
        return t + SEARCH_GROUP, base, cnt, jnp.max(cnt) > n_keep

    _, base, cnt, _ = lax.while_loop(cond, body, (jnp.int32(SEARCH_HEAD_BITS), base, cnt, jnp.max(cnt) > n_keep))
    return jnp.maximum(base, KEY_INVALID + 1), cnt


def _pattn_kernel(qt_ref, k_ref, vt_ref, qit_ref, ki_ref, wit_ref, b1_ref, tril_ref, o_ref,
                  keys_ref, qiw_ref, qbd_ref, m_ref, l_ref, acc_ref, ot_ref, p_ref, alpha_ref, *, n_keep):
    i = pl.program_id(1)
    tq = LANES
    row1 = (1, tq)
    lead = (i + 1) % 2
    npair = (i + 1) // 2
    qpos = i * tq + lax.broadcasted_iota(I32, row1, 1)

    def for_chunks(fn):
        @pl.when(lead == 1)
        def _():
            fn(0, LANES)

        def body(j, carry):
            fn(pl.multiple_of(lead * LANES + j * 2 * LANES, LANES), 2 * LANES)
            return carry
        lax.fori_loop(0, npair, body, 0)

    for h in range(IDX_HEADS):
        qiw_ref[:, h * tq:(h + 1) * tq] = qit_ref[h * IDX_DIM:(h + 1) * IDX_DIM, :]
    qbd_ref[...] = jnp.zeros(qbd_ref.shape, BF16)
    for h in range(N_HEADS):
        hp, e = divmod(h, 2)
        qbd_ref[hp, e * HEAD_DIM:(e + 1) * HEAD_DIM, e * tq:(e + 1) * tq] = qt_ref[h * HEAD_DIM:(h + 1) * HEAD_DIM, :]
    wit = wit_ref[...] * IDX_SCALE

    def idx_chunk(off, ck):
        sall = jnp.dot(ki_ref[pl.ds(off, ck), :], qiw_ref[...], preferred_element_type=F32)
        s = jnp.zeros((ck, tq), F32)
        for h in range(IDX_HEADS):
            s = s + wit[h:h + 1, :] * jnp.maximum(sall[:, h * tq:(h + 1) * tq], 0.0)
        kpos = off + lax.broadcasted_iota(I32, (ck, tq), 0)
        keys_ref[pl.ds(off, ck), :] = _score_to_key(s, kpos <= qpos)

    for_chunks(idx_chunk)

    blk = (LANES, tq)
    nblk = i + 1

    def count_ge(cand):
        cb = jnp.broadcast_to(cand, blk)

        def hit(off):
            return jnp.where(keys_ref[pl.ds(off, LANES), :] >= cb, 1.0, 0.0)

        def body(j, acc):
            off = pl.multiple_of(lead * LANES + j * 2 * LANES, LANES)
            return acc + hit(off) + hit(off + LANES)
        acc = lax.fori_loop(0, npair, body, jnp.where(lead == 1, hit(0), 0.0))
        return jnp.sum(acc, axis=0, keepdims=True)

    base, cnt = _threshold_search(count_ge, float(n_keep), row1)

    @pl.when(jnp.max(cnt) > float(n_keep))
    def _():
        need = float(n_keep) - count_ge(base + 1)
        tie_q = cnt > float(n_keep)

        def body(j, seen):
            off = pl.multiple_of(j * LANES, LANES)
            k = keys_ref[pl.ds(off, LANES), :]
            eq = jnp.logical_and(k == base, tie_q)
            eqf = jnp.where(eq, 1.0, 0.0)
            pref = seen + jnp.dot(tril_ref[...], eqf.astype(BF16), preferred_element_type=F32)
            keys_ref[pl.ds(off, LANES), :] = jnp.where(jnp.logical_and(eq, pref > need), KEY_INVALID, k)
            return seen + jnp.sum(eqf, axis=0, keepdims=True)
        lax.fori_loop(0, nblk, body, jnp.zeros(row1, F32))

    m_ref[...] = jnp.full(m_ref.shape, NEG_BIG, F32)
    l_ref[...] = jnp.zeros(l_ref.shape, F32)
    acc_ref[...] = jnp.zeros(acc_ref.shape, F32)
    p_ref[...] = jnp.zeros(p_ref.shape, BF16)
    alpha_ref[...] = jnp.ones(alpha_ref.shape, F32)
    base_b = jnp.broadcast_to(base, blk)

    def apply_pv(jb):
        off = pl.multiple_of(jb * LANES, LANES)
        for h in range(N_HEADS):
            pv = jnp.dot(vt_ref[h * HEAD_DIM:(h + 1) * HEAD_DIM, pl.ds(off, LANES)], p_ref[h],
                         preferred_element_type=F32)
            acc_ref[h] = alpha_ref[h] * acc_ref[h] + pv

    def att_body(j, carry):
        apply_pv(jnp.maximum(j - 1, 0))
        off = pl.multiple_of(j * LANES, LANES)
        sel = keys_ref[pl.ds(off, LANES), :] >= base_b
        d = jnp.minimum(i - j, 2)
        for hp in range(N_HEADS // 2):
            for e in range(2):
                h = 2 * hp + e
                s1 = jnp.dot(k_ref[pl.ds(off, LANES), hp * LANES:(hp + 1) * LANES],
                             qbd_ref[hp, :, e * tq:(e + 1) * tq], preferred_element_type=F32)
                s = jnp.where(sel, s1 + b1_ref[h * 3 + d], NEG_BIG)
                m_old = m_ref[h]
                m_new = jnp.maximum(m_old, jnp.max(s, axis=0, keepdims=True))
                p = jnp.exp(s - m_new)
                alpha = jnp.exp(m_old - m_new)
                l_ref[h] = alpha * l_ref[h] + jnp.sum(p, axis=0, keepdims=True)
                p_ref[h] = p.astype(BF16)
                alpha_ref[h] = alpha
                m_ref[h] = m_new
        return carry

    lax.fori_loop(0, nblk, att_body, 0)
    apply_pv(nblk - 1)
    for h in range(N_HEADS):
        ot_ref[h * HEAD_DIM:(h + 1) * HEAD_DIM, :] = acc_ref[h] / l_ref[h]
    o_ref[...] = ot_ref[...].T


def _prompt_attention(qt, kb, vt, qit, kib, wit, b1, tril, n_keep, nb, tp):
    tq = LANES
    nq = tp // tq
    n = nb * tp
    qcol = lambda b, i: (0, b * nq + i)
    return pl.pallas_call(
        functools.partial(_pattn_kernel, n_keep=n_keep),
        grid=(nb, nq),
        in_specs=[
            pl.BlockSpec((D_ATTN, tq), qcol),
            pl.BlockSpec((tp, D_ATTN), lambda b, i: (b, 0)),
            pl.BlockSpec((D_ATTN, tp), lambda b, i: (0, b)),
            pl.BlockSpec((IDX_HEADS * IDX_DIM, tq), qcol),
            pl.BlockSpec((tp, IDX_DIM), lambda b, i: (b, 0)),
            pl.BlockSpec((IDX_HEADS, tq), qcol),
            pl.BlockSpec((N_HEADS * 3, LANES, LANES), lambda b, i: (0, 0, 0)),
            pl.BlockSpec((LANES, LANES), lambda b, i: (0, 0)),
        ],
        out_specs=pl.BlockSpec((tq, D_ATTN), lambda b, i: (b * nq + i, 0)),
        out_shape=jax.ShapeDtypeStruct((n, D_ATTN), F32),
        scratch_shapes=[
            pltpu.VMEM((tp, tq), I32),
            pltpu.VMEM((IDX_DIM, IDX_HEADS * tq), BF16),
            pltpu.VMEM((N_HEADS // 2, LANES, 2 * tq), BF16),
            pltpu.VMEM((N_HEADS, 1, tq), F32),
            pltpu.VMEM((N_HEADS, 1, tq), F32),
            pltpu.VMEM((N_HEADS, HEAD_DIM, tq), F32),
            pltpu.VMEM((D_ATTN, tq), F32),
            pltpu.VMEM((N_HEADS, LANES, tq), BF16),
            pltpu.VMEM((N_HEADS, 1, tq), F32),
        ],
        compiler_params=_cparams(("parallel", "arbitrary")),
        name="prompt_attention",
    )(qt, kb, vt, qit, kib, wit, b1, tril)


def _sidx_kernel(pt_ref, qi_ref, wi_ref, kin_ref, tri_ref, lin_ref, cache_ref, idx_ref, seln_ref,
                 kibuf, sem, keys_ref, *, n_keep, n_pages, n_q, ppc, ncs, ncp, ns):
    b = pl.program_id(0)
    nb = pl.num_programs(0)
    slot = b % 2

    def page_copy(bb, p, sl):
        return pltpu.make_async_copy(cache_ref.at[0, pt_ref[bb, p]], kibuf.at[sl, p], sem.at[sl])

    def start_all(bb, sl):
        def body(p, c):
            page_copy(bb, p, sl).start()
            return c
        lax.fori_loop(0, n_pages, body, 0)

    def wait_all(bb, sl):
        def body(p, c):
            page_copy(bb, p, sl).wait()
            return c
        lax.fori_loop(0, n_pages, body, 0)

    @pl.when(b == 0)
    def _():
        start_all(b, slot)

    @pl.when(b + 1 < nb)
    def _():
        start_all(b + 1, 1 - slot)

    wait_all(b, slot)

    qi = qi_ref[0]
    w = wi_ref[0] * IDX_SCALE
    ck = ppc * PAGE_SIZE
    keys_ref[...] = jnp.full(keys_ref.shape, KEY_INVALID, I32)

    def score_rows(kc):
        s = lax.dot_general(qi, kc, NT_DIMS, preferred_element_type=F32)
        s = jnp.maximum(s, 0.0) * w
        return jnp.sum(s.reshape(n_q, IDX_HEADS, s.shape[-1]), axis=1)

    def chunk_body(c, carry):
        p0 = pl.multiple_of(c * ppc, ppc)
        kc = kibuf[slot, pl.ds(p0, ppc)].reshape(ck, IDX_DIM).astype(BF16)
        key = _score_to_key(score_rows(kc), jnp.full((n_q, ck), True))
        for q in range(n_q):
            for p in range(ppc):
                keys_ref[q, pl.ds(p0 + p, 1), :] = key[q:q + 1, p * LANES:(p + 1) * LANES]
        return carry

    lax.fori_loop(0, n_pages // ppc, chunk_body, 0)
    s_new = score_rows(kin_ref[0])
    qrow = lax.broadcasted_iota(I32, (n_q, LANES), 0)
    jcol = lax.broadcasted_iota(I32, (n_q, LANES), 1)
    key_new = _score_to_key(s_new, jcol <= qrow)
    for q in range(n_q):
        keys_ref[q, n_pages:n_pages + 1, :] = key_new[q:q + 1, :]

    shape = (n_q, 1, LANES)

    def count_ge(cand):
        hit = jnp.where(keys_ref[:, 0:ncs, :] >= cand, 1.0, 0.0)
        part = jnp.sum(hit, axis=1, keepdims=True)
        return jnp.broadcast_to(jnp.sum(part, axis=2, keepdims=True), shape)

    base, cnt = _threshold_search(count_ge, float(n_keep), shape)

    @pl.when(jnp.max(cnt) > float(n_keep))
    def _():
        need = float(n_keep) - count_ge(base + 1)
        ones = jnp.ones((LANES, LANES), BF16)
        for q in range(n_q):
            k = keys_ref[q]
            eq = jnp.logical_and(k == base[q], cnt[q] > float(n_keep))
            eqb = jnp.where(eq, 1.0, 0.0).astype(BF16)
            within = jnp.dot(eqb, tri_ref[...], preferred_element_type=F32)
            rowcnt = jnp.dot(eqb, ones, preferred_element_type=F32).astype(BF16)
            before = jnp.dot(lin_ref[...], rowcnt, preferred_element_type=F32) - rowcnt.astype(F32)
            drop = jnp.logical_and(eq, within + before > need[q])
            keys_ref[q] = jnp.where(drop, KEY_INVALID, k)

    lane_c = lax.broadcasted_iota(I32, (ns, ncp), 1).astype(F32)
    slot_c = lax.broadcasted_iota(I32, (ns, ncp), 0).astype(F32)
    slot_l = lax.broadcasted_iota(I32, (ns, LANES), 0).astype(F32)
    ones8 = jnp.ones((SUBLANES, LANES), BF16)
    lane8 = lax.broadcasted_iota(I32, (SUBLANES, LANES), 1).astype(BF16)
    for q in range(n_q):
        selb = jnp.where(keys_ref[q] >= base[q], 1.0, 0.0).astype(BF16)
        seln_ref[0, q:q + 1, :] = selb[n_pages:n_pages + 1, :].astype(F32)
        within = jnp.dot(selb, tri_ref[...], preferred_element_type=F32)
        upto = jnp.dot(lin_ref[...], selb, preferred_element_type=F32).astype(BF16)
        pin_row = lax.dot_general(ones8, upto, NT_DIMS, preferred_element_type=F32)[0:1, :]
        chunk_of = jnp.sum(jnp.where(pin_row <= slot_c, 1.0, 0.0), axis=1, keepdims=True)
        onehot = jnp.where(lane_c == chunk_of, 1.0, 0.0).astype(BF16)
        g_sel = jnp.dot(onehot, selb, preferred_element_type=F32)
        g_within = jnp.dot(onehot, within.astype(BF16), preferred_element_type=F32)
        g_upto = jnp.dot(onehot, upto, preferred_element_type=F32)
        g_pin = jnp.sum(g_upto, axis=1, keepdims=True)
        g_cnt = jnp.sum(g_sel, axis=1, keepdims=True)
        target = slot_l - (g_pin - g_cnt) + 1.0
        match = jnp.logical_and(g_sel > 0.5, g_within == target)
        matchb = jnp.where(match, 1.0, 0.0).astype(BF16)
        row_l = lax.dot_general(lane8, matchb, NT_DIMS, preferred_element_type=F32)
        chunk_b = jnp.broadcast_to(chunk_of, (ns, LANES))
        row_c = lax.dot_general(ones8, jnp.where(match, chunk_b, 0.0).astype(BF16), NT_DIMS,
                                preferred_element_type=F32)
        idx_ref[0, q:q + 1, :] = (row_c * float(LANES) + row_l)[0:1, :].astype(I32)
    for q in range(n_q, SUBLANES):
        idx_ref[0, q:q + 1, :] = jnp.zeros((1, ns), I32)
        seln_ref[0, q:q + 1, :] = jnp.zeros((1, LANES), F32)


def _sample_index(page_table, qi_rows, wi_rows, ki_new, tri, cache_ik, n_keep):
    nb, n_pages = page_table.shape
    rows = qi_rows.shape[1]
    n_q = rows // IDX_HEADS
    ppc = 8 if n_pages % 8 == 0 else 1
    ncs = _ceil_to(n_pages + 1, SUBLANES)
    ncp = _ceil_to(n_pages + 1, LANES)
    ns = _ceil_to(n_keep, LANES)
    assert ncp <= 256 and ns <= 256, "prefix counts are carried in bf16, exact up to 256"
    ar = jnp.arange(ncp)
    lin = (ar[None, :] <= ar[:, None]).astype(BF16)
    kern = functools.partial(_sidx_kernel, n_keep=n_keep, n_pages=n_pages, n_q=n_q, ppc=ppc,
                             ncs=ncs, ncp=ncp, ns=ns)
    return pl.pallas_call(
        kern,
        grid_spec=pltpu.PrefetchScalarGridSpec(
            num_scalar_prefetch=1,
            grid=(nb,),
            in_specs=[
                pl.BlockSpec((1, rows, IDX_DIM), lambda b, pt: (b, 0, 0)),
                pl.BlockSpec((1, rows, 1), lambda b, pt: (b, 0, 0)),
                pl.BlockSpec((1, LANES, IDX_DIM), lambda b, pt: (b, 0, 0)),
                pl.BlockSpec((LANES, LANES), lambda b, pt: (0, 0)),
                pl.BlockSpec((ncp, ncp), lambda b, pt: (0, 0)),
                pl.BlockSpec(memory_space=pl.ANY),
            ],
            out_specs=[pl.BlockSpec((1, SUBLANES, ns), lambda b, pt: (b, 0, 0)),
                       pl.BlockSpec((1, SUBLANES, LANES), lambda b, pt: (b, 0, 0))],
            scratch_shapes=[
                pltpu.VMEM((2, n_pages, PAGE_SIZE, IDX_DIM), F32),
                pltpu.SemaphoreType.DMA((2,)),
                pltpu.VMEM((n_q, ncp, LANES), I32),
            ],
        ),
        out_shape=[jax.ShapeDtypeStruct((nb, SUBLANES, ns), I32),
                   jax.ShapeDtypeStruct((nb, SUBLANES, LANES), F32)],
        compiler_params=_cparams(("arbitrary",)),
        name="sample_index",
    )(page_table, qi_rows, wi_rows, ki_new, tri, lin, cache_ik)


def _sgather_kernel(pt_ref, idx_smem, idx_ref, seln_ref, q8_ref, kn_ref, vn_ref, rbt_ref, bnew_ref, ck_ref, cv_ref,
                    o_ref, kbuf, vbuf, sem, *, n_keep, n_q, ns, past):
    b = pl.program_id(0)
    nb = pl.num_programs(0)
    slot = b % 2

    def start_all(bb, sl):
        for q in range(n_q):
            def body(s, carry):
                pos = jnp.minimum(idx_smem[bb * n_q + q, s], past - 1)
                page = pt_ref[bb, lax.shift_right_logical(pos, PAGE_SHIFT)]
                off = lax.bitwise_and(pos, PAGE_SIZE - 1)
                pltpu.make_async_copy(ck_ref.at[0, page, off], kbuf.at[sl, :, q * ns + s], sem.at[0, sl]).start()
                pltpu.make_async_copy(cv_ref.at[0, page, off], vbuf.at[sl, :, q * ns + s], sem.at[1, sl]).start()
                return carry
            lax.fori_loop(0, n_keep, body, 0, unroll=GATHER_UNROLL if n_keep % GATHER_UNROLL == 0 else 1)

    def wait_all(sl):
        rows = pl.ds(0, n_q * n_keep)
        pltpu.make_async_copy(kbuf.at[sl, :, rows], kbuf.at[sl, :, rows], sem.at[0, sl]).wait()
        pltpu.make_async_copy(vbuf.at[sl, :, rows], vbuf.at[sl, :, rows], sem.at[1, sl]).wait()

    @pl.when(b == 0)
    def _():
        if ns > n_keep:
            kbuf[...] = jnp.zeros(kbuf.shape, F32)
            vbuf[...] = jnp.zeros(vbuf.shape, F32)
        start_all(b, slot)

    @pl.when(b + 1 < nb)
    def _():
        start_all(b + 1, 1 - slot)

    wait_all(slot)

    lane = lax.broadcasted_iota(I32, (1, ns), 1)
    for q in range(n_q):
        pos = idx_ref[0, q:q + 1, :]
        live = jnp.logical_and(lane < n_keep, pos < past)
        bias = _bias_of(jnp.broadcast_to(past + q - pos, (N_HEADS, ns)), rbt_ref[...])
        s = jnp.zeros((N_HEADS, ns), F32)
        s_new = jnp.zeros((N_HEADS, LANES), F32)
        for h in range(N_HEADS):
            qh = q8_ref[0, q, h]
            kh = kbuf[slot, h, pl.ds(q * ns, ns), :].astype(BF16)
            s = s + lax.dot_general(qh, kh, NT_DIMS, preferred_element_type=F32)
            s_new = s_new + lax.dot_general(qh, kn_ref[0, h], NT_DIMS, preferred_element_type=F32)
        s = jnp.where(live, s + bias, NEG_BIG)
        s_new = jnp.where(seln_ref[0, q:q + 1, :] > 0.5, s_new + bnew_ref[q], NEG_BIG)
        m = jnp.maximum(jnp.max(s, axis=1, keepdims=True), jnp.max(s_new, axis=1, keepdims=True))
        p = jnp.exp(s - m)
        p_new = jnp.exp(s_new - m)
        denom = jnp.sum(p, axis=1, keepdims=True) + jnp.sum(p_new, axis=1, keepdims=True)
        pb = p.astype(BF16)
        pnb = p_new.astype(BF16)
        for h in range(N_HEADS):
            vh = vbuf[slot, h, pl.ds(q * ns, ns), :].astype(BF16)
            o8 = (jnp.dot(pb, vh, preferred_element_type=F32)
                  + jnp.dot(pnb, vn_ref[0, h], preferred_element_type=F32)) / denom
            o_ref[0, q:q + 1, h * HEAD_DIM:(h + 1) * HEAD_DIM] = o8[h:h + 1, :]


def _sample_attention(page_table, idx, seln, q8, knh, vnh, rbt, bnew, ck, cv, n_keep, past):
    nb = page_table.shape[0]
    n_q = q8.shape[1]
    ns = idx.shape[2]
    kern = functools.partial(_sgather_kernel, n_keep=n_keep, n_q=n_q, ns=ns, past=past)
    return pl.pallas_call(
        kern,
        grid_spec=pltpu.PrefetchScalarGridSpec(
            num_scalar_prefetch=2,
            grid=(nb,),
            in_specs=[
                pl.BlockSpec((1, SUBLANES, ns), lambda b, pt, ix: (b, 0, 0)),
                pl.BlockSpec((1, SUBLANES, LANES), lambda b, pt, ix: (b, 0, 0)),
                pl.BlockSpec((1, n_q, N_HEADS, SUBLANES, HEAD_DIM), lambda b, pt, ix: (b, 0, 0, 0, 0)),
                pl.BlockSpec((1, N_HEADS, LANES, HEAD_DIM), lambda b, pt, ix: (b, 0, 0, 0)),
                pl.BlockSpec((1, N_HEADS, LANES, HEAD_DIM), lambda b, pt, ix: (b, 0, 0, 0)),
                pl.BlockSpec((N_HEADS, N_BUCKETS), lambda b, pt, ix: (0, 0)),
                pl.BlockSpec((n_q, N_HEADS, LANES), lambda b, pt, ix: (0, 0, 0)),
                pl.BlockSpec(memory_space=pl.ANY),
                pl.BlockSpec(memory_space=pl.ANY),
            ],
            out_specs=pl.BlockSpec((1, n_q, D_ATTN), lambda b, pt, ix: (b, 0, 0)),
            scratch_shapes=[
                pltpu.VMEM((2, N_HEADS, n_q * ns, HEAD_DIM), F32),
                pltpu.VMEM((2, N_HEADS, n_q * ns, HEAD_DIM), F32),
                pltpu.SemaphoreType.DMA((2, 2)),
            ],
        ),
        out_shape=jax.ShapeDtypeStruct((nb, n_q, D_ATTN), F32),
        compiler_params=_cparams(("arbitrary",)),
        name="sample_attention",
    )(page_table, idx[:, :n_q].reshape(nb * n_q, ns), idx, seln, q8, knh, vnh, rbt, bnew, ck, cv)


def _sidx_t_kernel(pt_ref, qi_ref, wi_ref, kin_ref, tri_ref, lin_ref, cache_ref, sel_ref,
                   kibuf, sem, keys_ref, *, n_keep, n_pages, n_q, ppc, ncs, ncp):
    b = pl.program_id(0)
    nb = pl.num_programs(0)
    slot = b % 2

    def page_copy(bb, p, sl):
        return pltpu.make_async_copy(cache_ref.at[pt_ref[bb, p]], kibuf.at[sl, p], sem.at[sl])

    def start_all(bb, sl):
        def body(p, c):
            page_copy(bb, p, sl).start()
            return c
        lax.fori_loop(0, n_pages, body, 0, unroll=ppc)

    def wait_all(bb, sl):
        def body(p, c):
            page_copy(bb, p, sl).wait()
            return c
        lax.fori_loop(0, n_pages, body, 0, unroll=ppc)

    @pl.when(b == 0)
    def _():
        start_all(b, slot)

    @pl.when(b + 1 < nb)
    def _():
        start_all(b + 1, 1 - slot)

    wait_all(b, slot)

    qi = qi_ref[0]
    w = wi_ref[0] * IDX_SCALE
    keys_ref[...] = jnp.full(keys_ref.shape, KEY_INVALID, I32)

    def score_rows(s):
        s = jnp.maximum(s, 0.0) * w
        return jnp.sum(s.reshape(n_q, IDX_HEADS, s.shape[-1]), axis=1)

    def chunk_body(c, carry):
        p0 = pl.multiple_of(c * ppc, ppc)
        kt = jnp.concatenate([kibuf[slot, p0 + p] for p in range(ppc)], axis=1).astype(BF16)
        s = score_rows(jnp.dot(qi, kt, preferred_element_type=F32))
        key = _score_to_key(s, jnp.full(s.shape, True))
        for q in range(n_q):
            for p in range(ppc):
                keys_ref[q, pl.ds(p0 + p, 1), :] = key[q:q + 1, p * LANES:(p + 1) * LANES]
        return carry

    lax.fori_loop(0, n_pages // ppc, chunk_body, 0)
    s_new = score_rows(lax.dot_general(qi, kin_ref[0], NT_DIMS, preferred_element_type=F32))
    qrow = lax.broadcasted_iota(I32, (n_q, LANES), 0)
    jcol = lax.broadcasted_iota(I32, (n_q, LANES), 1)
    key_new = _score_to_key(s_new, jcol <= qrow)
    for q in range(n_q):
        keys_ref[q, n_pages:n_pages + 1, :] = key_new[q:q + 1, :]

    shape = (n_q, 1, LANES)

    def count_ge(cand):
        hit = jnp.where(keys_ref[:, 0:ncs, :] >= cand, 1.0, 0.0)
        part = jnp.sum(hit, axis=1, keepdims=True)
        return jnp.broadcast_to(jnp.sum(part, axis=2, keepdims=True), shape)

    base, cnt = _threshold_search(count_ge, float(n_keep), shape)

    @pl.when(jnp.max(cnt) > float(n_keep))
    def _():
        need = float(n_keep) - count_ge(base + 1)
        ones = jnp.ones((LANES, LANES), BF16)
        for q in range(n_q):
            k = keys_ref[q]
            eq = jnp.logical_and(k == base[q], cnt[q] > float(n_keep))
            eqb = jnp.where(eq, 1.0, 0.0).astype(BF16)
            within = jnp.dot(eqb, tri_ref[...], preferred_element_type=F32)
            rowcnt = jnp.dot(eqb, ones, preferred_element_type=F32).astype(BF16)
            before = jnp.dot(lin_ref[...], rowcnt, preferred_element_type=F32) - rowcnt.astype(F32)
            drop = jnp.logical_and(eq, within + before > need[q])
            keys_ref[q] = jnp.where(drop, KEY_INVALID, k)

    sel_ref[0] = jnp.where(keys_ref[:, 0:ncs, :] >= base, 1.0, 0.0)


def _sample_index_t(page_table, qi_rows, wi_rows, ki_new, tri, cache_ikt, n_keep):
    nb, n_pages = page_table.shape
    rows = qi_rows.shape[1]
    n_q = rows // IDX_HEADS
    ppc = 8 if n_pages % 8 == 0 else 1
    ncs = _ceil_to(n_pages + 1, SUBLANES)
    ncp = _ceil_to(n_pages + 1, LANES)
    assert ncp <= 256, "tie-break prefix counts are carried in bf16, exact up to 256"
    ar = jnp.arange(ncp)
    lin = (ar[None, :] <= ar[:, None]).astype(BF16)
    kern = functools.partial(_sidx_t_kernel, n_keep=n_keep, n_pages=n_pages, n_q=n_q, ppc=ppc,
                             ncs=ncs, ncp=ncp)
    return pl.pallas_call(
        kern,
        grid_spec=pltpu.PrefetchScalarGridSpec(
            num_scalar_prefetch=1,
            grid=(nb,),
            in_specs=[
                pl.BlockSpec((1, rows, IDX_DIM), lambda b, pt: (b, 0, 0)),
                pl.BlockSpec((1, rows, 1), lambda b, pt: (b, 0, 0)),
                pl.BlockSpec((1, LANES, IDX_DIM), lambda b, pt: (b, 0, 0)),
                pl.BlockSpec((LANES, LANES), lambda b, pt: (0, 0)),
                pl.BlockSpec((ncp, ncp), lambda b, pt: (0, 0)),
                pl.BlockSpec(memory_space=pl.ANY),
            ],
            out_specs=pl.BlockSpec((1, n_q, ncs, LANES), lambda b, pt: (b, 0, 0, 0)),
            scratch_shapes=[
                pltpu.VMEM((2, n_pages, IDX_DIM, PAGE_SIZE), F32),
                pltpu.SemaphoreType.DMA((2,)),
                pltpu.VMEM((n_q, ncp, LANES), I32),
            ],
        ),
        out_shape=jax.ShapeDtypeStruct((nb, n_q, ncs, LANES), F32),
        compiler_params=_cparams(("arbitrary",)),
        name="sample_index",
    )(page_table, qi_rows, wi_rows, ki_new, tri, lin, cache_ikt)


def _sdense_kernel(pt_ref, qbd_ref, sel_ref, seln_ref, kn_ref, vn_ref, bias_ref, hsel_ref, ck_ref, cv_ref, o_ref,
                   kbuf, vbuf, sem, bias_buf, m_ref, l_ref, acc_ref, *, n_q, n_pages, ppc):
    b = pl.program_id(0)
    c = pl.program_id(1)
    nb = pl.num_programs(0)
    n_chunks = n_pages // ppc
    step = b * n_chunks + c
    slot = step % 2
    rows = n_q * N_HEADS
    ck = ppc * PAGE_SIZE

    def copies(bb, cc, sl, p):
        page = pt_ref[bb, cc * ppc + p]
        return (pltpu.make_async_copy(ck_ref.at[page], kbuf.at[sl, p], sem.at[0, sl]),
                pltpu.make_async_copy(cv_ref.at[page], vbuf.at[sl, p], sem.at[1, sl]))

    def start_all(bb, cc, sl):
        for p in range(ppc):
            kc, vc = copies(bb, cc, sl, p)
            kc.start()
            vc.start()

    def wait_all(bb, cc, sl):
        for p in range(ppc):
            kc, vc = copies(bb, cc, sl, p)
            kc.wait()
            vc.wait()

    @pl.when(step == 0)
    def _():
        start_all(b, c, slot)

    @pl.when(step + 1 < nb * n_chunks)
    def _():
        nxt = step + 1
        start_all(nxt // n_chunks, nxt % n_chunks, 1 - slot)

    @pl.when(c == 0)
    def _():
        m_ref[...] = jnp.full(m_ref.shape, NEG_BIG, F32)
        l_ref[...] = jnp.zeros(l_ref.shape, F32)
        acc_ref[...] = jnp.zeros(acc_ref.shape, F32)
        bias_buf[...] = jnp.broadcast_to(bias_ref[2][:, 0:1], bias_buf.shape)

    @pl.when(c == n_chunks - 1)
    def _():
        bias_buf[:, ck - PAGE_SIZE:ck] = bias_ref[0]

    wait_all(b, c, slot)
    qbd = qbd_ref[0]

    def expand(sel):
        n = sel.shape[-1]
        return jnp.broadcast_to(sel[:, None, :], (n_q, N_HEADS, n)).reshape(rows, n)

    def update(s, live, pv_fn):
        s = jnp.where(live, s, NEG_BIG)
        m_old = m_ref[...]
        m_new = jnp.maximum(m_old, jnp.max(s, axis=1, keepdims=True))
        p = jnp.exp(s - m_new)
        alpha = jnp.exp(m_old - m_new)
        l_ref[...] = alpha * l_ref[...] + jnp.sum(p, axis=1, keepdims=True)
        acc_ref[...] = alpha * acc_ref[...] + pv_fn(p.astype(BF16))
        m_ref[...] = m_new

    s = jnp.concatenate([jnp.dot(qbd, kbuf[slot, p].astype(BF16), preferred_element_type=F32)
                         for p in range(ppc)], axis=1) + bias_buf[...]
    live = jnp.concatenate([expand(sel_ref[0, :, p, :]) for p in range(ppc)], axis=1) > 0.5

    def pv_pages(pb):
        out = jnp.zeros((rows, D_ATTN), F32)
        for p in range(ppc):
            out = out + lax.dot_general(pb[:, p * PAGE_SIZE:(p + 1) * PAGE_SIZE], vbuf[slot, p].astype(BF16),
                                        NT_DIMS, preferred_element_type=F32)
        return out

    update(s, live, pv_pages)

    @pl.when(c == n_chunks - 1)
    def _():
        s_new = lax.dot_general(qbd, kn_ref[0], NT_DIMS, preferred_element_type=F32) + bias_ref[1]
        update(s_new, expand(seln_ref[0, :, 0, :]) > 0.5,
               lambda pb: jnp.dot(pb, vn_ref[0], preferred_element_type=F32))
        full = acc_ref[...] / l_ref[...] * hsel_ref[...]
        o_ref[0] = jnp.sum(full.reshape(n_q, N_HEADS, D_ATTN), axis=1)


def _sample_attention_dense(page_table, qbd, sel, k_new, v_new, sbias, hsel, ckt, cvt):
    nb, n_pages = page_table.shape
    rows = qbd.shape[1]
    n_q = rows // N_HEADS
    ppc = 16
    assert n_pages % ppc == 0, "cached pages are attended in chunks of 16"
    n_chunks = n_pages // ppc
    ck = ppc * PAGE_SIZE
    kern = functools.partial(_sdense_kernel, n_q=n_q, n_pages=n_pages, ppc=ppc)
    return pl.pallas_call(
        kern,
        grid_spec=pltpu.PrefetchScalarGridSpec(
            num_scalar_prefetch=1,
            grid=(nb, n_chunks),
            in_specs=[
                pl.BlockSpec((1, rows, D_ATTN), lambda b, c, pt: (b, 0, 0)),
                pl.BlockSpec((1, n_q, ppc, LANES), lambda b, c, pt: (b, 0, c, 0)),
                pl.BlockSpec((1, n_q, SUBLANES, LANES), lambda b, c, pt: (b, 0, n_pages // SUBLANES, 0)),
                pl.BlockSpec((1, LANES, D_ATTN), lambda b, c, pt: (b, 0, 0)),
                pl.BlockSpec((1, LANES, D_ATTN), lambda b, c, pt: (b, 0, 0)),
                pl.BlockSpec((3, rows, LANES), lambda b, c, pt: (0, 0, 0)),
                pl.BlockSpec((rows, D_ATTN), lambda b, c, pt: (0, 0)),
                pl.BlockSpec(memory_space=pl.ANY),
                pl.BlockSpec(memory_space=pl.ANY),
            ],
            out_specs=pl.BlockSpec((1, n_q, D_ATTN), lambda b, c, pt: (b, 0, 0)),
            scratch_shapes=[
                pltpu.VMEM((2, ppc, D_ATTN, PAGE_SIZE), F32),
                pltpu.VMEM((2, ppc, D_ATTN, PAGE_SIZE), F32),
                pltpu.SemaphoreType.DMA((2, 2)),
                pltpu.VMEM((rows, ck), F32),
                pltpu.VMEM((rows, 1), F32),
                pltpu.VMEM((rows, 1), F32),
                pltpu.VMEM((rows, D_ATTN), F32),
            ],
        ),
        out_shape=jax.ShapeDtypeStruct((nb, n_q, D_ATTN), F32),
        compiler_params=_cparams(("arbitrary", "arbitrary")),
        name="sample_attention",
    )(page_table, qbd, sel, sel, k_new, v_new, sbias, hsel, ckt, cvt)


def _sample_bias_kernel(rbt_ref, o_ref, *, n_q):
    rows = n_q * N_HEADS
    q = lax.broadcasted_iota(I32, (rows, LANES), 0) // N_HEADS
    col = lax.broadcasted_iota(I32, (rows, LANES), 1)
    rb = rbt_ref[...]
    o_ref[0] = _bias_of(PAGE_SIZE + q - col, rb)
    o_ref[1] = _bias_of(q - col, rb)
    o_ref[2] = _bias_of(jnp.full((rows, LANES), REL_MAX_DIST, I32), rb)


def _sample_bias(rbt_rows, n_q):
    rows = n_q * N_HEADS
    return pl.pallas_call(
        functools.partial(_sample_bias_kernel, n_q=n_q),
        out_shape=jax.ShapeDtypeStruct((3, rows, LANES), F32),
        name="sample_bias",
    )(rbt_rows)


def _merge_kernel(o_ref, ya_ref, ga_ref, gb_ref, x_ref, wb_ref, wo_ref, x1_ref):
    yb = jnp.dot(o_ref[...].astype(BF16), wb_ref[...], preferred_element_type=F32)
    m = _sigmoid(ga_ref[...]) * ya_ref[...] + _sigmoid(gb_ref[...]) * yb
    x1_ref[...] = x_ref[...] + jnp.dot(m.astype(BF16), wo_ref[...], preferred_element_type=F32)


def _merge(o2d, ya, gates, x2d, wb, wo, tm):
    n = x2d.shape[0]
    return pl.pallas_call(
        _merge_kernel,
        grid=(n // tm,),
        in_specs=[
            pl.BlockSpec((tm, D_ATTN), lambda i: (i, 0)),
            pl.BlockSpec((tm, D_MODEL), lambda i: (i, 0)),
            pl.BlockSpec((tm, D_MODEL), lambda i: (i, 0)),
            pl.BlockSpec((tm, D_MODEL), lambda i: (i, 1)),
            pl.BlockSpec((tm, D_MODEL), lambda i: (i, 0)),
            pl.BlockSpec((D_ATTN, D_MODEL), lambda i: (0, 0)),
            pl.BlockSpec((D_MODEL, D_MODEL), lambda i: (0, 0)),
        ],
        out_specs=pl.BlockSpec((tm, D_MODEL), lambda i: (i, 0)),
        out_shape=jax.ShapeDtypeStruct((n, D_MODEL), F32),
        compiler_params=_cparams(("parallel",)),
        name="merge_out_proj",
    )(o2d, ya, gates, gates, x2d, wb, wo)


def _ffn_kernel(x_ref, g_ref, wua_ref, wub_ref, fcw_ref, fcb_ref, past_ref, wd_ref, y_ref, tail_ref,
                xn_ref, carry_ref, ext_ref, acc_ref, *, stride, tm, pad, tail_loc, rs):
    t = pl.program_id(1)
    f = pl.program_id(2)
    nf = pl.num_programs(2)

    @pl.when(f == 0)
    def _():
        x = x_ref[...]
        ms = jnp.mean(x * x, axis=-1, keepdims=True)
        xn_ref[...] = (x * lax.rsqrt(ms + EPS) * g_ref[...]).astype(BF16)
        acc_ref[...] = jnp.zeros(acc_ref.shape, F32)

    @pl.when(t == 0)
    def _():
        ext_ref[0:pad, :] = past_ref[0]

    @pl.when(t > 0)
    def _():
        ext_ref[0:pad, :] = carry_ref[f]

    for r0 in range(0, tm, rs):
        xs = xn_ref[r0:r0 + rs, :]
        a = jnp.dot(xs, wua_ref[...], preferred_element_type=F32)
        bq = jnp.dot(xs, wub_ref[...], preferred_element_type=F32)
        ext_ref[pad + r0:pad + r0 + rs, :] = a
        lo = pad + r0
        conv = (ext_ref[lo - 2 * stride:lo - 2 * stride + rs, :] * fcw_ref[0:1, :]
                + ext_ref[lo - stride:lo - stride + rs, :] * fcw_ref[1:2, :]
                + a * fcw_ref[2:3, :] + fcb_ref[...])
        h = conv * _sigmoid(conv) * bq
        acc_ref[r0:r0 + rs, :] += jnp.dot(h.astype(BF16), wd_ref[...], preferred_element_type=F32)

    carry_ref[f] = ext_ref[tm:tm + pad, :]
    tail_ref[0, 0] = ext_ref[tail_loc:tail_loc + pad, :]

    @pl.when(f == nf - 1)
    def _():
        y_ref[...] = x_ref[...] + acc_ref[...]


def _ffn(x1, g, wup, fcw, fcb, past, wd, *, nb, nt, tm, stride, tf, rows_real):
    pad = past.shape[1]
    nf = D_FF // tf
    tail_tile, tail_loc = _tail_position(rows_real, tm)
    rs = tm // 2 if tm % (2 * SUBLANES) == 0 and tm // 2 >= 2 * stride else tm
    kern = functools.partial(_ffn_kernel, stride=stride, tm=tm, pad=pad, tail_loc=tail_loc, rs=rs)
    y, tail = pl.pallas_call(
        kern,
        grid=(nb, nt, nf),
        in_specs=[
            pl.BlockSpec((tm, D_MODEL), lambda b, t, f: (b * nt + t, 0)),
            pl.BlockSpec((1, D_MODEL), lambda b, t, f: (0, 0)),
            pl.BlockSpec((D_MODEL, tf), lambda b, t, f: (0, f)),
            pl.BlockSpec((D_MODEL, tf), lambda b, t, f: (0, nf + f)),
            pl.BlockSpec((SUBLANES, tf), lambda b, t, f: (0, f)),
            pl.BlockSpec((1, tf), lambda b, t, f: (0, f)),
            pl.BlockSpec((1, pad, tf), lambda b, t, f: (b, 0, f)),
            pl.BlockSpec((tf, D_MODEL), lambda b, t, f: (f, 0)),
        ],
        out_specs=[
            pl.BlockSpec((tm, D_MODEL), lambda b, t, f: (b * nt + t, 0)),
            pl.BlockSpec((1, 1, pad, tf), lambda b, t, f: (b, t, 0, f)),
        ],
        out_shape=[jax.ShapeDtypeStruct((nb * nt * tm, D_MODEL), F32),
                   jax.ShapeDtypeStruct((nb, nt, pad, D_FF), F32)],
        scratch_shapes=[
            pltpu.VMEM((tm, D_MODEL), BF16),
            pltpu.VMEM((nf, pad, tf), F32),
            pltpu.VMEM((pad + tm, tf), F32),
            pltpu.VMEM((tm, D_MODEL), F32),
        ],
        compiler_params=_cparams(("parallel", "arbitrary", "arbitrary")),
        name="conv_ffn",
    )(x1, g, wup, wup, fcw, fcb, past, wd)
    return y, tail[:, tail_tile]


def _row_tile(n, cap):
    best = SUBLANES
    for cand in range(SUBLANES, cap + 1, SUBLANES):
        if n % cand == 0:
            best = cand
    return best


def kernel(x_prompt, x_sample, cache_k, cache_v, cache_idx_k, page_table, state_conv, state_ffn_conv, meta_tokens, g_attn_norm, w_in, conv_w, conv_b, conv_ln_g, conv_ln_b, w_a_out, q_norm_g, k_norm_g, rel_bias, w_b_out, w_o, g_ffn_norm, w_up, ffn_conv_w, ffn_conv_b, w_down):
    nbp, seq, _ = x_prompt.shape
    nbs, n_q, _ = x_sample.shape
    n_pages = page_table.shape[1]
    past_len = n_pages * PAGE_SIZE
    t_real = seq + N_META
    tp = _ceil_to(t_real, LANES)
    keep_p = min(TOP_K_MAX, t_real // 4)
    keep_s = min(TOP_K_MAX, (past_len + n_q) // 4)
    assert g_attn_norm.shape[0] == 1, "single trunk layer"

    w = w_in[0]
    c_q, c_v, c_qi, c_ki, c_wi, c_ga = 1024, 2048, 2560, 3072, 3136, 3144
    wm = jnp.concatenate([w[:, :c_ki], w[:, c_ga:]], axis=1).astype(BF16)
    wt = jnp.stack([w[:, c:c + D_ATTN].T for c in (c_q, c_v, c_qi)]).astype(BF16)
    ws = jnp.concatenate([w[:, c_ki:c_wi], jnp.zeros((D_MODEL, LANES - IDX_DIM), F32)], axis=1).astype(BF16)
    wst = w[:, c_wi:c_ga].T.astype(BF16)
    g_attn = g_attn_norm[0][None, :]
    gq = (jnp.tile(q_norm_g[0], N_HEADS) * ATTN_SCALE)[:, None]
    gk = jnp.tile(k_norm_g[0], N_HEADS)[None, :]
    hid = jnp.arange(D_ATTN) // HEAD_DIM
    seg = jnp.where(hid[:LANES, None] == hid[None, :LANES], 1.0 / HEAD_DIM, 0.0).astype(BF16)
    cw = jnp.concatenate([conv_w[0], jnp.zeros((32 - CONV_W, D_CONV), F32)], axis=0)
    cb, lg, lb = conv_b[0][None, :], conv_ln_g[0][None, :], conv_ln_b[0][None, :]
    wa = w_a_out[0].astype(BF16)
    wb = w_b_out[0].astype(BF16)
    wo = w_o[0].astype(BF16)
    g_ffn = g_ffn_norm[0][None, :]
    wup = w_up[0].astype(BF16)
    fcw = jnp.concatenate([ffn_conv_w[0], jnp.zeros((SUBLANES - FFN_CONV_W, D_FF), F32)], axis=0)
    fcb = ffn_conv_b[0][None, :]
    wd = w_down[0].astype(BF16)
    rbt = rel_bias.T
    ar = jnp.arange(LANES)
    tril = (ar[None, :] <= ar[:, None]).astype(BF16)
    tri = tril.T

    def project(x2d):
        return _in_proj(x2d, g_attn, wm, wt, ws, wst, gq, gk, seg, _row_tile(x2d.shape[0], 1024))

    xp = jnp.concatenate([jnp.broadcast_to(meta_tokens[None], (nbp, N_META, D_MODEL)), x_prompt,
                          jnp.zeros((nbp, tp - t_real, D_MODEL), F32)], axis=1).reshape(nbp * tp, D_MODEL)
    n_p = nbp * tp
    a_p, k_p, kb_p, v_p, qt_p, vt_p, qit_p, g2_p, ki_p, kib_p, wit_p = project(xp)

    tm_p = _row_tile(tp, 640)
    nt_p = tp // tm_p
    ya_p, ctail_p = _conv_branch(a_p, jnp.zeros((nbp, 32, D_CONV), F32), cw, cb, lg, lb, wa,
                                 nb=nbp, nt=nt_p, tm=tm_p, stride=1, rows_real=t_real)
    o_p = _prompt_attention(qt_p, kb_p, vt_p, qit_p, kib_p, wit_p, _prompt_bias(rbt), tril, keep_p, nbp, tp)
    x1_p = _merge(o_p, ya_p, g2_p, xp, wb, wo, _row_tile(n_p, 512))
    y_p, ftail_p = _ffn(x1_p, g_ffn, wup, fcw, fcb, jnp.zeros((nbp, SUBLANES, D_FF), F32), wd,
                        nb=nbp, nt=nt_p, tm=tm_p, stride=1, tf=FFN_TILE, rows_real=t_real)

    y_prompt = y_p.reshape(nbp, tp, D_MODEL)[:, N_META:t_real]
    p_k = k_p.reshape(nbp, tp, N_HEADS, HEAD_DIM)[None, :, :t_real]
    p_v = v_p.reshape(nbp, tp, N_HEADS, HEAD_DIM)[None, :, :t_real]
    p_ik = ki_p.reshape(nbp, tp, IDX_DIM)[None, :, :t_real]
    p_cv = ctail_p[None, :, 32 - (CONV_W - 1):]
    p_fc = ftail_p[None, :, SUBLANES - (FFN_CONV_W - 1):]

    n_s = nbs * n_q
    xs = x_sample.transpose(1, 0, 2).reshape(n_s, D_MODEL)
    a_s, k_s2, _, v_s2, qt_s, _, qit_s, g2_s, ki_s2, _, wit_s = project(xs)

    pad_c = _ceil_to((CONV_W - 1) * nbs, SUBLANES)
    past_c = state_conv[0].transpose(1, 0, 2).reshape(1, (CONV_W - 1) * nbs, D_CONV)
    past_c = jnp.pad(past_c, ((0, 0), (pad_c - (CONV_W - 1) * nbs, 0), (0, 0)))
    ya_s, ctail_s = _conv_branch(a_s, past_c, cw, cb, lg, lb, wa, nb=1, nt=1, tm=n_s, stride=nbs,
                                 rows_real=n_s)

    def batch_major(x2d, *tail):
        return x2d.reshape((n_q, nbs) + tail).transpose((1, 0) + tuple(range(2, 2 + len(tail))))

    q_s = batch_major(qt_s.T.astype(F32), N_HEADS, HEAD_DIM)
    k_s = batch_major(k_s2, N_HEADS, HEAD_DIM)
    v_s = batch_major(v_s2, N_HEADS, HEAD_DIM)
    qi_s = batch_major(qit_s.T, IDX_HEADS, IDX_DIM)
    ki_s = batch_major(ki_s2, IDX_DIM)
    wi_s = batch_major(wit_s.T, IDX_HEADS)
    rows = n_q * N_HEADS

    qi_rows = qi_s.reshape(nbs, rows, IDX_DIM)
    wi_rows = wi_s.reshape(nbs, rows, 1)
    ki_new = jnp.pad(ki_s, ((0, 0), (0, LANES - n_q), (0, 0))).astype(BF16)
    n_pool = cache_k.shape[1]
    ckt = cache_k.transpose(0, 1, 3, 4, 2).reshape(n_pool, D_ATTN, PAGE_SIZE)
    cvt = cache_v.transpose(0, 1, 3, 4, 2).reshape(n_pool, D_ATTN, PAGE_SIZE)
    cikt = cache_idx_k.transpose(0, 1, 3, 2).reshape(n_pool, IDX_DIM, PAGE_SIZE)
    sel = _sample_index_t(page_table, qi_rows, wi_rows, ki_new, tri, cikt, keep_s)

    eye = jnp.eye(N_HEADS, dtype=F32)
    qbd = (q_s[:, :, :, None, :] * eye[None, None, :, :, None]).reshape(nbs, rows, D_ATTN).astype(BF16)
    k_new = jnp.pad(k_s.reshape(nbs, n_q, D_ATTN), ((0, 0), (0, LANES - n_q), (0, 0))).astype(BF16)
    v_new = jnp.pad(v_s.reshape(nbs, n_q, D_ATTN), ((0, 0), (0, LANES - n_q), (0, 0))).astype(BF16)
    hsel = (jnp.arange(rows)[:, None] % N_HEADS == hid[None, :]).astype(F32)
    o_s = _sample_attention_dense(page_table, qbd, sel, k_new, v_new, _sample_bias(jnp.tile(rbt, (n_q, 1)), n_q),
                                  hsel, ckt, cvt)
    o_s2d = o_s.transpose(1, 0, 2).reshape(n_s, D_ATTN)

    x1_s = _merge(o_s2d, ya_s, g2_s, xs, wb, wo, _row_tile(n_s, 512))
    pad_f = _ceil_to((FFN_CONV_W - 1) * nbs, SUBLANES)
    past_f = state_ffn_conv[0].transpose(1, 0, 2).reshape(1, (FFN_CONV_W - 1) * nbs, D_FF)
    past_f = jnp.pad(past_f, ((0, 0), (pad_f - (FFN_CONV_W - 1) * nbs, 0), (0, 0)))
    y_s, ftail_s = _ffn(x1_s, g_ffn, wup, fcw, fcb, past_f, wd, nb=1, nt=1, tm=n_s, stride=nbs, tf=FFN_TILE,
                        rows_real=n_s)

    y_sample = y_s.reshape(n_q, nbs, D_MODEL).transpose(1, 0, 2)
    s_cv = ctail_s[0, pad_c - (CONV_W - 1) * nbs:].reshape(CONV_W - 1, nbs, D_CONV).transpose(1, 0, 2)[None]
    s_fc = ftail_s[0, pad_f - (FFN_CONV_W - 1) * nbs:].reshape(FFN_CONV_W - 1, nbs, D_FF).transpose(1, 0, 2)[None]

    return (y_prompt, y_sample, p_k, p_v, p_ik, p_cv, p_fc,
            k_s[None], v_s[None], ki_s[None], s_cv, s_fc)
```

```python
import functools
import math

import jax
import jax.numpy as jnp
from jax import lax
from jax.experimental import pallas as pl
from jax.experimental.pallas import tpu as pltpu

F32 = jnp.float32
BF16 = jnp.bfloat16
I32 = jnp.int32

D_MODEL = 1024
D_CONV = D_MODEL // 2
CONV_W = 31
N_HEADS = 8
HEAD_DIM = 64
D_ATTN = N_HEADS * HEAD_DIM
IDX_HEADS = 8
IDX_DIM = 64
TOP_K_MAX = 256
N_BUCKETS = 32
MAX_EXACT = N_BUCKETS // 2
REL_MAX_DIST = 128
D_FF = 2816
FFN_CONV_W = 3
N_META = 16
PAGE_SIZE = 128
FFN_TILE = D_FF // 2
SEARCH_HEAD_BITS = 3
SEARCH_GROUP = 4
EPS = 1e-6
ATTN_SCALE = HEAD_DIM ** -0.5
IDX_SCALE = (IDX_HEADS * IDX_DIM) ** -0.5
D_MAIN = 2 * D_CONV + 4 * D_ATTN + 2 * D_MODEL
JB_Q, JB_K, JB_V, JB_QI, JB_G = 2, 3, 4, 5, 6

LANES = 128
SUBLANES = 8
KEY_INVALID = -2 ** 31
NEG_BIG = -1e30
VMEM_LIMIT = 56 * 1024 * 1024
NT_DIMS = (((1,), (1,)), ((), ()))


def _cparams(sem):
    return pltpu.CompilerParams(dimension_semantics=sem, vmem_limit_bytes=VMEM_LIMIT)


def _sigmoid(x):
    return 1.0 / (1.0 + jnp.exp(-x))


def _ceil_to(x, m):
    return -(-x // m) * m


def _split_bf16(x):
    hi = x.astype(BF16)
    return hi, (x - hi.astype(F32)).astype(BF16)


def _in_proj_kernel(x_ref, g_ref, wm_ref, wt_ref, ws_ref, wst_ref, gq_ref, gk_ref, seg_ref,
                    a_ref, k_ref, kb_ref, v_ref, qt_ref, vt_ref, qit_ref, g2_ref, ki_ref, kib_ref, wit_ref,
                    xn_ref):
    j = pl.program_id(1)

    @pl.when(j == 0)
    def _():
        x = x_ref[...]
        ms = jnp.mean(x * x, axis=-1, keepdims=True)
        xn_ref[...] = (x * lax.rsqrt(ms + EPS) * g_ref[...]).astype(BF16)
        ki = jnp.dot(xn_ref[...], ws_ref[...], preferred_element_type=F32)[:, :IDX_DIM]
        ki_ref[...] = ki
        kib_ref[...] = ki.astype(BF16)
        wit_ref[...] = lax.dot_general(wst_ref[...], xn_ref[...], NT_DIMS, preferred_element_type=F32)

    @pl.when(j < JB_Q)
    def _():
        a_ref[...] = jnp.dot(xn_ref[...], wm_ref[...], preferred_element_type=F32)

    @pl.when(j >= JB_G)
    def _():
        g2_ref[...] = jnp.dot(xn_ref[...], wm_ref[...], preferred_element_type=F32).astype(BF16)

    @pl.when(j == JB_Q)
    def _():
        yt = lax.dot_general(wt_ref[0], xn_ref[...], NT_DIMS, preferred_element_type=F32)
        hi, lo = _split_bf16(yt * yt)
        seg = seg_ref[...]
        ms = jnp.concatenate(
            [jnp.dot(seg, hi[r:r + LANES], preferred_element_type=F32)
             + jnp.dot(seg, lo[r:r + LANES], preferred_element_type=F32) for r in range(0, D_ATTN, LANES)],
            axis=0)
        qt_ref[...] = (yt * lax.rsqrt(ms + EPS) * gq_ref[...]).astype(BF16)

    @pl.when(j == JB_K)
    def _():
        y = jnp.dot(xn_ref[...], wm_ref[...], preferred_element_type=F32)
        hi, lo = _split_bf16(y * y)
        seg = seg_ref[...]
        ms = jnp.concatenate(
            [jnp.dot(hi[:, c:c + LANES], seg, preferred_element_type=F32)
             + jnp.dot(lo[:, c:c + LANES], seg, preferred_element_type=F32) for c in range(0, D_ATTN, LANES)],
            axis=1)
        k = y * lax.rsqrt(ms + EPS) * gk_ref[...]
        k_ref[...] = k
        kb_ref[...] = k.astype(BF16)

    @pl.when(j == JB_V)
    def _():
        v_ref[...] = jnp.dot(xn_ref[...], wm_ref[...], preferred_element_type=F32)
        vt_ref[...] = lax.dot_general(wt_ref[1], xn_ref[...], NT_DIMS, preferred_element_type=F32).astype(BF16)

    @pl.when(j == JB_QI)
    def _():
        qit_ref[...] = lax.dot_general(wt_ref[2], xn_ref[...], NT_DIMS, preferred_element_type=F32).astype(BF16)


def _in_proj(x2d, g, wm, wt, ws, wst, gq, gk, seg, tm):
    n = x2d.shape[0]
    tn = D_ATTN
    nj = D_MAIN // tn
    row = lambda i, j: (i, 0)
    colt = lambda i, j: (0, i)
    const2 = lambda i, j: (0, 0)
    out_shape = [
        jax.ShapeDtypeStruct((n, 2 * D_CONV), F32),
        jax.ShapeDtypeStruct((n, D_ATTN), F32),
        jax.ShapeDtypeStruct((n, D_ATTN), BF16),
        jax.ShapeDtypeStruct((n, D_ATTN), F32),
        jax.ShapeDtypeStruct((D_ATTN, n), BF16),
        jax.ShapeDtypeStruct((D_ATTN, n), BF16),
        jax.ShapeDtypeStruct((D_ATTN, n), BF16),
        jax.ShapeDtypeStruct((n, 2 * D_MODEL), BF16),
        jax.ShapeDtypeStruct((n, IDX_DIM), F32),
        jax.ShapeDtypeStruct((n, IDX_DIM), BF16),
        jax.ShapeDtypeStruct((IDX_HEADS, n), F32),
    ]
    out_specs = [
        pl.BlockSpec((tm, tn), lambda i, j: (i, jnp.minimum(j, 1))),
        pl.BlockSpec((tm, tn), row),
        pl.BlockSpec((tm, tn), row),
        pl.BlockSpec((tm, tn), row),
        pl.BlockSpec((tn, tm), colt),
        pl.BlockSpec((tn, tm), colt),
        pl.BlockSpec((tn, tm), colt),
        pl.BlockSpec((tm, tn), lambda i, j: (i, jnp.clip(j - JB_G, 0, nj - JB_G - 1))),
        pl.BlockSpec((tm, IDX_DIM), row),
        pl.BlockSpec((tm, IDX_DIM), row),
        pl.BlockSpec((IDX_HEADS, tm), colt),
    ]
    return pl.pallas_call(
        _in_proj_kernel,
        grid=(n // tm, nj),
        in_specs=[
            pl.BlockSpec((tm, D_MODEL), row),
            pl.BlockSpec((1, D_MODEL), const2),
            pl.BlockSpec((D_MODEL, tn), lambda i, j: (0, j)),
            pl.BlockSpec((3, tn, D_MODEL), lambda i, j: (0, 0, 0)),
            pl.BlockSpec((D_MODEL, LANES), const2),
            pl.BlockSpec((IDX_HEADS, D_MODEL), const2),
            pl.BlockSpec((D_ATTN, 1), const2),
            pl.BlockSpec((1, D_ATTN), const2),
            pl.BlockSpec((LANES, LANES), const2),
        ],
        out_specs=out_specs,
        out_shape=out_shape,
        scratch_shapes=[pltpu.VMEM((tm, D_MODEL), BF16)],
        compiler_params=_cparams(("parallel", "arbitrary")),
        name="in_proj",
    )(x2d, g, wm, wt, ws, wst, gq, gk, seg)


def _conv_kernel(a_ref, past_ref, cw_ref, cb_ref, lg_ref, lb_ref, wa_ref, ya_ref, tail_ref, ext_ref, h_ref,
                 *, stride, tm, pad, rc, tail_tile, tail_loc):
    t = pl.program_id(1)

    @pl.when(t == 0)
    def _():
        ext_ref[0, 0:pad, :] = past_ref[0]

    @pl.when(t > 0)
    def _():
        ext_ref[0, 0:pad, :] = ext_ref[0, tm:tm + pad, :]

    a = a_ref[...]
    ext_ref[0, pad:pad + tm, :] = a[:, :D_CONV] * _sigmoid(a[:, D_CONV:])

    @pl.when(t <= tail_tile)
    def _():
        tail_ref[0] = ext_ref[0, tail_loc:tail_loc + pad, :]

    n_shift = ext_ref.shape[0]
    length = pad + tm
    for r in range(1, n_shift):
        ext_ref[r, 0:length - SUBLANES, :] = ext_ref[0, r:r + length - SUBLANES, :]

    cb = cb_ref[...]
    lg = lg_ref[...]
    lb = lb_ref[...]
    for r0 in range(0, tm, rc):
        acc = jnp.zeros((rc, D_CONV), F32) + cb
        for w in range(CONV_W):
            off = pad - (CONV_W - 1 - w) * stride + r0
            r = off % n_shift
            acc = acc + ext_ref[r, off - r:off - r + rc, :] * cw_ref[w:w + 1, :]
        mu = jnp.mean(acc, axis=-1, keepdims=True)
        d = acc - mu
        var = jnp.mean(d * d, axis=-1, keepdims=True)
        h = d * lax.rsqrt(var + EPS) * lg + lb
        h_ref[r0:r0 + rc, :] = (h * _sigmoid(h)).astype(BF16)
    ya_ref[...] = jnp.dot(h_ref[...], wa_ref[...], preferred_element_type=F32).astype(BF16)


def _tail_position(rows_real, tm):
    tail_tile = (rows_real - 1) // tm
    return tail_tile, rows_real - tail_tile * tm


def _conv_branch(a_in, past, cw, cb, lg, lb, wa, *, nb, nt, tm, stride, rows_real):
    pad = past.shape[1]
    rc = 32 if tm % 32 == 0 else SUBLANES
    tail_tile, tail_loc = _tail_position(rows_real, tm)
    kern = functools.partial(_conv_kernel, stride=stride, tm=tm, pad=pad, rc=rc,
                             tail_tile=tail_tile, tail_loc=tail_loc)
    return pl.pallas_call(
        kern,
        grid=(nb, nt),
        in_specs=[
            pl.BlockSpec((tm, 2 * D_CONV), lambda b, t: (b * nt + t, 0)),
            pl.BlockSpec((1, pad, D_CONV), lambda b, t: (b, 0, 0)),
            pl.BlockSpec((32, D_CONV), lambda b, t: (0, 0)),
            pl.BlockSpec((1, D_CONV), lambda b, t: (0, 0)),
            pl.BlockSpec((1, D_CONV), lambda b, t: (0, 0)),
            pl.BlockSpec((1, D_CONV), lambda b, t: (0, 0)),
            pl.BlockSpec((D_CONV, D_MODEL), lambda b, t: (0, 0)),
        ],
        out_specs=[
            pl.BlockSpec((tm, D_MODEL), lambda b, t: (b * nt + t, 0)),
            pl.BlockSpec((1, pad, D_CONV), lambda b, t: (b, 0, 0)),
        ],
        out_shape=[jax.ShapeDtypeStruct((nb * nt * tm, D_MODEL), BF16),
                   jax.ShapeDtypeStruct((nb, pad, D_CONV), F32)],
        scratch_shapes=[pltpu.VMEM((1 if stride % SUBLANES == 0 else SUBLANES, pad + tm, D_CONV), F32),
                        pltpu.VMEM((tm, D_CONV), BF16)],
        compiler_params=_cparams(("parallel", "arbitrary")),
        name="conv_branch",
    )(a_in, past, cw, cb, lg, lb, wa)


def _rel_bucket(rel):
    n = jnp.maximum(rel, 0)
    nf = jnp.maximum(n, 1).astype(F32)
    large = MAX_EXACT + (jnp.log(nf / MAX_EXACT) / math.log(REL_MAX_DIST / MAX_EXACT)
                         * (N_BUCKETS - MAX_EXACT)).astype(I32)
    large = jnp.minimum(large, N_BUCKETS - 1)
    return jnp.where(n < MAX_EXACT, n, large)


def _bias_of(rel, rb_rows):
    bucket = _rel_bucket(rel)
    out = jnp.zeros(rel.shape, F32)
    for b in range(N_BUCKETS):
        out = jnp.where(bucket == b, rb_rows[:, b:b + 1], out)
    return out


def _prompt_bias_kernel(rbt_ref, b1_ref):
    key = lax.broadcasted_iota(I32, (LANES, LANES), 0)
    qry = lax.broadcasted_iota(I32, (LANES, LANES), 1)
    for h in range(N_HEADS):
        for d in range(3):
            b1_ref[h * 3 + d] = _bias_of(d * LANES + qry - key, rbt_ref[h:h + 1, :])


def _prompt_bias(rbt):
    return pl.pallas_call(
        _prompt_bias_kernel,
        out_shape=jax.ShapeDtypeStruct((N_HEADS * 3, LANES, LANES), F32),
        name="prompt_bias",
    )(rbt)


def _score_to_key(s, valid):
    bits = pltpu.bitcast(s, I32)
    key = bits ^ ((bits >> 31) & 0x7FFFFFFF)
    key = jnp.where(bits == KEY_INVALID, 0, key)
    return jnp.where(valid, key, KEY_INVALID)


def _threshold_search(count_ge, n_keep, shape):
    cnt_valid = count_ge(jnp.full(shape, KEY_INVALID + 1, I32))
    c0 = count_ge(jnp.zeros(shape, I32))
    ok0 = c0 >= n_keep
    base = jnp.where(ok0, 0, KEY_INVALID).astype(I32)
    cnt = jnp.where(ok0, c0, cnt_valid)

    def step(t, base, cnt):
        cand = base | lax.shift_left(jnp.int32(1), 30 - t)
        c = count_ge(cand)
        ok = c >= n_keep
        return jnp.where(ok, cand, base), jnp.where(ok, c, cnt)

    for t in range(SEARCH_HEAD_BITS):
        base, cnt = step(t, base, cnt)

    def cond(state):
        t, _, _, unsettled = state
        return jnp.logical_and(t < 31, unsettled)

    def body(state):
        t, base, cnt, _ = state
        for u in range(SEARCH_GROUP):
            base, cnt = step(t + u, base, cnt)
        return t + SEARCH_GROUP, base, cnt, jnp.max(cnt) > n_keep

    _, base, cnt, _ = lax.while_loop(cond, body, (jnp.int32(SEARCH_HEAD_BITS), base, cnt, jnp.max(cnt) > n_keep))
    return jnp.maximum(base, KEY_INVALID + 1), cnt


def _pattn_kernel(qt_ref, k_ref, vt_ref, qit_ref, ki_ref, wit_ref, b1_ref, tril_ref, o_ref,
                  keys_ref, qiw_ref, qbd_ref, m_ref, l_ref, acc_ref, ot_ref, p_ref, alpha_ref, *, n_keep):
    i = pl.program_id(1)
    tq = LANES
    row1 = (1, tq)
    lead = (i + 1) % 2
    npair = (i + 1) // 2
    qpos = i * tq + lax.broadcasted_iota(I32, row1, 1)

    def for_chunks(fn):
        @pl.when(lead == 1)
        def _():
            fn(0, LANES)

        def body(j, carry):
            fn(pl.multiple_of(lead * LANES + j * 2 * LANES, LANES), 2 * LANES)
            return carry
        lax.fori_loop(0, npair, body, 0)

    for h in range(IDX_HEADS):
        qiw_ref[:, h * tq:(h + 1) * tq] = qit_ref[h * IDX_DIM:(h + 1) * IDX_DIM, :]
    qbd_ref[...] = jnp.zeros(qbd_ref.shape, BF16)
    for h in range(N_HEADS):
        hp, e = divmod(h, 2)
        qbd_ref[hp, e * HEAD_DIM:(e + 1) * HEAD_DIM, e * tq:(e + 1) * tq] = qt_ref[h * HEAD_DIM:(h + 1) * HEAD_DIM, :]
    wit = wit_ref[...] * IDX_SCALE

    def idx_chunk(off, ck):
        sall = jnp.dot(ki_ref[pl.ds(off, ck), :], qiw_ref[...], preferred_element_type=F32)
        s = jnp.zeros((ck, tq), F32)
        for h in range(IDX_HEADS):
            s = s + wit[h:h + 1, :] * jnp.maximum(sall[:, h * tq:(h + 1) * tq], 0.0)
        kpos = off + lax.broadcasted_iota(I32, (ck, tq), 0)
        keys_ref[pl.ds(off, ck), :] = _score_to_key(s, kpos <= qpos)

    for_chunks(idx_chunk)

    blk = (LANES, tq)
    nblk = i + 1

    def count_ge(cand):
        cb = jnp.broadcast_to(cand, blk)

        def hit(off):
            return jnp.where(keys_ref[pl.ds(off, LANES), :] >= cb, 1.0, 0.0)

        def body(j, acc):
            off = pl.multiple_of(lead * LANES + j * 2 * LANES, LANES)
            return acc + hit(off) + hit(off + LANES)
        acc = lax.fori_loop(0, npair, body, jnp.where(lead == 1, hit(0), 0.0))
        return jnp.sum(acc, axis=0, keepdims=True)

    base, cnt = _threshold_search(count_ge, float(n_keep), row1)

    @pl.when(jnp.max(cnt) > float(n_keep))
    def _():
        need = float(n_keep) - count_ge(base + 1)
        tie_q = cnt > float(n_keep)

        def body(j, seen):
            off = pl.multiple_of(j * LANES, LANES)
            k = keys_ref[pl.ds(off, LANES), :]
            eq = jnp.logical_and(k == base, tie_q)
            eqf = jnp.where(eq, 1.0, 0.0)
            pref = seen + jnp.dot(tril_ref[...], eqf.astype(BF16), preferred_element_type=F32)
            keys_ref[pl.ds(off, LANES), :] = jnp.where(jnp.logical_and(eq, pref > need), KEY_INVALID, k)
            return seen + jnp.sum(eqf, axis=0, keepdims=True)
        lax.fori_loop(0, nblk, body, jnp.zeros(row1, F32))

    m_ref[...] = jnp.full(m_ref.shape, NEG_BIG, F32)
    l_ref[...] = jnp.zeros(l_ref.shape, F32)
    acc_ref[...] = jnp.zeros(acc_ref.shape, F32)
    p_ref[...] = jnp.zeros(p_ref.shape, BF16)
    alpha_ref[...] = jnp.ones(alpha_ref.shape, F32)
    base_b = jnp.broadcast_to(base, blk)

    def apply_pv(jb):
        off = pl.multiple_of(jb * LANES, LANES)
        for h in range(N_HEADS):
            pv = jnp.dot(vt_ref[h * HEAD_DIM:(h + 1) * HEAD_DIM, pl.ds(off, LANES)], p_ref[h],
                         preferred_element_type=F32)
            acc_ref[h] = alpha_ref[h] * acc_ref[h] + pv

    def att_body(j, carry):
        apply_pv(jnp.maximum(j - 1, 0))
        off = pl.multiple_of(j * LANES, LANES)
        sel = keys_ref[pl.ds(off, LANES), :] >= base_b
        d = jnp.minimum(i - j, 2)
        for hp in range(N_HEADS // 2):
            for e in range(2):
                h = 2 * hp + e
                s1 = jnp.dot(k_ref[pl.ds(off, LANES), hp * LANES:(hp + 1) * LANES],
                             qbd_ref[hp, :, e * tq:(e + 1) * tq], preferred_element_type=F32)
                s = jnp.where(sel, s1 + b1_ref[h * 3 + d], NEG_BIG)
                m_old = m_ref[h]
                m_new = jnp.maximum(m_old, jnp.max(s, axis=0, keepdims=True))
                p = jnp.exp(s - m_new)
                alpha = jnp.exp(m_old - m_new)
                l_ref[h] = alpha * l_ref[h] + jnp.sum(p, axis=0, keepdims=True)
                p_ref[h] = p.astype(BF16)
                alpha_ref[h] = alpha
                m_ref[h] = m_new
        return carry

    lax.fori_loop(0, nblk, att_body, 0)
    apply_pv(nblk - 1)
    for h in range(N_HEADS):
        ot_ref[h * HEAD_DIM:(h + 1) * HEAD_DIM, :] = acc_ref[h] / l_ref[h]
    o_ref[...] = ot_ref[...].T


def _prompt_attention(qt, kb, vt, qit, kib, wit, b1, tril, n_keep, nb, tp):
    tq = LANES
    nq = tp // tq
    n = nb * tp
    qcol = lambda b, i: (0, b * nq + i)
    return pl.pallas_call(
        functools.partial(_pattn_kernel, n_keep=n_keep),
        grid=(nb, nq),
        in_specs=[
            pl.BlockSpec((D_ATTN, tq), qcol),
            pl.BlockSpec((tp, D_ATTN), lambda b, i: (b, 0)),
            pl.BlockSpec((D_ATTN, tp), lambda b, i: (0, b)),
            pl.BlockSpec((IDX_HEADS * IDX_DIM, tq), qcol),
            pl.BlockSpec((tp, IDX_DIM), lambda b, i: (b, 0)),
            pl.BlockSpec((IDX_HEADS, tq), qcol),
            pl.BlockSpec((N_HEADS * 3, LANES, LANES), lambda b, i: (0, 0, 0)),
            pl.BlockSpec((LANES, LANES), lambda b, i: (0, 0)),
        ],
        out_specs=pl.BlockSpec((tq, D_ATTN), lambda b, i: (b * nq + i, 0)),
        out_shape=jax.ShapeDtypeStruct((n, D_ATTN), F32),
        scratch_shapes=[
            pltpu.VMEM((tp, tq), I32),
            pltpu.VMEM((IDX_DIM, IDX_HEADS * tq), BF16),
            pltpu.VMEM((N_HEADS // 2, LANES, 2 * tq), BF16),
            pltpu.VMEM((N_HEADS, 1, tq), F32),
            pltpu.VMEM((N_HEADS, 1, tq), F32),
            pltpu.VMEM((N_HEADS, HEAD_DIM, tq), F32),
            pltpu.VMEM((D_ATTN, tq), F32),
            pltpu.VMEM((N_HEADS, LANES, tq), BF16),
            pltpu.VMEM((N_HEADS, 1, tq), F32),
        ],
        compiler_params=_cparams(("parallel", "arbitrary")),
        name="prompt_attention",
    )(qt, kb, vt, qit, kib, wit, b1, tril)


def _sidx_t_kernel(pt_ref, qi_ref, wi_ref, kin_ref, tri_ref, lin_ref, cache_ref, sel_ref,
                   kibuf, sem, keys_ref, *, n_keep, n_pages, n_q, ppc, ncs, ncp, gb):
    b = pl.program_id(0)
    nb = pl.num_programs(0)
    slot = b % 2

    def page_copy(step, g, p, sl):
        return pltpu.make_async_copy(cache_ref.at[pt_ref[step * gb + g, p]], kibuf.at[sl, g * n_pages + p],
                                     sem.at[sl])

    def start_all(step, sl):
        for g in range(gb):
            def body(p, c):
                page_copy(step, g, p, sl).start()
                return c
            lax.fori_loop(0, n_pages, body, 0, unroll=ppc)

    def wait_all(step, sl):
        for g in range(gb):
            def body(p, c):
                page_copy(step, g, p, sl).wait()
                return c
            lax.fori_loop(0, n_pages, body, 0, unroll=ppc)

    @pl.when(b == 0)
    def _():
        start_all(b, slot)

    @pl.when(b + 1 < nb)
    def _():
        start_all(b + 1, 1 - slot)

    wait_all(b, slot)
    keys_ref[...] = jnp.full(keys_ref.shape, KEY_INVALID, I32)
    qrow = lax.broadcasted_iota(I32, (n_q, LANES), 0)
    jcol = lax.broadcasted_iota(I32, (n_q, LANES), 1)

    for g in range(gb):
        qi = qi_ref[g]
        w = wi_ref[g] * IDX_SCALE

        def score_rows(s):
            s = jnp.maximum(s, 0.0) * w
            return jnp.sum(s.reshape(n_q, IDX_HEADS, s.shape[-1]), axis=1)

        def chunk_body(c, carry):
            p0 = pl.multiple_of(c * ppc, ppc)
            kt = jnp.concatenate([kibuf[slot, g * n_pages + p0 + p] for p in range(ppc)],
                                 axis=1).astype(BF16)
            s = score_rows(jnp.dot(qi, kt, preferred_element_type=F32))
            key = _score_to_key(s, jnp.full(s.shape, True))
            for q in range(n_q):
                for p in range(ppc):
                    keys_ref[g * n_q + q, pl.ds(p0 + p, 1), :] = key[q:q + 1, p * LANES:(p + 1) * LANES]
            return carry

        lax.fori_loop(0, n_pages // ppc, chunk_body, 0)
        s_new = score_rows(lax.dot_general(qi, kin_ref[g], NT_DIMS, preferred_element_type=F32))
        key_new = _score_to_key(s_new, jcol <= qrow)
        for q in range(n_q):
            keys_ref[g * n_q + q, n_pages:n_pages + 1, :] = key_new[q:q + 1, :]

    shape = (gb * n_q, 1, LANES)

    def count_ge(cand):
        hit = jnp.where(keys_ref[:, 0:ncs, :] >= cand, 1.0, 0.0)
        part = jnp.sum(hit, axis=1, keepdims=True)
        return jnp.broadcast_to(jnp.sum(part, axis=2, keepdims=True), shape)

    base, cnt = _threshold_search(count_ge, float(n_keep), shape)

    @pl.when(jnp.max(cnt) > float(n_keep))
    def _():
        need = float(n_keep) - count_ge(base + 1)
        ones = jnp.ones((LANES, LANES), BF16)
        for q in range(gb * n_q):
            k = keys_ref[q]
            eq = jnp.logical_and(k == base[q], cnt[q] > float(n_keep))
            eqb = jnp.where(eq, 1.0, 0.0).astype(BF16)
            within = jnp.dot(eqb, tri_ref[...], preferred_element_type=F32)
            rowcnt = jnp.dot(eqb, ones, preferred_element_type=F32).astype(BF16)
            before = jnp.dot(lin_ref[...], rowcnt, preferred_element_type=F32) - rowcnt.astype(F32)
            drop = jnp.logical_and(eq, within + before > need[q])
            keys_ref[q] = jnp.where(drop, KEY_INVALID, k)

    sel = jnp.where(keys_ref[:, 0:ncs, :] >= base, 1.0, 0.0)
    for g in range(gb):
        sel_ref[g] = sel[g * n_q:(g + 1) * n_q]


def _sample_index_t(page_table, qi_rows, wi_rows, ki_new, tri, cache_ikt, n_keep):
    nb_all, n_pages = page_table.shape
    gb = 2 if nb_all % 2 == 0 else 1
    nb = nb_all // gb
    rows = qi_rows.shape[1]
    n_q = rows // IDX_HEADS
    ppc = 8 if n_pages % 8 == 0 else 1
    ncs = _ceil_to(n_pages + 1, SUBLANES)
    ncp = _ceil_to(n_pages + 1, LANES)
    assert ncp <= 256, "tie-break prefix counts are carried in bf16, exact up to 256"
    ar = jnp.arange(ncp)
    lin = (ar[None, :] <= ar[:, None]).astype(BF16)
    kern = functools.partial(_sidx_t_kernel, n_keep=n_keep, n_pages=n_pages, n_q=n_q, ppc=ppc,
                             ncs=ncs, ncp=ncp, gb=gb)
    return pl.pallas_call(
        kern,
        grid_spec=pltpu.PrefetchScalarGridSpec(
            num_scalar_prefetch=1,
            grid=(nb,),
            in_specs=[
                pl.BlockSpec((gb, rows, IDX_DIM), lambda b, pt: (b, 0, 0)),
                pl.BlockSpec((gb, rows, 1), lambda b, pt: (b, 0, 0)),
                pl.BlockSpec((gb, LANES, IDX_DIM), lambda b, pt: (b, 0, 0)),
                pl.BlockSpec((LANES, LANES), lambda b, pt: (0, 0)),
                pl.BlockSpec((ncp, ncp), lambda b, pt: (0, 0)),
                pl.BlockSpec(memory_space=pl.ANY),
            ],
            out_specs=pl.BlockSpec((gb, n_q, ncs, LANES), lambda b, pt: (b, 0, 0, 0)),
            scratch_shapes=[
                pltpu.VMEM((2, gb * n_pages, IDX_DIM, PAGE_SIZE), F32),
                pltpu.SemaphoreType.DMA((2,)),
                pltpu.VMEM((gb * n_q, ncp, LANES), I32),
            ],
        ),
        out_shape=jax.ShapeDtypeStruct((nb_all, n_q, ncs, LANES), F32),
        compiler_params=_cparams(("arbitrary",)),
        name="sample_index",
    )(page_table, qi_rows, wi_rows, ki_new, tri, lin, cache_ikt)


def _sdense_kernel(pt_ref, qbd_ref, sel_ref, seln_ref, kn_ref, vn_ref, bias_ref, hsel_ref, ck_ref, cv_ref, o_ref,
                   kbuf, vbuf, sem, bias_buf, m_ref, l_ref, acc_ref, *, n_q, n_pages, ppc):
    b = pl.program_id(0)
    c = pl.program_id(1)
    nb = pl.num_programs(0)
    n_chunks = n_pages // ppc
    step = b * n_chunks + c
    slot = step % 2
    rows = n_q * N_HEADS
    ck = ppc * PAGE_SIZE

    def copies(bb, cc, sl, p):
        page = pt_ref[bb, cc * ppc + p]
        return (pltpu.make_async_copy(ck_ref.at[page], kbuf.at[sl, p], sem.at[0, sl]),
                pltpu.make_async_copy(cv_ref.at[page], vbuf.at[sl, p], sem.at[1, sl]))

    def start_all(bb, cc, sl):
        for p in range(ppc):
            kc, vc = copies(bb, cc, sl, p)
            kc.start()
            vc.start()

    def wait_all(bb, cc, sl):
        for p in range(ppc):
            kc, vc = copies(bb, cc, sl, p)
            kc.wait()
            vc.wait()

    @pl.when(step == 0)
    def _():
        start_all(b, c, slot)

    @pl.when(step + 1 < nb * n_chunks)
    def _():
        nxt = step + 1
        start_all(nxt // n_chunks, nxt % n_chunks, 1 - slot)

    @pl.when(c == 0)
    def _():
        m_ref[...] = jnp.full(m_ref.shape, NEG_BIG, F32)
        l_ref[...] = jnp.zeros(l_ref.shape, F32)
        acc_ref[...] = jnp.zeros(acc_ref.shape, F32)
        bias_buf[...] = jnp.broadcast_to(bias_ref[2][:, 0:1], bias_buf.shape)

    @pl.when(c == n_chunks - 1)
    def _():
        bias_buf[:, ck - PAGE_SIZE:ck] = bias_ref[0]

    wait_all(b, c, slot)
    qbd = qbd_ref[0]

    def expand(sel):
        n = sel.shape[-1]
        return jnp.broadcast_to(sel[:, None, :], (n_q, N_HEADS, n)).reshape(rows, n)

    def update(s, live, pv_fn):
        s = jnp.where(live, s, NEG_BIG)
        m_old = m_ref[...]
        m_new = jnp.maximum(m_old, jnp.max(s, axis=1, keepdims=True))
        p = jnp.exp(s - m_new)
        alpha = jnp.exp(m_old - m_new)
        l_ref[...] = alpha * l_ref[...] + jnp.sum(p, axis=1, keepdims=True)
        acc_ref[...] = alpha * acc_ref[...] + pv_fn(p.astype(BF16))
        m_ref[...] = m_new

    s = jnp.concatenate([jnp.dot(qbd, kbuf[slot, p].astype(BF16), preferred_element_type=F32)
                         for p in range(ppc)], axis=1) + bias_buf[...]
    live = jnp.concatenate([expand(sel_ref[0, :, p, :]) for p in range(ppc)], axis=1) > 0.5

    def pv_pages(pb):
        out = jnp.zeros((rows, D_ATTN), F32)
        for p in range(ppc):
            out = out + lax.dot_general(pb[:, p * PAGE_SIZE:(p + 1) * PAGE_SIZE], vbuf[slot, p].astype(BF16),
                                        NT_DIMS, preferred_element_type=F32)
        return out

    update(s, live, pv_pages)

    @pl.when(c == n_chunks - 1)
    def _():
        s_new = lax.dot_general(qbd, kn_ref[0], NT_DIMS, preferred_element_type=F32) + bias_ref[1]
        update(s_new, expand(seln_ref[0, :, 0, :]) > 0.5,
               lambda pb: jnp.dot(pb, vn_ref[0], preferred_element_type=F32))
        full = acc_ref[...] / l_ref[...] * hsel_ref[...]
        o_ref[0] = jnp.sum(full.reshape(n_q, N_HEADS, D_ATTN), axis=1)


def _sample_attention_dense(page_table, qbd, sel, k_new, v_new, sbias, hsel, ckt, cvt):
    nb, n_pages = page_table.shape
    rows = qbd.shape[1]
    n_q = rows // N_HEADS
    ppc = 16
    assert n_pages % ppc == 0, "cached pages are attended in chunks of 16"
    n_chunks = n_pages // ppc
    ck = ppc * PAGE_SIZE
    kern = functools.partial(_sdense_kernel, n_q=n_q, n_pages=n_pages, ppc=ppc)
    return pl.pallas_call(
        kern,
        grid_spec=pltpu.PrefetchScalarGridSpec(
            num_scalar_prefetch=1,
            grid=(nb, n_chunks),
            in_specs=[
                pl.BlockSpec((1, rows, D_ATTN), lambda b, c, pt: (b, 0, 0)),
                pl.BlockSpec((1, n_q, ppc, LANES), lambda b, c, pt: (b, 0, c, 0)),
                pl.BlockSpec((1, n_q, SUBLANES, LANES), lambda b, c, pt: (b, 0, n_pages // SUBLANES, 0)),
                pl.BlockSpec((1, LANES, D_ATTN), lambda b, c, pt: (b, 0, 0)),
                pl.BlockSpec((1, LANES, D_ATTN), lambda b, c, pt: (b, 0, 0)),
                pl.BlockSpec((3, rows, LANES), lambda b, c, pt: (0, 0, 0)),
                pl.BlockSpec((rows, D_ATTN), lambda b, c, pt: (0, 0)),
                pl.BlockSpec(memory_space=pl.ANY),
                pl.BlockSpec(memory_space=pl.ANY),
            ],
            out_specs=pl.BlockSpec((1, n_q, D_ATTN), lambda b, c, pt: (b, 0, 0)),
            scratch_shapes=[
                pltpu.VMEM((2, ppc, D_ATTN, PAGE_SIZE), F32),
                pltpu.VMEM((2, ppc, D_ATTN, PAGE_SIZE), F32),
                pltpu.SemaphoreType.DMA((2, 2)),
                pltpu.VMEM((rows, ck), F32),
                pltpu.VMEM((rows, 1), F32),
                pltpu.VMEM((rows, 1), F32),
                pltpu.VMEM((rows, D_ATTN), F32),
            ],
        ),
        out_shape=jax.ShapeDtypeStruct((nb, n_q, D_ATTN), F32),
        compiler_params=_cparams(("arbitrary", "arbitrary")),
        name="sample_attention",
    )(page_table, qbd, sel, sel, k_new, v_new, sbias, hsel, ckt, cvt)


def _sample_bias_kernel(rbt_ref, o_ref, *, n_q):
    rows = n_q * N_HEADS
    q = lax.broadcasted_iota(I32, (rows, LANES), 0) // N_HEADS
    col = lax.broadcasted_iota(I32, (rows, LANES), 1)
    rb = rbt_ref[...]
    o_ref[0] = _bias_of(PAGE_SIZE + q - col, rb)
    o_ref[1] = _bias_of(q - col, rb)
    o_ref[2] = _bias_of(jnp.full((rows, LANES), REL_MAX_DIST, I32), rb)


def _sample_bias(rbt_rows, n_q):
    rows = n_q * N_HEADS
    return pl.pallas_call(
        functools.partial(_sample_bias_kernel, n_q=n_q),
        out_shape=jax.ShapeDtypeStruct((3, rows, LANES), F32),
        name="sample_bias",
    )(rbt_rows)


def _merge_kernel(o_ref, ya_ref, ga_ref, gb_ref, x_ref, wb_ref, wo_ref, x1_ref):
    yb = jnp.dot(o_ref[...].astype(BF16), wb_ref[...], preferred_element_type=F32)
    m = (_sigmoid(ga_ref[...].astype(F32)) * ya_ref[...].astype(F32)
         + _sigmoid(gb_ref[...].astype(F32)) * yb)
    x1_ref[...] = x_ref[...] + jnp.dot(m.astype(BF16), wo_ref[...], preferred_element_type=F32)


def _merge(o2d, ya, gates, x2d, wb, wo, tm):
    n = x2d.shape[0]
    return pl.pallas_call(
        _merge_kernel,
        grid=(n // tm,),
        in_specs=[
            pl.BlockSpec((tm, D_ATTN), lambda i: (i, 0)),
            pl.BlockSpec((tm, D_MODEL), lambda i: (i, 0)),
            pl.BlockSpec((tm, D_MODEL), lambda i: (i, 0)),
            pl.BlockSpec((tm, D_MODEL), lambda i: (i, 1)),
            pl.BlockSpec((tm, D_MODEL), lambda i: (i, 0)),
            pl.BlockSpec((D_ATTN, D_MODEL), lambda i: (0, 0)),
            pl.BlockSpec((D_MODEL, D_MODEL), lambda i: (0, 0)),
        ],
        out_specs=pl.BlockSpec((tm, D_MODEL), lambda i: (i, 0)),
        out_shape=jax.ShapeDtypeStruct((n, D_MODEL), F32),
        compiler_params=_cparams(("parallel",)),
        name="merge_out_proj",
    )(o2d, ya, gates, gates, x2d, wb, wo)


def _ffn_kernel(x_ref, g_ref, wua_ref, wub_ref, fcw_ref, fcb_ref, past_ref, wd_ref, y_ref, tail_ref,
                xn_ref, carry_ref, ext_ref, acc_ref, *, stride, tm, pad, tail_loc, rs):
    t = pl.program_id(1)
    f = pl.program_id(2)
    nf = pl.num_programs(2)

    @pl.when(f == 0)
    def _():
        x = x_ref[...]
        ms = jnp.mean(x * x, axis=-1, keepdims=True)
        xn_ref[...] = (x * lax.rsqrt(ms + EPS) * g_ref[...]).astype(BF16)
        acc_ref[...] = jnp.zeros(acc_ref.shape, F32)

    @pl.when(t == 0)
    def _():
        ext_ref[0:pad, :] = past_ref[0]

    @pl.when(t > 0)
    def _():
        ext_ref[0:pad, :] = carry_ref[f]

    for r0 in range(0, tm, rs):
        xs = xn_ref[r0:r0 + rs, :]
        a = jnp.dot(xs, wua_ref[...], preferred_element_type=F32)
        bq = jnp.dot(xs, wub_ref[...], preferred_element_type=F32)
        ext_ref[pad + r0:pad + r0 + rs, :] = a
        lo = pad + r0
        conv = (ext_ref[lo - 2 * stride:lo - 2 * stride + rs, :] * fcw_ref[0:1, :]
                + ext_ref[lo - stride:lo - stride + rs, :] * fcw_ref[1:2, :]
                + a * fcw_ref[2:3, :] + fcb_ref[...])
        h = conv * _sigmoid(conv) * bq
        acc_ref[r0:r0 + rs, :] += jnp.dot(h.astype(BF16), wd_ref[...], preferred_element_type=F32)

    carry_ref[f] = ext_ref[tm:tm + pad, :]
    tail_ref[0, 0] = ext_ref[tail_loc:tail_loc + pad, :]

    @pl.when(f == nf - 1)
    def _():
        y_ref[...] = x_ref[...] + acc_ref[...]


def _ffn(x1, g, wup, fcw, fcb, past, wd, *, nb, nt, tm, stride, tf, rows_real):
    pad = past.shape[1]
    nf = D_FF // tf
    tail_tile, tail_loc = _tail_position(rows_real, tm)
    rs = tm // 2 if tm % (2 * SUBLANES) == 0 and tm // 2 >= 2 * stride else tm
    kern = functools.partial(_ffn_kernel, stride=stride, tm=tm, pad=pad, tail_loc=tail_loc, rs=rs)
    y, tail = pl.pallas_call(
        kern,
        grid=(nb, nt, nf),
        in_specs=[
            pl.BlockSpec((tm, D_MODEL), lambda b, t, f: (b * nt + t, 0)),
            pl.BlockSpec((1, D_MODEL), lambda b, t, f: (0, 0)),
            pl.BlockSpec((D_MODEL, tf), lambda b, t, f: (0, f)),
            pl.BlockSpec((D_MODEL, tf), lambda b, t, f: (0, nf + f)),
            pl.BlockSpec((SUBLANES, tf), lambda b, t, f: (0, f)),
            pl.BlockSpec((1, tf), lambda b, t, f: (0, f)),
            pl.BlockSpec((1, pad, tf), lambda b, t, f: (b, 0, f)),
            pl.BlockSpec((tf, D_MODEL), lambda b, t, f: (f, 0)),
        ],
        out_specs=[
            pl.BlockSpec((tm, D_MODEL), lambda b, t, f: (b * nt + t, 0)),
            pl.BlockSpec((1, 1, pad, tf), lambda b, t, f: (b, t, 0, f)),
        ],
        out_shape=[jax.ShapeDtypeStruct((nb * nt * tm, D_MODEL), F32),
                   jax.ShapeDtypeStruct((nb, nt, pad, D_FF), F32)],
        scratch_shapes=[
            pltpu.VMEM((tm, D_MODEL), BF16),
            pltpu.VMEM((nf, pad, tf), F32),
            pltpu.VMEM((pad + tm, tf), F32),
            pltpu.VMEM((tm, D_MODEL), F32),
        ],
        compiler_params=_cparams(("parallel", "arbitrary", "arbitrary")),
        name="conv_ffn",
    )(x1, g, wup, wup, fcw, fcb, past, wd)
    return y, tail[:, tail_tile]


def _row_tile(n, cap):
    best = SUBLANES
    for cand in range(SUBLANES, cap + 1, SUBLANES):
        if n % cand == 0:
            best = cand
    return best


def kernel(x_prompt, x_sample, cache_k, cache_v, cache_idx_k, page_table, state_conv, state_ffn_conv, meta_tokens, g_attn_norm, w_in, conv_w, conv_b, conv_ln_g, conv_ln_b, w_a_out, q_norm_g, k_norm_g, rel_bias, w_b_out, w_o, g_ffn_norm, w_up, ffn_conv_w, ffn_conv_b, w_down):
    nbp, seq, _ = x_prompt.shape
    nbs, n_q, _ = x_sample.shape
    n_pages = page_table.shape[1]
    past_len = n_pages * PAGE_SIZE
    t_real = seq + N_META
    tp = _ceil_to(t_real, LANES)
    keep_p = min(TOP_K_MAX, t_real // 4)
    keep_s = min(TOP_K_MAX, (past_len + n_q) // 4)
    assert g_attn_norm.shape[0] == 1, "single trunk layer"

    w = w_in[0]
    c_q, c_v, c_qi, c_ki, c_wi, c_ga = 1024, 2048, 2560, 3072, 3136, 3144
    wm = jnp.concatenate([w[:, :c_ki], w[:, c_ga:]], axis=1).astype(BF16)
    wt = jnp.stack([w[:, c:c + D_ATTN].T for c in (c_q, c_v, c_qi)]).astype(BF16)
    ws = jnp.concatenate([w[:, c_ki:c_wi], jnp.zeros((D_MODEL, LANES - IDX_DIM), F32)], axis=1).astype(BF16)
    wst = w[:, c_wi:c_ga].T.astype(BF16)
    g_attn = g_attn_norm[0][None, :]
    gq = (jnp.tile(q_norm_g[0], N_HEADS) * ATTN_SCALE)[:, None]
    gk = jnp.tile(k_norm_g[0], N_HEADS)[None, :]
    hid = jnp.arange(D_ATTN) // HEAD_DIM
    seg = jnp.where(hid[:LANES, None] == hid[None, :LANES], 1.0 / HEAD_DIM, 0.0).astype(BF16)
    cw = jnp.concatenate([conv_w[0], jnp.zeros((32 - CONV_W, D_CONV), F32)], axis=0)
    cb, lg, lb = conv_b[0][None, :], conv_ln_g[0][None, :], conv_ln_b[0][None, :]
    wa = w_a_out[0].astype(BF16)
    wb = w_b_out[0].astype(BF16)
    wo = w_o[0].astype(BF16)
    g_ffn = g_ffn_norm[0][None, :]
    wup = w_up[0].astype(BF16)
    fcw = jnp.concatenate([ffn_conv_w[0], jnp.zeros((SUBLANES - FFN_CONV_W, D_FF), F32)], axis=0)
    fcb = ffn_conv_b[0][None, :]
    wd = w_down[0].astype(BF16)
    rbt = rel_bias.T
    ar = jnp.arange(LANES)
    tril = (ar[None, :] <= ar[:, None]).astype(BF16)
    tri = tril.T

    def project(x2d):
        return _in_proj(x2d, g_attn, wm, wt, ws, wst, gq, gk, seg, _row_tile(x2d.shape[0], 1024))

    xp = jnp.concatenate([jnp.broadcast_to(meta_tokens[None], (nbp, N_META, D_MODEL)), x_prompt,
                          jnp.zeros((nbp, tp - t_real, D_MODEL), F32)], axis=1).reshape(nbp * tp, D_MODEL)
    n_p = nbp * tp
    a_p, k_p, kb_p, v_p, qt_p, vt_p, qit_p, g2_p, ki_p, kib_p, wit_p = project(xp)

    tm_p = _row_tile(tp, 640)
    nt_p = tp // tm_p
    ya_p, ctail_p = _conv_branch(a_p, jnp.zeros((nbp, 32, D_CONV), F32), cw, cb, lg, lb, wa,
                                 nb=nbp, nt=nt_p, tm=tm_p, stride=1, rows_real=t_real)
    o_p = _prompt_attention(qt_p, kb_p, vt_p, qit_p, kib_p, wit_p, _prompt_bias(rbt), tril, keep_p, nbp, tp)
    x1_p = _merge(o_p, ya_p, g2_p, xp, wb, wo, _row_tile(n_p, 512))
    y_p, ftail_p = _ffn(x1_p, g_ffn, wup, fcw, fcb, jnp.zeros((nbp, SUBLANES, D_FF), F32), wd,
                        nb=nbp, nt=nt_p, tm=tm_p, stride=1, tf=FFN_TILE, rows_real=t_real)

    y_prompt = y_p.reshape(nbp, tp, D_MODEL)[:, N_META:t_real]
    p_k = k_p.reshape(nbp, tp, N_HEADS, HEAD_DIM)[None, :, :t_real]
    p_v = v_p.reshape(nbp, tp, N_HEADS, HEAD_DIM)[None, :, :t_real]
    p_ik = ki_p.reshape(nbp, tp, IDX_DIM)[None, :, :t_real]
    p_cv = ctail_p[None, :, 32 - (CONV_W - 1):]
    p_fc = ftail_p[None, :, SUBLANES - (FFN_CONV_W - 1):]

    n_s = nbs * n_q
    xs = x_sample.transpose(1, 0, 2).reshape(n_s, D_MODEL)
    a_s, k_s2, _, v_s2, qt_s, _, qit_s, g2_s, ki_s2, _, wit_s = project(xs)

    pad_c = _ceil_to((CONV_W - 1) * nbs, SUBLANES)
    past_c = state_conv[0].transpose(1, 0, 2).reshape(1, (CONV_W - 1) * nbs, D_CONV)
    past_c = jnp.pad(past_c, ((0, 0), (pad_c - (CONV_W - 1) * nbs, 0), (0, 0)))
    ya_s, ctail_s = _conv_branch(a_s, past_c, cw, cb, lg, lb, wa, nb=1, nt=1, tm=n_s, stride=nbs,
                                 rows_real=n_s)

    def batch_major(x2d, *tail):
        return x2d.reshape((n_q, nbs) + tail).transpose((1, 0) + tuple(range(2, 2 + len(tail))))

    q_s = batch_major(qt_s.T.astype(F32), N_HEADS, HEAD_DIM)
    k_s = batch_major(k_s2, N_HEADS, HEAD_DIM)
    v_s = batch_major(v_s2, N_HEADS, HEAD_DIM)
    qi_s = batch_major(qit_s.T, IDX_HEADS, IDX_DIM)
    ki_s = batch_major(ki_s2, IDX_DIM)
    wi_s = batch_major(wit_s.T, IDX_HEADS)
    rows = n_q * N_HEADS

    qi_rows = qi_s.reshape(nbs, rows, IDX_DIM)
    wi_rows = wi_s.reshape(nbs, rows, 1)
    ki_new = jnp.pad(ki_s, ((0, 0), (0, LANES - n_q), (0, 0))).astype(BF16)
    n_pool = cache_k.shape[1]
    ckt = cache_k.transpose(0, 1, 3, 4, 2).reshape(n_pool, D_ATTN, PAGE_SIZE)
    cvt = cache_v.transpose(0, 1, 3, 4, 2).reshape(n_pool, D_ATTN, PAGE_SIZE)
    cikt = cache_idx_k.transpose(0, 1, 3, 2).reshape(n_pool, IDX_DIM, PAGE_SIZE)
    sel = _sample_index_t(page_table, qi_rows, wi_rows, ki_new, tri, cikt, keep_s)

    eye = jnp.eye(N_HEADS, dtype=F32)
    qbd = (q_s[:, :, :, None, :] * eye[None, None, :, :, None]).reshape(nbs, rows, D_ATTN).astype(BF16)
    k_new = jnp.pad(k_s.reshape(nbs, n_q, D_ATTN), ((0, 0), (0, LANES - n_q), (0, 0))).astype(BF16)
    v_new = jnp.pad(v_s.reshape(nbs, n_q, D_ATTN), ((0, 0), (0, LANES - n_q), (0, 0))).astype(BF16)
    hsel = (jnp.arange(rows)[:, None] % N_HEADS == hid[None, :]).astype(F32)
    o_s = _sample_attention_dense(page_table, qbd, sel, k_new, v_new, _sample_bias(jnp.tile(rbt, (n_q, 1)), n_q),
                                  hsel, ckt, cvt)
    o_s2d = o_s.transpose(1, 0, 2).reshape(n_s, D_ATTN)

    x1_s = _merge(o_s2d, ya_s, g2_s, xs, wb, wo, _row_tile(n_s, 512))
    pad_f = _ceil_to((FFN_CONV_W - 1) * nbs, SUBLANES)
    past_f = state_ffn_conv[0].transpose(1, 0, 2).reshape(1, (FFN_CONV_W - 1) * nbs, D_FF)
    past_f = jnp.pad(past_f, ((0, 0), (pad_f - (FFN_CONV_W - 1) * nbs, 0), (0, 0)))
    y_s, ftail_s = _ffn(x1_s, g_ffn, wup, fcw, fcb, past_f, wd, nb=1, nt=1, tm=n_s, stride=nbs, tf=FFN_TILE,
                        rows_real=n_s)

    y_sample = y_s.reshape(n_q, nbs, D_MODEL).transpose(1, 0, 2)
    s_cv = ctail_s[0, pad_c - (CONV_W - 1) * nbs:].reshape(CONV_W - 1, nbs, D_CONV).transpose(1, 0, 2)[None]
    s_fc = ftail_s[0, pad_f - (FFN_CONV_W - 1) * nbs:].reshape(FFN_CONV_W - 1, nbs, D_FF).transpose(1, 0, 2)[None]

    return (y_prompt, y_sample, p_k, p_v, p_ik, p_cv, p_fc,
            k_s[None], v_s[None], ki_s[None], s_cv, s_fc)
```

```python
import functools
import math

import jax
import jax.numpy as jnp
from jax import lax
from jax.experimental import pallas as pl
from jax.experimental.pallas import tpu as pltpu

F32 = jnp.float32
BF16 = jnp.bfloat16
I32 = jnp.int32

D_MODEL = 1024
D_CONV = D_MODEL // 2
CONV_W = 31
N_HEADS = 8
HEAD_DIM = 64
D_ATTN = N_HEADS * HEAD_DIM
IDX_HEADS = 8
IDX_DIM = 64
TOP_K_MAX = 256
N_BUCKETS = 32
MAX_EXACT = N_BUCKETS // 2
REL_MAX_DIST = 128
D_FF = 2816
FFN_CONV_W = 3
N_META = 16
PAGE_SIZE = 128
FFN_TILE = D_FF // 2
SEARCH_HEAD_BITS = 3
SEARCH_GROUP = 4
EPS = 1e-6
ATTN_SCALE = HEAD_DIM ** -0.5
IDX_SCALE = (IDX_HEADS * IDX_DIM) ** -0.5
D_MAIN = 2 * D_CONV + 4 * D_ATTN + 2 * D_MODEL
JB_Q, JB_K, JB_V, JB_QI, JB_G = 2, 3, 4, 5, 6

LANES = 128
SUBLANES = 8
KEY_INVALID = -2 ** 31
NEG_BIG = -1e30
VMEM_LIMIT = 56 * 1024 * 1024
NT_DIMS = (((1,), (1,)), ((), ()))


def _cparams(sem):
    return pltpu.CompilerParams(dimension_semantics=sem, vmem_limit_bytes=VMEM_LIMIT)


def _sigmoid(x):
    return 1.0 / (1.0 + jnp.exp(-x))


def _ceil_to(x, m):
    return -(-x // m) * m


def _col_reduce(x, op, reduce_fn):
    n = x.shape[0]
    while n > SUBLANES and n % (2 * SUBLANES) == 0:
        n //= 2
        x = op(x[:n], x[n:2 * n])
    return reduce_fn(x, axis=0, keepdims=True)


def _split_bf16(x):
    hi = x.astype(BF16)
    return hi, (x - hi.astype(F32)).astype(BF16)


def _in_proj_kernel(x_ref, g_ref, wm_ref, wt_ref, ws_ref, wst_ref, gq_ref, gk_ref, seg_ref,
                    a_ref, k_ref, kb_ref, v_ref, qt_ref, vt_ref, qit_ref, g2_ref, ki_ref, kib_ref, wit_ref,
                    xn_ref):
    j = pl.program_id(1)

    @pl.when(j == 0)
    def _():
        x = x_ref[...]
        ms = jnp.mean(x * x, axis=-1, keepdims=True)
        xn_ref[...] = (x * lax.rsqrt(ms + EPS) * g_ref[...]).astype(BF16)
        ki = jnp.dot(xn_ref[...], ws_ref[...], preferred_element_type=F32)[:, :IDX_DIM]
        ki_ref[...] = ki
        kib_ref[...] = ki.astype(BF16)
        wit_ref[...] = lax.dot_general(wst_ref[...], xn_ref[...], NT_DIMS, preferred_element_type=F32)

    @pl.when(j < JB_Q)
    def _():
        a_ref[...] = jnp.dot(xn_ref[...], wm_ref[...], preferred_element_type=F32)

    @pl.when(j >= JB_G)
    def _():
        g2_ref[...] = jnp.dot(xn_ref[...], wm_ref[...], preferred_element_type=F32).astype(BF16)

    @pl.when(j == JB_Q)
    def _():
        yt = lax.dot_general(wt_ref[0], xn_ref[...], NT_DIMS, preferred_element_type=F32)
        hi, lo = _split_bf16(yt * yt)
        seg = seg_ref[...]
        ms = jnp.concatenate(
            [jnp.dot(seg, hi[r:r + LANES], preferred_element_type=F32)
             + jnp.dot(seg, lo[r:r + LANES], preferred_element_type=F32) for r in range(0, D_ATTN, LANES)],
            axis=0)
        qt_ref[...] = (yt * lax.rsqrt(ms + EPS) * gq_ref[...]).astype(BF16)

    @pl.when(j == JB_K)
    def _():
        y = jnp.dot(xn_ref[...], wm_ref[...], preferred_element_type=F32)
        hi, lo = _split_bf16(y * y)
        seg = seg_ref[...]
        ms = jnp.concatenate(
            [jnp.dot(hi[:, c:c + LANES], seg, preferred_element_type=F32)
             + jnp.dot(lo[:, c:c + LANES], seg, preferred_element_type=F32) for c in range(0, D_ATTN, LANES)],
            axis=1)
        k = y * lax.rsqrt(ms + EPS) * gk_ref[...]
        k_ref[...] = k
        kb_ref[...] = k.astype(BF16)

    @pl.when(j == JB_V)
    def _():
        v_ref[...] = jnp.dot(xn_ref[...], wm_ref[...], preferred_element_type=F32)
        vt_ref[...] = lax.dot_general(wt_ref[1], xn_ref[...], NT_DIMS, preferred_element_type=F32).astype(BF16)

    @pl.when(j == JB_QI)
    def _():
        qit_ref[...] = lax.dot_general(wt_ref[2], xn_ref[...], NT_DIMS, preferred_element_type=F32).astype(BF16)


def _in_proj(x2d, g, wm, wt, ws, wst, gq, gk, seg, tm):
    n = x2d.shape[0]
    tn = D_ATTN
    nj = D_MAIN // tn
    row = lambda i, j: (i, 0)
    colt = lambda i, j: (0, i)
    const2 = lambda i, j: (0, 0)
    out_shape = [
        jax.ShapeDtypeStruct((n, 2 * D_CONV), F32),
        jax.ShapeDtypeStruct((n, D_ATTN), F32),
        jax.ShapeDtypeStruct((n, D_ATTN), BF16),
        jax.ShapeDtypeStruct((n, D_ATTN), F32),
        jax.ShapeDtypeStruct((D_ATTN, n), BF16),
        jax.ShapeDtypeStruct((D_ATTN, n), BF16),
        jax.ShapeDtypeStruct((D_ATTN, n), BF16),
        jax.ShapeDtypeStruct((n, 2 * D_MODEL), BF16),
        jax.ShapeDtypeStruct((n, IDX_DIM), F32),
        jax.ShapeDtypeStruct((n, IDX_DIM), BF16),
        jax.ShapeDtypeStruct((IDX_HEADS, n), F32),
    ]
    out_specs = [
        pl.BlockSpec((tm, tn), lambda i, j: (i, jnp.minimum(j, 1))),
        pl.BlockSpec((tm, tn), row),
        pl.BlockSpec((tm, tn), row),
        pl.BlockSpec((tm, tn), row),
        pl.BlockSpec((tn, tm), colt),
        pl.BlockSpec((tn, tm), colt),
        pl.BlockSpec((tn, tm), colt),
        pl.BlockSpec((tm, tn), lambda i, j: (i, jnp.clip(j - JB_G, 0, nj - JB_G - 1))),
        pl.BlockSpec((tm, IDX_DIM), row),
        pl.BlockSpec((tm, IDX_DIM), row),
        pl.BlockSpec((IDX_HEADS, tm), colt),
    ]
    return pl.pallas_call(
        _in_proj_kernel,
        grid=(n // tm, nj),
        in_specs=[
            pl.BlockSpec((tm, D_MODEL), row),
            pl.BlockSpec((1, D_MODEL), const2),
            pl.BlockSpec((D_MODEL, tn), lambda i, j: (0, j)),
            pl.BlockSpec((3, tn, D_MODEL), lambda i, j: (0, 0, 0)),
            pl.BlockSpec((D_MODEL, LANES), const2),
            pl.BlockSpec((IDX_HEADS, D_MODEL), const2),
            pl.BlockSpec((D_ATTN, 1), const2),
            pl.BlockSpec((1, D_ATTN), const2),
            pl.BlockSpec((LANES, LANES), const2),
        ],
        out_specs=out_specs,
        out_shape=out_shape,
        scratch_shapes=[pltpu.VMEM((tm, D_MODEL), BF16)],
        compiler_params=_cparams(("parallel", "arbitrary")),
        name="in_proj",
    )(x2d, g, wm, wt, ws, wst, gq, gk, seg)


def _conv_kernel(a_ref, past_ref, cw_ref, cb_ref, lg_ref, lb_ref, wa_ref, ya_ref, tail_ref, ext_ref, h_ref,
                 *, stride, tm, pad, rc, tail_tile, tail_loc):
    t = pl.program_id(1)

    @pl.when(t == 0)
    def _():
        ext_ref[0, 0:pad, :] = past_ref[0]

    @pl.when(t > 0)
    def _():
        ext_ref[0, 0:pad, :] = ext_ref[0, tm:tm + pad, :]

    a = a_ref[...]
    ext_ref[0, pad:pad + tm, :] = a[:, :D_CONV] * _sigmoid(a[:, D_CONV:])

    @pl.when(t <= tail_tile)
    def _():
        tail_ref[0] = ext_ref[0, tail_loc:tail_loc + pad, :]

    n_shift = ext_ref.shape[0]
    length = pad + tm
    for r in range(1, n_shift):
        ext_ref[r, 0:length - SUBLANES, :] = ext_ref[0, r:r + length - SUBLANES, :]

    cb = cb_ref[...]
    lg = lg_ref[...]
    lb = lb_ref[...]
    for r0 in range(0, tm, rc):
        acc = jnp.zeros((rc, D_CONV), F32) + cb
        for w in range(CONV_W):
            off = pad - (CONV_W - 1 - w) * stride + r0
            r = off % n_shift
            acc = acc + ext_ref[r, off - r:off - r + rc, :] * cw_ref[w:w + 1, :]
        mu = jnp.mean(acc, axis=-1, keepdims=True)
        d = acc - mu
        var = jnp.mean(d * d, axis=-1, keepdims=True)
        h = d * lax.rsqrt(var + EPS) * lg + lb
        h_ref[r0:r0 + rc, :] = (h * _sigmoid(h)).astype(BF16)
    ya_ref[...] = jnp.dot(h_ref[...], wa_ref[...], preferred_element_type=F32).astype(BF16)


def _tail_position(rows_real, tm):
    tail_tile = (rows_real - 1) // tm
    return tail_tile, rows_real - tail_tile * tm


def _conv_branch(a_in, past, cw, cb, lg, lb, wa, *, nb, nt, tm, stride, rows_real):
    pad = past.shape[1]
    rc = 32 if tm % 32 == 0 else SUBLANES
    tail_tile, tail_loc = _tail_position(rows_real, tm)
    kern = functools.partial(_conv_kernel, stride=stride, tm=tm, pad=pad, rc=rc,
                             tail_tile=tail_tile, tail_loc=tail_loc)
    return pl.pallas_call(
        kern,
        grid=(nb, nt),
        in_specs=[
            pl.BlockSpec((tm, 2 * D_CONV), lambda b, t: (b * nt + t, 0)),
            pl.BlockSpec((1, pad, D_CONV), lambda b, t: (b, 0, 0)),
            pl.BlockSpec((32, D_CONV), lambda b, t: (0, 0)),
            pl.BlockSpec((1, D_CONV), lambda b, t: (0, 0)),
            pl.BlockSpec((1, D_CONV), lambda b, t: (0, 0)),
            pl.BlockSpec((1, D_CONV), lambda b, t: (0, 0)),
            pl.BlockSpec((D_CONV, D_MODEL), lambda b, t: (0, 0)),
        ],
        out_specs=[
            pl.BlockSpec((tm, D_MODEL), lambda b, t: (b * nt + t, 0)),
            pl.BlockSpec((1, pad, D_CONV), lambda b, t: (b, 0, 0)),
        ],
        out_shape=[jax.ShapeDtypeStruct((nb * nt * tm, D_MODEL), BF16),
                   jax.ShapeDtypeStruct((nb, pad, D_CONV), F32)],
        scratch_shapes=[pltpu.VMEM((1 if stride % SUBLANES == 0 else SUBLANES, pad + tm, D_CONV), F32),
                        pltpu.VMEM((tm, D_CONV), BF16)],
        compiler_params=_cparams(("parallel", "arbitrary")),
        name="conv_branch",
    )(a_in, past, cw, cb, lg, lb, wa)


def _rel_bucket(rel):
    n = jnp.maximum(rel, 0)
    nf = jnp.maximum(n, 1).astype(F32)
    large = MAX_EXACT + (jnp.log(nf / MAX_EXACT) / math.log(REL_MAX_DIST / MAX_EXACT)
                         * (N_BUCKETS - MAX_EXACT)).astype(I32)
    large = jnp.minimum(large, N_BUCKETS - 1)
    return jnp.where(n < MAX_EXACT, n, large)


def _bias_of(rel, rb_rows):
    bucket = _rel_bucket(rel)
    out = jnp.zeros(rel.shape, F32)
    for b in range(N_BUCKETS):
        out = jnp.where(bucket == b, rb_rows[:, b:b + 1], out)
    return out


def _prompt_bias_kernel(rbt_ref, b1_ref):
    key = lax.broadcasted_iota(I32, (LANES, LANES), 0)
    qry = lax.broadcasted_iota(I32, (LANES, LANES), 1)
    for h in range(N_HEADS):
        for d in range(3):
            b1_ref[h * 3 + d] = _bias_of(d * LANES + qry - key, rbt_ref[h:h + 1, :])


def _prompt_bias(rbt):
    return pl.pallas_call(
        _prompt_bias_kernel,
        out_shape=jax.ShapeDtypeStruct((N_HEADS * 3, LANES, LANES), F32),
        name="prompt_bias",
    )(rbt)


def _score_to_key(s, valid):
    bits = pltpu.bitcast(s, I32)
    key = bits ^ ((bits >> 31) & 0x7FFFFFFF)
    key = jnp.where(bits == KEY_INVALID, 0, key)
    return jnp.where(valid, key, KEY_INVALID)


def _threshold_search(count_ge, n_keep, shape):
    cnt_valid = count_ge(jnp.full(shape, KEY_INVALID + 1, I32))
    c0 = count_ge(jnp.zeros(shape, I32))
    ok0 = c0 >= n_keep
    base = jnp.where(ok0, 0, KEY_INVALID).astype(I32)
    cnt = jnp.where(ok0, c0, cnt_valid)

    def step(t, base, cnt):
        cand = base | lax.shift_left(jnp.int32(1), 30 - t)
        c = count_ge(cand)
        ok = c >= n_keep
        return jnp.where(ok, cand, base), jnp.where(ok, c, cnt)

    for t in range(SEARCH_HEAD_BITS):
        base, cnt = step(t, base, cnt)

    def cond(state):
        t, _, _, unsettled = state
        return jnp.logical_and(t < 31, unsettled)

    def body(state):
        t, base, cnt, _ = state
        for u in range(SEARCH_GROUP):
            base, cnt = step(t + u, base, cnt)
        return t + SEARCH_GROUP, base, cnt, jnp.max(cnt) > n_keep

    _, base, cnt, _ = lax.while_loop(cond, body, (jnp.int32(SEARCH_HEAD_BITS), base, cnt, jnp.max(cnt) > n_keep))
    return jnp.maximum(base, KEY_INVALID + 1), cnt


def _pattn_kernel(qt_ref, k_ref, vt_ref, qit_ref, ki_ref, wit_ref, b1_ref, tril_ref, o_ref,
                  keys_ref, qiw_ref, qbd_ref, m_ref, l_ref, acc_ref, ot_ref, p_ref, alpha_ref, sa_ref, sb_ref, ia_ref, ib_ref, *, n_keep):
    i = pl.program_id(1)
    tq = LANES
    row1 = (1, tq)
    lead = (i + 1) % 2
    npair = (i + 1) // 2
    qpos = i * tq + lax.broadcasted_iota(I32, row1, 1)

    for h in range(IDX_HEADS):
        qiw_ref[:, h * tq:(h + 1) * tq] = qit_ref[h * IDX_DIM:(h + 1) * IDX_DIM, :]
    qbd_ref[...] = jnp.zeros(qbd_ref.shape, BF16)
    for h in range(N_HEADS):
        hp, e = divmod(h, 2)
        qbd_ref[hp, e * HEAD_DIM:(e + 1) * HEAD_DIM, e * tq:(e + 1) * tq] = qt_ref[h * HEAD_DIM:(h + 1) * HEAD_DIM, :]
    wit = wit_ref[...] * IDX_SCALE

    blk = (LANES, tq)
    nblk = i + 1

    ck = 2 * LANES
    last_off = jnp.maximum(nblk * LANES - ck, 0)

    def chunk_off(c):
        return pl.multiple_of(jnp.minimum(c * ck, last_off), LANES)

    def idx_scores_into(dst, c):
        dst[...] = jnp.dot(ki_ref[pl.ds(chunk_off(c), ck), :], qiw_ref[...], preferred_element_type=F32)

    def idx_keys_from(src, c):
        off = chunk_off(c)
        s = jnp.zeros((ck, tq), F32)
        for h in range(IDX_HEADS):
            s = s + wit[h:h + 1, :] * jnp.maximum(src[:, h * tq:(h + 1) * tq], 0.0)
        kpos = off + lax.broadcasted_iota(I32, (ck, tq), 0)
        keys_ref[pl.ds(off, ck), :] = _score_to_key(s, kpos <= qpos)

    idx_scores_into(ia_ref, 0)

    def idx_body(t, carry):
        idx_scores_into(ib_ref, 2 * t + 1)
        idx_keys_from(ia_ref, 2 * t)
        idx_scores_into(ia_ref, 2 * t + 2)
        idx_keys_from(ib_ref, 2 * t + 1)
        return carry

    lax.fori_loop(0, (nblk + 3) // 4, idx_body, 0)

    def count_ge(cand):
        cb = jnp.broadcast_to(cand, blk)

        def hit(off):
            return jnp.where(keys_ref[pl.ds(off, LANES), :] >= cb, 1.0, 0.0)

        def body(j, acc):
            off = pl.multiple_of(lead * LANES + j * 2 * LANES, LANES)
            return acc + hit(off) + hit(off + LANES)
        acc = lax.fori_loop(0, npair, body, jnp.where(lead == 1, hit(0), 0.0))
        return _col_reduce(acc, jnp.add, jnp.sum)

    base, cnt = _threshold_search(count_ge, float(n_keep), row1)

    @pl.when(jnp.max(cnt) > float(n_keep))
    def _():
        need = float(n_keep) - count_ge(base + 1)
        tie_q = cnt > float(n_keep)

        def body(j, seen):
            off = pl.multiple_of(j * LANES, LANES)
            k = keys_ref[pl.ds(off, LANES), :]
            eq = jnp.logical_and(k == base, tie_q)
            eqf = jnp.where(eq, 1.0, 0.0)
            pref = seen + jnp.dot(tril_ref[...], eqf.astype(BF16), preferred_element_type=F32)
            keys_ref[pl.ds(off, LANES), :] = jnp.where(jnp.logical_and(eq, pref > need), KEY_INVALID, k)
            return seen + jnp.sum(eqf, axis=0, keepdims=True)
        lax.fori_loop(0, nblk, body, jnp.zeros(row1, F32))

    m_ref[...] = jnp.full(m_ref.shape, NEG_BIG, F32)
    l_ref[...] = jnp.zeros(l_ref.shape, F32)
    acc_ref[...] = jnp.zeros(acc_ref.shape, F32)
    p_ref[...] = jnp.zeros(p_ref.shape, BF16)
    alpha_ref[...] = jnp.ones(alpha_ref.shape, F32)
    base_b = jnp.broadcast_to(base, blk)

    def clamp_blk(jb):
        return jnp.clip(jb, 0, nblk - 1)

    def apply_pv(jb):
        off = pl.multiple_of(clamp_blk(jb) * LANES, LANES)
        for h in range(N_HEADS):
            pv = jnp.dot(vt_ref[h * HEAD_DIM:(h + 1) * HEAD_DIM, pl.ds(off, LANES)], p_ref[h],
                         preferred_element_type=F32)
            acc_ref[h] = alpha_ref[h] * acc_ref[h] + pv

    def qk_into(s_dst, jb):
        off = pl.multiple_of(clamp_blk(jb) * LANES, LANES)
        for hp in range(N_HEADS // 2):
            for e in range(2):
                s_dst[2 * hp + e] = jnp.dot(k_ref[pl.ds(off, LANES), hp * LANES:(hp + 1) * LANES],
                                            qbd_ref[hp, :, e * tq:(e + 1) * tq], preferred_element_type=F32)

    def softmax_from(s_src, jb):
        jc = clamp_blk(jb)
        off = pl.multiple_of(jc * LANES, LANES)
        sel = jnp.logical_and(keys_ref[pl.ds(off, LANES), :] >= base_b, jb < nblk)
        d = jnp.minimum(i - jc, 2)
        for h in range(N_HEADS):
            s = jnp.where(sel, s_src[h] + b1_ref[h * 3 + d], NEG_BIG)
            m_old = m_ref[h]
            m_new = jnp.maximum(m_old, _col_reduce(s, jnp.maximum, jnp.max))
            p = jnp.exp(s - m_new)
            alpha = jnp.exp(m_old - m_new)
            l_ref[h] = alpha * l_ref[h] + _col_reduce(p, jnp.add, jnp.sum)
            p_ref[h] = p.astype(BF16)
            alpha_ref[h] = alpha
            m_ref[h] = m_new

    def half_trip(jb, s_cur, s_nxt):
        apply_pv(jb - 1)
        qk_into(s_nxt, jb + 1)
        softmax_from(s_cur, jb)

    qk_into(sa_ref, 0)

    def att_body(t, carry):
        half_trip(2 * t, sa_ref, sb_ref)
        half_trip(2 * t + 1, sb_ref, sa_ref)
        return carry

    n_trips = (nblk + 1) // 2
    lax.fori_loop(0, n_trips, att_body, 0)
    apply_pv(2 * n_trips - 1)
    for h in range(N_HEADS):
        ot_ref[h * HEAD_DIM:(h + 1) * HEAD_DIM, :] = acc_ref[h] / l_ref[h]
    o_ref[...] = ot_ref[...].T


def _prompt_attention(qt, kb, vt, qit, kib, wit, b1, tril, n_keep, nb, tp):
    tq = LANES
    nq = tp // tq
    n = nb * tp
    qcol = lambda b, i: (0, b * nq + i)
    return pl.pallas_call(
        functools.partial(_pattn_kernel, n_keep=n_keep),
        grid=(nb, nq),
        in_specs=[
            pl.BlockSpec((D_ATTN, tq), qcol),
            pl.BlockSpec((tp, D_ATTN), lambda b, i: (b, 0)),
            pl.BlockSpec((D_ATTN, tp), lambda b, i: (0, b)),
            pl.BlockSpec((IDX_HEADS * IDX_DIM, tq), qcol),
            pl.BlockSpec((tp, IDX_DIM), lambda b, i: (b, 0)),
            pl.BlockSpec((IDX_HEADS, tq), qcol),
            pl.BlockSpec((N_HEADS * 3, LANES, LANES), lambda b, i: (0, 0, 0)),
            pl.BlockSpec((LANES, LANES), lambda b, i: (0, 0)),
        ],
        out_specs=pl.BlockSpec((tq, D_ATTN), lambda b, i: (b * nq + i, 0)),
        out_shape=jax.ShapeDtypeStruct((n, D_ATTN), F32),
        scratch_shapes=[
            pltpu.VMEM((tp, tq), I32),
            pltpu.VMEM((IDX_DIM, IDX_HEADS * tq), BF16),
            pltpu.VMEM((N_HEADS // 2, LANES, 2 * tq), BF16),
            pltpu.VMEM((N_HEADS, 1, tq), F32),
            pltpu.VMEM((N_HEADS, 1, tq), F32),
            pltpu.VMEM((N_HEADS, HEAD_DIM, tq), F32),
            pltpu.VMEM((D_ATTN, tq), F32),
            pltpu.VMEM((N_HEADS, LANES, tq), BF16),
            pltpu.VMEM((N_HEADS, 1, tq), F32),
            pltpu.VMEM((N_HEADS, LANES, tq), F32),
            pltpu.VMEM((N_HEADS, LANES, tq), F32),
            pltpu.VMEM((2 * LANES, IDX_HEADS * tq), F32),
            pltpu.VMEM((2 * LANES, IDX_HEADS * tq), F32),
        ],
        compiler_params=_cparams(("parallel", "arbitrary")),
        name="prompt_attention",
    )(qt, kb, vt, qit, kib, wit, b1, tril)


def _sidx_t_kernel(pt_ref, qi_ref, wi_ref, kin_ref, tri_ref, lin_ref, cache_ref, sel_ref,
                   kibuf, sem, keys_ref, *, n_keep, n_pages, n_q, ppc, ncs, ncp, gb):
    b = pl.program_id(0)
    nb = pl.num_programs(0)
    slot = b % 2

    def page_copy(step, g, p, sl):
        return pltpu.make_async_copy(cache_ref.at[pt_ref[step * gb + g, p]], kibuf.at[sl, g * n_pages + p],
                                     sem.at[sl])

    def start_all(step, sl):
        for g in range(gb):
            def body(p, c):
                page_copy(step, g, p, sl).start()
                return c
            lax.fori_loop(0, n_pages, body, 0, unroll=ppc)

    def wait_all(step, sl):
        for g in range(gb):
            def body(p, c):
                page_copy(step, g, p, sl).wait()
                return c
            lax.fori_loop(0, n_pages, body, 0, unroll=ppc)

    @pl.when(b == 0)
    def _():
        start_all(b, slot)

    @pl.when(b + 1 < nb)
    def _():
        start_all(b + 1, 1 - slot)

    wait_all(b, slot)
    keys_ref[...] = jnp.full(keys_ref.shape, KEY_INVALID, I32)
    qrow = lax.broadcasted_iota(I32, (n_q, LANES), 0)
    jcol = lax.broadcasted_iota(I32, (n_q, LANES), 1)

    for g in range(gb):
        qi = qi_ref[g]
        w = wi_ref[g] * IDX_SCALE

        def score_rows(s):
            s = jnp.maximum(s, 0.0) * w
            return jnp.sum(s.reshape(n_q, IDX_HEADS, s.shape[-1]), axis=1)

        def chunk_body(c, carry):
            p0 = pl.multiple_of(c * ppc, ppc)
            kt = jnp.concatenate([kibuf[slot, g * n_pages + p0 + p] for p in range(ppc)],
                                 axis=1).astype(BF16)
            s = score_rows(jnp.dot(qi, kt, preferred_element_type=F32))
            key = _score_to_key(s, jnp.full(s.shape, True))
            for q in range(n_q):
                for p in range(ppc):
                    keys_ref[g * n_q + q, pl.ds(p0 + p, 1), :] = key[q:q + 1, p * LANES:(p + 1) * LANES]
            return carry

        lax.fori_loop(0, n_pages // ppc, chunk_body, 0)
        s_new = score_rows(lax.dot_general(qi, kin_ref[g], NT_DIMS, preferred_element_type=F32))
        key_new = _score_to_key(s_new, jcol <= qrow)
        for q in range(n_q):
            keys_ref[g * n_q + q, n_pages:n_pages + 1, :] = key_new[q:q + 1, :]

    shape = (gb * n_q, 1, LANES)

    def count_ge(cand):
        hit = jnp.where(keys_ref[:, 0:ncs, :] >= cand, 1.0, 0.0)
        part = jnp.sum(hit, axis=1, keepdims=True)
        return jnp.broadcast_to(jnp.sum(part, axis=2, keepdims=True), shape)

    base, cnt = _threshold_search(count_ge, float(n_keep), shape)

    @pl.when(jnp.max(cnt) > float(n_keep))
    def _():
        need = float(n_keep) - count_ge(base + 1)
        ones = jnp.ones((LANES, LANES), BF16)
        for q in range(gb * n_q):
            k = keys_ref[q]
            eq = jnp.logical_and(k == base[q], cnt[q] > float(n_keep))
            eqb = jnp.where(eq, 1.0, 0.0).astype(BF16)
            within = jnp.dot(eqb, tri_ref[...], preferred_element_type=F32)
            rowcnt = jnp.dot(eqb, ones, preferred_element_type=F32).astype(BF16)
            before = jnp.dot(lin_ref[...], rowcnt, preferred_element_type=F32) - rowcnt.astype(F32)
            drop = jnp.logical_and(eq, within + before > need[q])
            keys_ref[q] = jnp.where(drop, KEY_INVALID, k)

    sel = jnp.where(keys_ref[:, 0:ncs, :] >= base, 1.0, 0.0)
    for g in range(gb):
        sel_ref[g] = sel[g * n_q:(g + 1) * n_q]


def _sample_index_t(page_table, qi_rows, wi_rows, ki_new, tri, cache_ikt, n_keep):
    nb_all, n_pages = page_table.shape
    gb = 2 if nb_all % 2 == 0 else 1
    nb = nb_all // gb
    rows = qi_rows.shape[1]
    n_q = rows // IDX_HEADS
    ppc = 8 if n_pages % 8 == 0 else 1
    ncs = _ceil_to(n_pages + 1, SUBLANES)
    ncp = _ceil_to(n_pages + 1, LANES)
    assert ncp <= 256, "tie-break prefix counts are carried in bf16, exact up to 256"
    ar = jnp.arange(ncp)
    lin = (ar[None, :] <= ar[:, None]).astype(BF16)
    kern = functools.partial(_sidx_t_kernel, n_keep=n_keep, n_pages=n_pages, n_q=n_q, ppc=ppc,
                             ncs=ncs, ncp=ncp, gb=gb)
    return pl.pallas_call(
        kern,
        grid_spec=pltpu.PrefetchScalarGridSpec(
            num_scalar_prefetch=1,
            grid=(nb,),
            in_specs=[
                pl.BlockSpec((gb, rows, IDX_DIM), lambda b, pt: (b, 0, 0)),
                pl.BlockSpec((gb, rows, 1), lambda b, pt: (b, 0, 0)),
                pl.BlockSpec((gb, LANES, IDX_DIM), lambda b, pt: (b, 0, 0)),
                pl.BlockSpec((LANES, LANES), lambda b, pt: (0, 0)),
                pl.BlockSpec((ncp, ncp), lambda b, pt: (0, 0)),
                pl.BlockSpec(memory_space=pl.ANY),
            ],
            out_specs=pl.BlockSpec((gb, n_q, ncs, LANES), lambda b, pt: (b, 0, 0, 0)),
            scratch_shapes=[
                pltpu.VMEM((2, gb * n_pages, IDX_DIM, PAGE_SIZE), F32),
                pltpu.SemaphoreType.DMA((2,)),
                pltpu.VMEM((gb * n_q, ncp, LANES), I32),
            ],
        ),
        out_shape=jax.ShapeDtypeStruct((nb_all, n_q, ncs, LANES), F32),
        compiler_params=_cparams(("arbitrary",)),
        name="sample_index",
    )(page_table, qi_rows, wi_rows, ki_new, tri, lin, cache_ikt)


def _sdense_kernel(pt_ref, qbd_ref, sel_ref, seln_ref, kn_ref, vn_ref, bias_ref, hsel_ref, ck_ref, cv_ref, o_ref,
                   kbuf, vbuf, sem, bias_buf, m_ref, l_ref, acc_ref, *, n_q, n_pages, ppc):
    b = pl.program_id(0)
    c = pl.program_id(1)
    nb = pl.num_programs(0)
    n_chunks = n_pages // ppc
    step = b * n_chunks + c
    slot = step % 2
    rows = n_q * N_HEADS
    ck = ppc * PAGE_SIZE

    def copies(bb, cc, sl, p):
        page = pt_ref[bb, cc * ppc + p]
        return (pltpu.make_async_copy(ck_ref.at[page], kbuf.at[sl, p], sem.at[0, sl]),
                pltpu.make_async_copy(cv_ref.at[page], vbuf.at[sl, p], sem.at[1, sl]))

    def start_all(bb, cc, sl):
        for p in range(ppc):
            kc, vc = copies(bb, cc, sl, p)
            kc.start()
            vc.start()

    def wait_all(bb, cc, sl):
        for p in range(ppc):
            kc, vc = copies(bb, cc, sl, p)
            kc.wait()
            vc.wait()

    @pl.when(step == 0)
    def _():
        start_all(b, c, slot)

    @pl.when(step + 1 < nb * n_chunks)
    def _():
        nxt = step + 1
        start_all(nxt // n_chunks, nxt % n_chunks, 1 - slot)

    @pl.when(c == 0)
    def _():
        m_ref[...] = jnp.full(m_ref.shape, NEG_BIG, F32)
        l_ref[...] = jnp.zeros(l_ref.shape, F32)
        acc_ref[...] = jnp.zeros(acc_ref.shape, F32)
        bias_buf[...] = jnp.broadcast_to(bias_ref[2][:, 0:1], bias_buf.shape)

    @pl.when(c == n_chunks - 1)
    def _():
        bias_buf[:, ck - PAGE_SIZE:ck] = bias_ref[0]

    wait_all(b, c, slot)
    qbd = qbd_ref[0]

    def expand(sel):
        n = sel.shape[-1]
        return jnp.broadcast_to(sel[:, None, :], (n_q, N_HEADS, n)).reshape(rows, n)

    def update(s, live, pv_fn):
        s = jnp.where(live, s, NEG_BIG)
        m_old = m_ref[...]
        m_new = jnp.maximum(m_old, jnp.max(s, axis=1, keepdims=True))
        p = jnp.exp(s - m_new)
        alpha = jnp.exp(m_old - m_new)
        l_ref[...] = alpha * l_ref[...] + jnp.sum(p, axis=1, keepdims=True)
        acc_ref[...] = alpha * acc_ref[...] + pv_fn(p.astype(BF16))
        m_ref[...] = m_new

    s = jnp.concatenate([jnp.dot(qbd, kbuf[slot, p].astype(BF16), preferred_element_type=F32)
                         for p in range(ppc)], axis=1) + bias_buf[...]
    live = jnp.concatenate([expand(sel_ref[0, :, p, :]) for p in range(ppc)], axis=1) > 0.5

    def pv_pages(pb):
        out = jnp.zeros((rows, D_ATTN), F32)
        for p in range(ppc):
            out = out + lax.dot_general(pb[:, p * PAGE_SIZE:(p + 1) * PAGE_SIZE], vbuf[slot, p].astype(BF16),
                                        NT_DIMS, preferred_element_type=F32)
        return out

    update(s, live, pv_pages)

    @pl.when(c == n_chunks - 1)
    def _():
        s_new = lax.dot_general(qbd, kn_ref[0], NT_DIMS, preferred_element_type=F32) + bias_ref[1]
        update(s_new, expand(seln_ref[0, :, 0, :]) > 0.5,
               lambda pb: jnp.dot(pb, vn_ref[0], preferred_element_type=F32))
        full = acc_ref[...] / l_ref[...] * hsel_ref[...]
        o_ref[0] = jnp.sum(full.reshape(n_q, N_HEADS, D_ATTN), axis=1)


def _sample_attention_dense(page_table, qbd, sel, k_new, v_new, sbias, hsel, ckt, cvt):
    nb, n_pages = page_table.shape
    rows = qbd.shape[1]
    n_q = rows // N_HEADS
    ppc = 16
    assert n_pages % ppc == 0, "cached pages are attended in chunks of 16"
    n_chunks = n_pages // ppc
    ck = ppc * PAGE_SIZE
    kern = functools.partial(_sdense_kernel, n_q=n_q, n_pages=n_pages, ppc=ppc)
    return pl.pallas_call(
        kern,
        grid_spec=pltpu.PrefetchScalarGridSpec(
            num_scalar_prefetch=1,
            grid=(nb, n_chunks),
            in_specs=[
                pl.BlockSpec((1, rows, D_ATTN), lambda b, c, pt: (b, 0, 0)),
                pl.BlockSpec((1, n_q, ppc, LANES), lambda b, c, pt: (b, 0, c, 0)),
                pl.BlockSpec((1, n_q, SUBLANES, LANES), lambda b, c, pt: (b, 0, n_pages // SUBLANES, 0)),
                pl.BlockSpec((1, LANES, D_ATTN), lambda b, c, pt: (b, 0, 0)),
                pl.BlockSpec((1, LANES, D_ATTN), lambda b, c, pt: (b, 0, 0)),
                pl.BlockSpec((3, rows, LANES), lambda b, c, pt: (0, 0, 0)),
                pl.BlockSpec((rows, D_ATTN), lambda b, c, pt: (0, 0)),
                pl.BlockSpec(memory_space=pl.ANY),
                pl.BlockSpec(memory_space=pl.ANY),
            ],
            out_specs=pl.BlockSpec((1, n_q, D_ATTN), lambda b, c, pt: (b, 0, 0)),
            scratch_shapes=[
                pltpu.VMEM((2, ppc, D_ATTN, PAGE_SIZE), F32),
                pltpu.VMEM((2, ppc, D_ATTN, PAGE_SIZE), F32),
                pltpu.SemaphoreType.DMA((2, 2)),
                pltpu.VMEM((rows, ck), F32),
                pltpu.VMEM((rows, 1), F32),
                pltpu.VMEM((rows, 1), F32),
                pltpu.VMEM((rows, D_ATTN), F32),
            ],
        ),
        out_shape=jax.ShapeDtypeStruct((nb, n_q, D_ATTN), F32),
        compiler_params=_cparams(("arbitrary", "arbitrary")),
        name="sample_attention",
    )(page_table, qbd, sel, sel, k_new, v_new, sbias, hsel, ckt, cvt)


def _sample_bias_kernel(rbt_ref, o_ref, *, n_q):
    rows = n_q * N_HEADS
    q = lax.broadcasted_iota(I32, (rows, LANES), 0) // N_HEADS
    col = lax.broadcasted_iota(I32, (rows, LANES), 1)
    rb = rbt_ref[...]
    o_ref[0] = _bias_of(PAGE_SIZE + q - col, rb)
    o_ref[1] = _bias_of(q - col, rb)
    o_ref[2] = _bias_of(jnp.full((rows, LANES), REL_MAX_DIST, I32), rb)


def _sample_bias(rbt_rows, n_q):
    rows = n_q * N_HEADS
    return pl.pallas_call(
        functools.partial(_sample_bias_kernel, n_q=n_q),
        out_shape=jax.ShapeDtypeStruct((3, rows, LANES), F32),
        name="sample_bias",
    )(rbt_rows)


def _merge_kernel(o_ref, ya_ref, ga_ref, gb_ref, x_ref, wb_ref, wo_ref, x1_ref):
    yb = jnp.dot(o_ref[...].astype(BF16), wb_ref[...], preferred_element_type=F32)
    m = (_sigmoid(ga_ref[...].astype(F32)) * ya_ref[...].astype(F32)
         + _sigmoid(gb_ref[...].astype(F32)) * yb)
    x1_ref[...] = x_ref[...] + jnp.dot(m.astype(BF16), wo_ref[...], preferred_element_type=F32)


def _merge(o2d, ya, gates, x2d, wb, wo, tm):
    n = x2d.shape[0]
    return pl.pallas_call(
        _merge_kernel,
        grid=(n // tm,),
        in_specs=[
            pl.BlockSpec((tm, D_ATTN), lambda i: (i, 0)),
            pl.BlockSpec((tm, D_MODEL), lambda i: (i, 0)),
            pl.BlockSpec((tm, D_MODEL), lambda i: (i, 0)),
            pl.BlockSpec((tm, D_MODEL), lambda i: (i, 1)),
            pl.BlockSpec((tm, D_MODEL), lambda i: (i, 0)),
            pl.BlockSpec((D_ATTN, D_MODEL), lambda i: (0, 0)),
            pl.BlockSpec((D_MODEL, D_MODEL), lambda i: (0, 0)),
        ],
        out_specs=pl.BlockSpec((tm, D_MODEL), lambda i: (i, 0)),
        out_shape=jax.ShapeDtypeStruct((n, D_MODEL), F32),
        compiler_params=_cparams(("parallel",)),
        name="merge_out_proj",
    )(o2d, ya, gates, gates, x2d, wb, wo)


def _ffn_kernel(x_ref, g_ref, wua_ref, wub_ref, fcw_ref, fcb_ref, past_ref, wd_ref, y_ref, tail_ref,
                xn_ref, carry_ref, ext_ref, acc_ref, *, stride, tm, pad, tail_loc, rs):
    t = pl.program_id(1)
    f = pl.program_id(2)
    nf = pl.num_programs(2)

    @pl.when(f == 0)
    def _():
        x = x_ref[...]
        ms = jnp.mean(x * x, axis=-1, keepdims=True)
        xn_ref[...] = (x * lax.rsqrt(ms + EPS) * g_ref[...]).astype(BF16)
        acc_ref[...] = jnp.zeros(acc_ref.shape, F32)

    @pl.when(t == 0)
    def _():
        ext_ref[0:pad, :] = past_ref[0]

    @pl.when(t > 0)
    def _():
        ext_ref[0:pad, :] = carry_ref[f]

    for r0 in range(0, tm, rs):
        xs = xn_ref[r0:r0 + rs, :]
        a = jnp.dot(xs, wua_ref[...], preferred_element_type=F32)
        bq = jnp.dot(xs, wub_ref[...], preferred_element_type=F32)
        ext_ref[pad + r0:pad + r0 + rs, :] = a
        lo = pad + r0
        conv = (ext_ref[lo - 2 * stride:lo - 2 * stride + rs, :] * fcw_ref[0:1, :]
                + ext_ref[lo - stride:lo - stride + rs, :] * fcw_ref[1:2, :]
                + a * fcw_ref[2:3, :] + fcb_ref[...])
        h = conv * _sigmoid(conv) * bq
        acc_ref[r0:r0 + rs, :] += jnp.dot(h.astype(BF16), wd_ref[...], preferred_element_type=F32)

    carry_ref[f] = ext_ref[tm:tm + pad, :]
    tail_ref[0, 0] = ext_ref[tail_loc:tail_loc + pad, :]

    @pl.when(f == nf - 1)
    def _():
        y_ref[...] = x_ref[...] + acc_ref[...]


def _ffn(x1, g, wup, fcw, fcb, past, wd, *, nb, nt, tm, stride, tf, rows_real):
    pad = past.shape[1]
    nf = D_FF // tf
    tail_tile, tail_loc = _tail_position(rows_real, tm)
    rs = tm // 2 if tm % (2 * SUBLANES) == 0 and tm // 2 >= 2 * stride else tm
    kern = functools.partial(_ffn_kernel, stride=stride, tm=tm, pad=pad, tail_loc=tail_loc, rs=rs)
    y, tail = pl.pallas_call(
        kern,
        grid=(nb, nt, nf),
        in_specs=[
            pl.BlockSpec((tm, D_MODEL), lambda b, t, f: (b * nt + t, 0)),
            pl.BlockSpec((1, D_MODEL), lambda b, t, f: (0, 0)),
            pl.BlockSpec((D_MODEL, tf), lambda b, t, f: (0, f)),
            pl.BlockSpec((D_MODEL, tf), lambda b, t, f: (0, nf + f)),
            pl.BlockSpec((SUBLANES, tf), lambda b, t, f: (0, f)),
            pl.BlockSpec((1, tf), lambda b, t, f: (0, f)),
            pl.BlockSpec((1, pad, tf), lambda b, t, f: (b, 0, f)),
            pl.BlockSpec((tf, D_MODEL), lambda b, t, f: (f, 0)),
        ],
        out_specs=[
            pl.BlockSpec((tm, D_MODEL), lambda b, t, f: (b * nt + t, 0)),
            pl.BlockSpec((1, 1, pad, tf), lambda b, t, f: (b, t, 0, f)),
        ],
        out_shape=[jax.ShapeDtypeStruct((nb * nt * tm, D_MODEL), F32),
                   jax.ShapeDtypeStruct((nb, nt, pad, D_FF), F32)],
        scratch_shapes=[
            pltpu.VMEM((tm, D_MODEL), BF16),
            pltpu.VMEM((nf, pad, tf), F32),
            pltpu.VMEM((pad + tm, tf), F32),
            pltpu.VMEM((tm, D_MODEL), F32),
        ],
        compiler_params=_cparams(("parallel", "arbitrary", "arbitrary")),
        name="conv_ffn",
    )(x1, g, wup, wup, fcw, fcb, past, wd)
    return y, tail[:, tail_tile]


def _row_tile(n, cap):
    best = SUBLANES
    for cand in range(SUBLANES, cap + 1, SUBLANES):
        if n % cand == 0:
            best = cand
    return best


def kernel(x_prompt, x_sample, cache_k, cache_v, cache_idx_k, page_table, state_conv, state_ffn_conv, meta_tokens, g_attn_norm, w_in, conv_w, conv_b, conv_ln_g, conv_ln_b, w_a_out, q_norm_g, k_norm_g, rel_bias, w_b_out, w_o, g_ffn_norm, w_up, ffn_conv_w, ffn_conv_b, w_down):
    nbp, seq, _ = x_prompt.shape
    nbs, n_q, _ = x_sample.shape
    n_pages = page_table.shape[1]
    past_len = n_pages * PAGE_SIZE
    t_real = seq + N_META
    tp = _ceil_to(t_real, LANES)
    keep_p = min(TOP_K_MAX, t_real // 4)
    keep_s = min(TOP_K_MAX, (past_len + n_q) // 4)
    assert g_attn_norm.shape[0] == 1, "single trunk layer"

    w = w_in[0]
    c_q, c_v, c_qi, c_ki, c_wi, c_ga = 1024, 2048, 2560, 3072, 3136, 3144
    wm = jnp.concatenate([w[:, :c_ki], w[:, c_ga:]], axis=1).astype(BF16)
    wt = jnp.stack([w[:, c:c + D_ATTN].T for c in (c_q, c_v, c_qi)]).astype(BF16)
    ws = jnp.concatenate([w[:, c_ki:c_wi], jnp.zeros((D_MODEL, LANES - IDX_DIM), F32)], axis=1).astype(BF16)
    wst = w[:, c_wi:c_ga].T.astype(BF16)
    g_attn = g_attn_norm[0][None, :]
    gq = (jnp.tile(q_norm_g[0], N_HEADS) * ATTN_SCALE)[:, None]
    gk = jnp.tile(k_norm_g[0], N_HEADS)[None, :]
    hid = jnp.arange(D_ATTN) // HEAD_DIM
    seg = jnp.where(hid[:LANES, None] == hid[None, :LANES], 1.0 / HEAD_DIM, 0.0).astype(BF16)
    cw = jnp.concatenate([conv_w[0], jnp.zeros((32 - CONV_W, D_CONV), F32)], axis=0)
    cb, lg, lb = conv_b[0][None, :], conv_ln_g[0][None, :], conv_ln_b[0][None, :]
    wa = w_a_out[0].astype(BF16)
    wb = w_b_out[0].astype(BF16)
    wo = w_o[0].astype(BF16)
    g_ffn = g_ffn_norm[0][None, :]
    wup = w_up[0].astype(BF16)
    fcw = jnp.concatenate([ffn_conv_w[0], jnp.zeros((SUBLANES - FFN_CONV_W, D_FF), F32)], axis=0)
    fcb = ffn_conv_b[0][None, :]
    wd = w_down[0].astype(BF16)
    rbt = rel_bias.T
    ar = jnp.arange(LANES)
    tril = (ar[None, :] <= ar[:, None]).astype(BF16)
    tri = tril.T

    def project(x2d):
        return _in_proj(x2d, g_attn, wm, wt, ws, wst, gq, gk, seg, _row_tile(x2d.shape[0], 1024))

    xp = jnp.concatenate([jnp.broadcast_to(meta_tokens[None], (nbp, N_META, D_MODEL)), x_prompt,
                          jnp.zeros((nbp, tp - t_real, D_MODEL), F32)], axis=1).reshape(nbp * tp, D_MODEL)
    n_p = nbp * tp
    a_p, k_p, kb_p, v_p, qt_p, vt_p, qit_p, g2_p, ki_p, kib_p, wit_p = project(xp)

    tm_p = _row_tile(tp, 640)
    nt_p = tp // tm_p
    ya_p, ctail_p = _conv_branch(a_p, jnp.zeros((nbp, 32, D_CONV), F32), cw, cb, lg, lb, wa,
                                 nb=nbp, nt=nt_p, tm=tm_p, stride=1, rows_real=t_real)
    o_p = _prompt_attention(qt_p, kb_p, vt_p, qit_p, kib_p, wit_p, _prompt_bias(rbt), tril, keep_p, nbp, tp)
    x1_p = _merge(o_p, ya_p, g2_p, xp, wb, wo, _row_tile(n_p, 512))
    y_p, ftail_p = _ffn(x1_p, g_ffn, wup, fcw, fcb, jnp.zeros((nbp, SUBLANES, D_FF), F32), wd,
                        nb=nbp, nt=nt_p, tm=tm_p, stride=1, tf=FFN_TILE, rows_real=t_real)

    y_prompt = y_p.reshape(nbp, tp, D_MODEL)[:, N_META:t_real]
    p_k = k_p.reshape(nbp, tp, N_HEADS, HEAD_DIM)[None, :, :t_real]
    p_v = v_p.reshape(nbp, tp, N_HEADS, HEAD_DIM)[None, :, :t_real]
    p_ik = ki_p.reshape(nbp, tp, IDX_DIM)[None, :, :t_real]
    p_cv = ctail_p[None, :, 32 - (CONV_W - 1):]
    p_fc = ftail_p[None, :, SUBLANES - (FFN_CONV_W - 1):]

    n_s = nbs * n_q
    xs = x_sample.transpose(1, 0, 2).reshape(n_s, D_MODEL)
    a_s, k_s2, _, v_s2, qt_s, _, qit_s, g2_s, ki_s2, _, wit_s = project(xs)

    pad_c = _ceil_to((CONV_W - 1) * nbs, SUBLANES)
    past_c = state_conv[0].transpose(1, 0, 2).reshape(1, (CONV_W - 1) * nbs, D_CONV)
    past_c = jnp.pad(past_c, ((0, 0), (pad_c - (CONV_W - 1) * nbs, 0), (0, 0)))
    ya_s, ctail_s = _conv_branch(a_s, past_c, cw, cb, lg, lb, wa, nb=1, nt=1, tm=n_s, stride=nbs,
                                 rows_real=n_s)

    def batch_major(x2d, *tail):
        return x2d.reshape((n_q, nbs) + tail).transpose((1, 0) + tuple(range(2, 2 + len(tail))))

    q_s = batch_major(qt_s.T.astype(F32), N_HEADS, HEAD_DIM)
    k_s = batch_major(k_s2, N_HEADS, HEAD_DIM)
    v_s = batch_major(v_s2, N_HEADS, HEAD_DIM)
    qi_s = batch_major(qit_s.T, IDX_HEADS, IDX_DIM)
    ki_s = batch_major(ki_s2, IDX_DIM)
    wi_s = batch_major(wit_s.T, IDX_HEADS)
    rows = n_q * N_HEADS

    qi_rows = qi_s.reshape(nbs, rows, IDX_DIM)
    wi_rows = wi_s.reshape(nbs, rows, 1)
    ki_new = jnp.pad(ki_s, ((0, 0), (0, LANES - n_q), (0, 0))).astype(BF16)
    n_pool = cache_k.shape[1]
    ckt = cache_k.transpose(0, 1, 3, 4, 2).reshape(n_pool, D_ATTN, PAGE_SIZE)
    cvt = cache_v.transpose(0, 1, 3, 4, 2).reshape(n_pool, D_ATTN, PAGE_SIZE)
    cikt = cache_idx_k.transpose(0, 1, 3, 2).reshape(n_pool, IDX_DIM, PAGE_SIZE)
    sel = _sample_index_t(page_table, qi_rows, wi_rows, ki_new, tri, cikt, keep_s)

    eye = jnp.eye(N_HEADS, dtype=F32)
    qbd = (q_s[:, :, :, None, :] * eye[None, None, :, :, None]).reshape(nbs, rows, D_ATTN).astype(BF16)
    k_new = jnp.pad(k_s.reshape(nbs, n_q, D_ATTN), ((0, 0), (0, LANES - n_q), (0, 0))).astype(BF16)
    v_new = jnp.pad(v_s.reshape(nbs, n_q, D_ATTN), ((0, 0), (0, LANES - n_q), (0, 0))).astype(BF16)
    hsel = (jnp.arange(rows)[:, None] % N_HEADS == hid[None, :]).astype(F32)
    o_s = _sample_attention_dense(page_table, qbd, sel, k_new, v_new, _sample_bias(jnp.tile(rbt, (n_q, 1)), n_q),
                                  hsel, ckt, cvt)
    o_s2d = o_s.transpose(1, 0, 2).reshape(n_s, D_ATTN)

    x1_s = _merge(o_s2d, ya_s, g2_s, xs, wb, wo, _row_tile(n_s, 512))
    pad_f = _ceil_to((FFN_CONV_W - 1) * nbs, SUBLANES)
    past_f = state_ffn_conv[0].transpose(1, 0, 2).reshape(1, (FFN_CONV_W - 1) * nbs, D_FF)
    past_f = jnp.pad(past_f, ((0, 0), (pad_f - (FFN_CONV_W - 1) * nbs, 0), (0, 0)))
    y_s, ftail_s = _ffn(x1_s, g_ffn, wup, fcw, fcb, past_f, wd, nb=1, nt=1, tm=n_s, stride=nbs, tf=FFN_TILE,
                        rows_real=n_s)

    y_sample = y_s.reshape(n_q, nbs, D_MODEL).transpose(1, 0, 2)
    s_cv = ctail_s[0, pad_c - (CONV_W - 1) * nbs:].reshape(CONV_W - 1, nbs, D_CONV).transpose(1, 0, 2)[None]
    s_fc = ftail_s[0, pad_f - (FFN_CONV_W - 1) * nbs:].reshape(FFN_CONV_W - 1, nbs, D_FF).transpose(1, 0, 2)[None]

    return (y_prompt, y_sample, p_k, p_v, p_ik, p_cv, p_fc,
            k_s[None], v_s[None], ki_s[None], s_cv, s_fc)
```

```python
import functools
import math

import jax
import jax.numpy as jnp
from jax import lax
from jax.experimental import pallas as pl
from jax.experimental.pallas import tpu as pltpu

F32 = jnp.float32
BF16 = jnp.bfloat16
I32 = jnp.int32

D_MODEL = 1024
D_CONV = D_MODEL // 2
CONV_W = 31
N_HEADS = 8
HEAD_DIM = 64
D_ATTN = N_HEADS * HEAD_DIM
IDX_HEADS = 8
IDX_DIM = 64
TOP_K_MAX = 256
N_BUCKETS = 32
MAX_EXACT = N_BUCKETS // 2
REL_MAX_DIST = 128
D_FF = 2816
FFN_CONV_W = 3
N_META = 16
PAGE_SIZE = 128
FFN_TILE = D_FF // 2
SEARCH_HEAD_BITS = 3
SEARCH_GROUP = 4
EPS = 1e-6
ATTN_SCALE = HEAD_DIM ** -0.5
IDX_SCALE = (IDX_HEADS * IDX_DIM) ** -0.5
D_MAIN = 2 * D_CONV + 4 * D_ATTN + 2 * D_MODEL
JB_Q, JB_K, JB_V, JB_QI, JB_G = 2, 3, 4, 5, 6

LANES = 128
SUBLANES = 8
KEY_INVALID = -2 ** 31
NEG_BIG = -1e30
VMEM_LIMIT = 56 * 1024 * 1024
NT_DIMS = (((1,), (1,)), ((), ()))


def _cparams(sem):
    return pltpu.CompilerParams(dimension_semantics=sem, vmem_limit_bytes=VMEM_LIMIT)


def _sigmoid(x):
    return 1.0 / (1.0 + jnp.exp(-x))


def _ceil_to(x, m):
    return -(-x // m) * m


def _col_reduce(x, op, reduce_fn):
    n = x.shape[0]
    while n > SUBLANES and n % (2 * SUBLANES) == 0:
        n //= 2
        x = op(x[:n], x[n:2 * n])
    return reduce_fn(x, axis=0, keepdims=True)


def _split_bf16(x):
    hi = x.astype(BF16)
    return hi, (x - hi.astype(F32)).astype(BF16)


def _in_proj_kernel(x_ref, g_ref, wm_ref, wt_ref, ws_ref, wst_ref, gq_ref, gk_ref, seg_ref,
                    a_ref, k_ref, kb_ref, v_ref, qt_ref, vt_ref, qit_ref, g2_ref, ki_ref, kib_ref, wit_ref,
                    xn_ref):
    j = pl.program_id(1)

    @pl.when(j == 0)
    def _():
        x = x_ref[...]
        ms = jnp.mean(x * x, axis=-1, keepdims=True)
        xn_ref[...] = (x * lax.rsqrt(ms + EPS) * g_ref[...]).astype(BF16)
        ki = jnp.dot(xn_ref[...], ws_ref[...], preferred_element_type=F32)[:, :IDX_DIM]
        ki_ref[...] = ki
        kib_ref[...] = ki.astype(BF16)
        wit_ref[...] = lax.dot_general(wst_ref[...], xn_ref[...], NT_DIMS, preferred_element_type=F32)

    @pl.when(j < JB_Q)
    def _():
        a_ref[...] = jnp.dot(xn_ref[...], wm_ref[...], preferred_element_type=F32)

    @pl.when(j >= JB_G)
    def _():
        g2_ref[...] = jnp.dot(xn_ref[...], wm_ref[...], preferred_element_type=F32).astype(BF16)

    @pl.when(j == JB_Q)
    def _():
        yt = lax.dot_general(wt_ref[0], xn_ref[...], NT_DIMS, preferred_element_type=F32)
        hi, lo = _split_bf16(yt * yt)
        seg = seg_ref[...]
        ms = jnp.concatenate(
            [jnp.dot(seg, hi[r:r + LANES], preferred_element_type=F32)
             + jnp.dot(seg, lo[r:r + LANES], preferred_element_type=F32) for r in range(0, D_ATTN, LANES)],
            axis=0)
        qt_ref[...] = (yt * lax.rsqrt(ms + EPS) * gq_ref[...]).astype(BF16)

    @pl.when(j == JB_K)
    def _():
        y = jnp.dot(xn_ref[...], wm_ref[...], preferred_element_type=F32)
        hi, lo = _split_bf16(y * y)
        seg = seg_ref[...]
        ms = jnp.concatenate(
            [jnp.dot(hi[:, c:c + LANES], seg, preferred_element_type=F32)
             + jnp.dot(lo[:, c:c + LANES], seg, preferred_element_type=F32) for c in range(0, D_ATTN, LANES)],
            axis=1)
        k = y * lax.rsqrt(ms + EPS) * gk_ref[...]
        k_ref[...] = k
        kb_ref[...] = k.astype(BF16)

    @pl.when(j == JB_V)
    def _():
        v_ref[...] = jnp.dot(xn_ref[...], wm_ref[...], preferred_element_type=F32)
        vt_ref[...] = lax.dot_general(wt_ref[1], xn_ref[...], NT_DIMS, preferred_element_type=F32).astype(BF16)

    @pl.when(j == JB_QI)
    def _():
        qit_ref[...] = lax.dot_general(wt_ref[2], xn_ref[...], NT_DIMS, preferred_element_type=F32).astype(BF16)


def _in_proj(x2d, g, wm, wt, ws, wst, gq, gk, seg, tm):
    n = x2d.shape[0]
    tn = D_ATTN
    nj = D_MAIN // tn
    row = lambda i, j: (i, 0)
    colt = lambda i, j: (0, i)
    const2 = lambda i, j: (0, 0)
    out_shape = [
        jax.ShapeDtypeStruct((n, 2 * D_CONV), F32),
        jax.ShapeDtypeStruct((n, D_ATTN), F32),
        jax.ShapeDtypeStruct((n, D_ATTN), BF16),
        jax.ShapeDtypeStruct((n, D_ATTN), F32),
        jax.ShapeDtypeStruct((D_ATTN, n), BF16),
        jax.ShapeDtypeStruct((D_ATTN, n), BF16),
        jax.ShapeDtypeStruct((D_ATTN, n), BF16),
        jax.ShapeDtypeStruct((n, 2 * D_MODEL), BF16),
        jax.ShapeDtypeStruct((n, IDX_DIM), F32),
        jax.ShapeDtypeStruct((n, IDX_DIM), BF16),
        jax.ShapeDtypeStruct((IDX_HEADS, n), F32),
    ]
    out_specs = [
        pl.BlockSpec((tm, tn), lambda i, j: (i, jnp.minimum(j, 1))),
        pl.BlockSpec((tm, tn), row),
        pl.BlockSpec((tm, tn), row),
        pl.BlockSpec((tm, tn), row),
        pl.BlockSpec((tn, tm), colt),
        pl.BlockSpec((tn, tm), colt),
        pl.BlockSpec((tn, tm), colt),
        pl.BlockSpec((tm, tn), lambda i, j: (i, jnp.clip(j - JB_G, 0, nj - JB_G - 1))),
        pl.BlockSpec((tm, IDX_DIM), row),
        pl.BlockSpec((tm, IDX_DIM), row),
        pl.BlockSpec((IDX_HEADS, tm), colt),
    ]
    return pl.pallas_call(
        _in_proj_kernel,
        grid=(n // tm, nj),
        in_specs=[
            pl.BlockSpec((tm, D_MODEL), row),
            pl.BlockSpec((1, D_MODEL), const2),
            pl.BlockSpec((D_MODEL, tn), lambda i, j: (0, j)),
            pl.BlockSpec((3, tn, D_MODEL), lambda i, j: (0, 0, 0)),
            pl.BlockSpec((D_MODEL, LANES), const2),
            pl.BlockSpec((IDX_HEADS, D_MODEL), const2),
            pl.BlockSpec((D_ATTN, 1), const2),
            pl.BlockSpec((1, D_ATTN), const2),
            pl.BlockSpec((LANES, LANES), const2),
        ],
        out_specs=out_specs,
        out_shape=out_shape,
        scratch_shapes=[pltpu.VMEM((tm, D_MODEL), BF16)],
        compiler_params=_cparams(("parallel", "arbitrary")),
        name="in_proj",
    )(x2d, g, wm, wt, ws, wst, gq, gk, seg)


def _conv_kernel(a_ref, past_ref, cw_ref, cb_ref, lg_ref, lb_ref, wa_ref, ya_ref, tail_ref, ext_ref, h_ref,
                 *, stride, tm, pad, rc, tail_tile, tail_loc):
    t = pl.program_id(1)

    @pl.when(t == 0)
    def _():
        ext_ref[0, 0:pad, :] = past_ref[0]

    @pl.when(t > 0)
    def _():
        ext_ref[0, 0:pad, :] = ext_ref[0, tm:tm + pad, :]

    a = a_ref[...]
    ext_ref[0, pad:pad + tm, :] = a[:, :D_CONV] * _sigmoid(a[:, D_CONV:])

    @pl.when(t <= tail_tile)
    def _():
        tail_ref[0] = ext_ref[0, tail_loc:tail_loc + pad, :]

    n_shift = ext_ref.shape[0]
    length = pad + tm
    for r in range(1, n_shift):
        ext_ref[r, 0:length - SUBLANES, :] = ext_ref[0, r:r + length - SUBLANES, :]

    cb = cb_ref[...]
    lg = lg_ref[...]
    lb = lb_ref[...]
    for r0 in range(0, tm, rc):
        acc = jnp.zeros((rc, D_CONV), F32) + cb
        for w in range(CONV_W):
            off = pad - (CONV_W - 1 - w) * stride + r0
            r = off % n_shift
            acc = acc + ext_ref[r, off - r:off - r + rc, :] * cw_ref[w:w + 1, :]
        mu = jnp.mean(acc, axis=-1, keepdims=True)
        d = acc - mu
        var = jnp.mean(d * d, axis=-1, keepdims=True)
        h = d * lax.rsqrt(var + EPS) * lg + lb
        h_ref[r0:r0 + rc, :] = (h * _sigmoid(h)).astype(BF16)
    ya_ref[...] = jnp.dot(h_ref[...], wa_ref[...], preferred_element_type=F32).astype(BF16)


def _tail_position(rows_real, tm):
    tail_tile = (rows_real - 1) // tm
    return tail_tile, rows_real - tail_tile * tm


def _conv_branch(a_in, past, cw, cb, lg, lb, wa, *, nb, nt, tm, stride, rows_real):
    pad = past.shape[1]
    rc = 32 if tm % 32 == 0 else SUBLANES
    tail_tile, tail_loc = _tail_position(rows_real, tm)
    kern = functools.partial(_conv_kernel, stride=stride, tm=tm, pad=pad, rc=rc,
                             tail_tile=tail_tile, tail_loc=tail_loc)
    return pl.pallas_call(
        kern,
        grid=(nb, nt),
        in_specs=[
            pl.BlockSpec((tm, 2 * D_CONV), lambda b, t: (b * nt + t, 0)),
            pl.BlockSpec((1, pad, D_CONV), lambda b, t: (b, 0, 0)),
            pl.BlockSpec((32, D_CONV), lambda b, t: (0, 0)),
            pl.BlockSpec((1, D_CONV), lambda b, t: (0, 0)),
            pl.BlockSpec((1, D_CONV), lambda b, t: (0, 0)),
            pl.BlockSpec((1, D_CONV), lambda b, t: (0, 0)),
            pl.BlockSpec((D_CONV, D_MODEL), lambda b, t: (0, 0)),
        ],
        out_specs=[
            pl.BlockSpec((tm, D_MODEL), lambda b, t: (b * nt + t, 0)),
            pl.BlockSpec((1, pad, D_CONV), lambda b, t: (b, 0, 0)),
        ],
        out_shape=[jax.ShapeDtypeStruct((nb * nt * tm, D_MODEL), BF16),
                   jax.ShapeDtypeStruct((nb, pad, D_CONV), F32)],
        scratch_shapes=[pltpu.VMEM((1 if stride % SUBLANES == 0 else SUBLANES, pad + tm, D_CONV), F32),
                        pltpu.VMEM((tm, D_CONV), BF16)],
        compiler_params=_cparams(("parallel", "arbitrary")),
        name="conv_branch",
    )(a_in, past, cw, cb, lg, lb, wa)


def _rel_bucket(rel):
    n = jnp.maximum(rel, 0)
    nf = jnp.maximum(n, 1).astype(F32)
    large = MAX_EXACT + (jnp.log(nf / MAX_EXACT) / math.log(REL_MAX_DIST / MAX_EXACT)
                         * (N_BUCKETS - MAX_EXACT)).astype(I32)
    large = jnp.minimum(large, N_BUCKETS - 1)
    return jnp.where(n < MAX_EXACT, n, large)


def _bias_of(rel, rb_rows):
    bucket = _rel_bucket(rel)
    out = jnp.zeros(rel.shape, F32)
    for b in range(N_BUCKETS):
        out = jnp.where(bucket == b, rb_rows[:, b:b + 1], out)
    return out


def _prompt_bias_kernel(rbt_ref, b1_ref):
    key = lax.broadcasted_iota(I32, (LANES, LANES), 0)
    qry = lax.broadcasted_iota(I32, (LANES, LANES), 1)
    for h in range(N_HEADS):
        for d in range(3):
            b1_ref[h * 3 + d] = _bias_of(d * LANES + qry - key, rbt_ref[h:h + 1, :])


def _prompt_bias(rbt):
    return pl.pallas_call(
        _prompt_bias_kernel,
        out_shape=jax.ShapeDtypeStruct((N_HEADS * 3, LANES, LANES), F32),
        name="prompt_bias",
    )(rbt)


def _score_to_key(s, valid):
    bits = pltpu.bitcast(s, I32)
    key = bits ^ ((bits >> 31) & 0x7FFFFFFF)
    key = jnp.where(bits == KEY_INVALID, 0, key)
    return jnp.where(valid, key, KEY_INVALID)


def _threshold_search(count_ge, n_keep, shape):
    cnt_valid = count_ge(jnp.full(shape, KEY_INVALID + 1, I32))
    c0 = count_ge(jnp.zeros(shape, I32))
    ok0 = c0 >= n_keep
    base = jnp.where(ok0, 0, KEY_INVALID).astype(I32)
    cnt = jnp.where(ok0, c0, cnt_valid)

    def step(t, base, cnt):
        cand = base | lax.shift_left(jnp.int32(1), 30 - t)
        c = count_ge(cand)
        ok = c >= n_keep
        return jnp.where(ok, cand, base), jnp.where(ok, c, cnt)

    for t in range(SEARCH_HEAD_BITS):
        base, cnt = step(t, base, cnt)

    def cond(state):
        t, _, _, unsettled = state
        return jnp.logical_and(t < 31, unsettled)

    def body(state):
        t, base, cnt, _ = state
        for u in range(SEARCH_GROUP):
            base, cnt = step(t + u, base, cnt)
        return t + SEARCH_GROUP, base, cnt, jnp.max(cnt) > n_keep

    _, base, cnt, _ = lax.while_loop(cond, body, (jnp.int32(SEARCH_HEAD_BITS), base, cnt, jnp.max(cnt) > n_keep))
    return jnp.maximum(base, KEY_INVALID + 1), cnt


def _pattn_kernel(qt_ref, k_ref, vt_ref, qit_ref, ki_ref, wit_ref, b1_ref, tril_ref, o_ref,
                  keys_ref, qiw_ref, qbd_ref, m_ref, l_ref, acc_ref, ot_ref, p_ref, alpha_ref, sa_ref, sb_ref, ia_ref, ib_ref, *, n_keep):
    i = pl.program_id(1)
    tq = LANES
    row1 = (1, tq)
    lead = (i + 1) % 2
    npair = (i + 1) // 2
    qpos = i * tq + lax.broadcasted_iota(I32, row1, 1)

    for h in range(IDX_HEADS):
        qiw_ref[:, h * tq:(h + 1) * tq] = qit_ref[h * IDX_DIM:(h + 1) * IDX_DIM, :]
    qbd_ref[...] = jnp.zeros(qbd_ref.shape, BF16)
    for h in range(N_HEADS):
        hp, e = divmod(h, 2)
        qbd_ref[hp, e * HEAD_DIM:(e + 1) * HEAD_DIM, e * tq:(e + 1) * tq] = qt_ref[h * HEAD_DIM:(h + 1) * HEAD_DIM, :]
    wit = wit_ref[...] * IDX_SCALE

    blk = (LANES, tq)
    nblk = i + 1

    ck = 2 * LANES
    last_off = jnp.maximum(nblk * LANES - ck, 0)

    def chunk_off(c):
        return pl.multiple_of(jnp.minimum(c * ck, last_off), LANES)

    def idx_scores_into(dst, c):
        dst[...] = jnp.dot(ki_ref[pl.ds(chunk_off(c), ck), :], qiw_ref[...], preferred_element_type=F32)

    def idx_keys_from(src, c):
        off = chunk_off(c)
        s = jnp.zeros((ck, tq), F32)
        for h in range(IDX_HEADS):
            s = s + wit[h:h + 1, :] * jnp.maximum(src[:, h * tq:(h + 1) * tq], 0.0)
        kpos = off + lax.broadcasted_iota(I32, (ck, tq), 0)
        keys_ref[pl.ds(off, ck), :] = _score_to_key(s, kpos <= qpos)

    idx_scores_into(ia_ref, 0)

    def idx_body(t, carry):
        idx_scores_into(ib_ref, 2 * t + 1)
        idx_keys_from(ia_ref, 2 * t)
        idx_scores_into(ia_ref, 2 * t + 2)
        idx_keys_from(ib_ref, 2 * t + 1)
        return carry

    lax.fori_loop(0, (nblk + 3) // 4, idx_body, 0)

    def count_ge(cand):
        cb = jnp.broadcast_to(cand, blk)

        def hit(off):
            return jnp.where(keys_ref[pl.ds(off, LANES), :] >= cb, 1.0, 0.0)

        def body(j, acc):
            off = pl.multiple_of(lead * LANES + j * 2 * LANES, LANES)
            return acc + hit(off) + hit(off + LANES)
        acc = lax.fori_loop(0, npair, body, jnp.where(lead == 1, hit(0), 0.0))
        return _col_reduce(acc, jnp.add, jnp.sum)

    base, cnt = _threshold_search(count_ge, float(n_keep), row1)

    @pl.when(jnp.max(cnt) > float(n_keep))
    def _():
        need = float(n_keep) - count_ge(base + 1)
        tie_q = cnt > float(n_keep)

        def body(j, seen):
            off = pl.multiple_of(j * LANES, LANES)
            k = keys_ref[pl.ds(off, LANES), :]
            eq = jnp.logical_and(k == base, tie_q)
            eqf = jnp.where(eq, 1.0, 0.0)
            pref = seen + jnp.dot(tril_ref[...], eqf.astype(BF16), preferred_element_type=F32)
            keys_ref[pl.ds(off, LANES), :] = jnp.where(jnp.logical_and(eq, pref > need), KEY_INVALID, k)
            return seen + jnp.sum(eqf, axis=0, keepdims=True)
        lax.fori_loop(0, nblk, body, jnp.zeros(row1, F32))

    m_ref[...] = jnp.full(m_ref.shape, NEG_BIG, F32)
    l_ref[...] = jnp.zeros(l_ref.shape, F32)
    acc_ref[...] = jnp.zeros(acc_ref.shape, F32)
    p_ref[...] = jnp.zeros(p_ref.shape, BF16)
    alpha_ref[...] = jnp.ones(alpha_ref.shape, F32)
    base_b = jnp.broadcast_to(base, blk)

    def clamp_blk(jb):
        return jnp.clip(jb, 0, nblk - 1)

    def apply_pv(jb):
        off = pl.multiple_of(clamp_blk(jb) * LANES, LANES)
        for h in range(N_HEADS):
            pv = jnp.dot(vt_ref[h * HEAD_DIM:(h + 1) * HEAD_DIM, pl.ds(off, LANES)], p_ref[h],
                         preferred_element_type=F32)
            acc_ref[h] = alpha_ref[h] * acc_ref[h] + pv

    def qk_into(s_dst, jb):
        off = pl.multiple_of(clamp_blk(jb) * LANES, LANES)
        for hp in range(N_HEADS // 2):
            for e in range(2):
                s_dst[2 * hp + e] = jnp.dot(k_ref[pl.ds(off, LANES), hp * LANES:(hp + 1) * LANES],
                                            qbd_ref[hp, :, e * tq:(e + 1) * tq], preferred_element_type=F32)

    def softmax_from(s_src, jb):
        jc = clamp_blk(jb)
        off = pl.multiple_of(jc * LANES, LANES)
        sel = jnp.logical_and(keys_ref[pl.ds(off, LANES), :] >= base_b, jb < nblk)
        d = jnp.minimum(i - jc, 2)
        for h in range(N_HEADS):
            s = jnp.where(sel, s_src[h] + b1_ref[h * 3 + d], NEG_BIG)
            m_old = m_ref[h]
            m_new = jnp.maximum(m_old, _col_reduce(s, jnp.maximum, jnp.max))
            p = jnp.exp(s - m_new)
            alpha = jnp.exp(m_old - m_new)
            l_ref[h] = alpha * l_ref[h] + _col_reduce(p, jnp.add, jnp.sum)
            p_ref[h] = p.astype(BF16)
            alpha_ref[h] = alpha
            m_ref[h] = m_new

    def half_trip(jb, s_cur, s_nxt):
        apply_pv(jb - 1)
        qk_into(s_nxt, jb + 1)
        softmax_from(s_cur, jb)

    qk_into(sa_ref, 0)

    def att_body(t, carry):
        half_trip(2 * t, sa_ref, sb_ref)
        half_trip(2 * t + 1, sb_ref, sa_ref)
        return carry

    n_trips = (nblk + 1) // 2
    lax.fori_loop(0, n_trips, att_body, 0)
    apply_pv(2 * n_trips - 1)
    for h in range(N_HEADS):
        ot_ref[h * HEAD_DIM:(h + 1) * HEAD_DIM, :] = acc_ref[h] / l_ref[h]
    o_ref[...] = ot_ref[...].T


def _prompt_attention(qt, kb, vt, qit, kib, wit, b1, tril, n_keep, nb, tp):
    tq = LANES
    nq = tp // tq
    n = nb * tp
    qcol = lambda b, i: (0, b * nq + i)
    return pl.pallas_call(
        functools.partial(_pattn_kernel, n_keep=n_keep),
        grid=(nb, nq),
        in_specs=[
            pl.BlockSpec((D_ATTN, tq), qcol),
            pl.BlockSpec((tp, D_ATTN), lambda b, i: (b, 0)),
            pl.BlockSpec((D_ATTN, tp), lambda b, i: (0, b)),
            pl.BlockSpec((IDX_HEADS * IDX_DIM, tq), qcol),
            pl.BlockSpec((tp, IDX_DIM), lambda b, i: (b, 0)),
            pl.BlockSpec((IDX_HEADS, tq), qcol),
            pl.BlockSpec((N_HEADS * 3, LANES, LANES), lambda b, i: (0, 0, 0)),
            pl.BlockSpec((LANES, LANES), lambda b, i: (0, 0)),
        ],
        out_specs=pl.BlockSpec((tq, D_ATTN), lambda b, i: (b * nq + i, 0)),
        out_shape=jax.ShapeDtypeStruct((n, D_ATTN), F32),
        scratch_shapes=[
            pltpu.VMEM((tp, tq), I32),
            pltpu.VMEM((IDX_DIM, IDX_HEADS * tq), BF16),
            pltpu.VMEM((N_HEADS // 2, LANES, 2 * tq), BF16),
            pltpu.VMEM((N_HEADS, 1, tq), F32),
            pltpu.VMEM((N_HEADS, 1, tq), F32),
            pltpu.VMEM((N_HEADS, HEAD_DIM, tq), F32),
            pltpu.VMEM((D_ATTN, tq), F32),
            pltpu.VMEM((N_HEADS, LANES, tq), BF16),
            pltpu.VMEM((N_HEADS, 1, tq), F32),
            pltpu.VMEM((N_HEADS, LANES, tq), F32),
            pltpu.VMEM((N_HEADS, LANES, tq), F32),
            pltpu.VMEM((2 * LANES, IDX_HEADS * tq), F32),
            pltpu.VMEM((2 * LANES, IDX_HEADS * tq), F32),
        ],
        compiler_params=_cparams(("parallel", "arbitrary")),
        name="prompt_attention",
    )(qt, kb, vt, qit, kib, wit, b1, tril)


def _sidx_t_kernel(pt_ref, qi_ref, wi_ref, kin_ref, tri_ref, lin_ref, cache_ref, sel_ref,
                   kibuf, sem, keys_ref, *, n_keep, n_pages, n_q, ppc, ncs, ncp, gb):
    b = pl.program_id(0)
    nb = pl.num_programs(0)
    slot = b % 2

    def page_copy(step, g, p, sl):
        return pltpu.make_async_copy(cache_ref.at[pt_ref[step * gb + g, p]], kibuf.at[sl, g * n_pages + p],
                                     sem.at[sl])

    def start_all(step, sl):
        for g in range(gb):
            def body(p, c):
                page_copy(step, g, p, sl).start()
                return c
            lax.fori_loop(0, n_pages, body, 0, unroll=ppc)

    def wait_all(step, sl):
        for g in range(gb):
            def body(p, c):
                page_copy(step, g, p, sl).wait()
                return c
            lax.fori_loop(0, n_pages, body, 0, unroll=ppc)

    @pl.when(b == 0)
    def _():
        start_all(b, slot)

    @pl.when(b + 1 < nb)
    def _():
        start_all(b + 1, 1 - slot)

    wait_all(b, slot)
    keys_ref[...] = jnp.full(keys_ref.shape, KEY_INVALID, I32)
    qrow = lax.broadcasted_iota(I32, (n_q, LANES), 0)
    jcol = lax.broadcasted_iota(I32, (n_q, LANES), 1)

    for g in range(gb):
        qi = qi_ref[g]
        w = wi_ref[g] * IDX_SCALE

        def score_rows(s):
            s = jnp.maximum(s, 0.0) * w
            return jnp.sum(s.reshape(n_q, IDX_HEADS, s.shape[-1]), axis=1)

        def chunk_body(c, carry):
            p0 = pl.multiple_of(c * ppc, ppc)
            kt = jnp.concatenate([kibuf[slot, g * n_pages + p0 + p] for p in range(ppc)],
                                 axis=1).astype(BF16)
            s = score_rows(jnp.dot(qi, kt, preferred_element_type=F32))
            key = _score_to_key(s, jnp.full(s.shape, True))
            for q in range(n_q):
                for p in range(ppc):
                    keys_ref[g * n_q + q, pl.ds(p0 + p, 1), :] = key[q:q + 1, p * LANES:(p + 1) * LANES]
            return carry

        lax.fori_loop(0, n_pages // ppc, chunk_body, 0)
        s_new = score_rows(lax.dot_general(qi, kin_ref[g], NT_DIMS, preferred_element_type=F32))
        key_new = _score_to_key(s_new, jcol <= qrow)
        for q in range(n_q):
            keys_ref[g * n_q + q, n_pages:n_pages + 1, :] = key_new[q:q + 1, :]

    shape = (gb * n_q, 1, LANES)

    def count_ge(cand):
        hit = jnp.where(keys_ref[:, 0:ncs, :] >= cand, 1.0, 0.0)
        part = jnp.sum(hit, axis=1, keepdims=True)
        return jnp.broadcast_to(jnp.sum(part, axis=2, keepdims=True), shape)

    base, cnt = _threshold_search(count_ge, float(n_keep), shape)

    @pl.when(jnp.max(cnt) > float(n_keep))
    def _():
        need = float(n_keep) - count_ge(base + 1)
        ones = jnp.ones((LANES, LANES), BF16)
        for q in range(gb * n_q):
            k = keys_ref[q]
            eq = jnp.logical_and(k == base[q], cnt[q] > float(n_keep))
            eqb = jnp.where(eq, 1.0, 0.0).astype(BF16)
            within = jnp.dot(eqb, tri_ref[...], preferred_element_type=F32)
            rowcnt = jnp.dot(eqb, ones, preferred_element_type=F32).astype(BF16)
            before = jnp.dot(lin_ref[...], rowcnt, preferred_element_type=F32) - rowcnt.astype(F32)
            drop = jnp.logical_and(eq, within + before > need[q])
            keys_ref[q] = jnp.where(drop, KEY_INVALID, k)

    sel = jnp.where(keys_ref[:, 0:ncs, :] >= base, 1.0, 0.0)
    for g in range(gb):
        sel_ref[g] = sel[g * n_q:(g + 1) * n_q]


def _sample_index_t(page_table, qi_rows, wi_rows, ki_new, tri, cache_ikt, n_keep):
    nb_all, n_pages = page_table.shape
    gb = 4 if nb_all % 4 == 0 else (2 if nb_all % 2 == 0 else 1)
    nb = nb_all // gb
    rows = qi_rows.shape[1]
    n_q = rows // IDX_HEADS
    ppc = 8 if n_pages % 8 == 0 else 1
    ncs = _ceil_to(n_pages + 1, SUBLANES)
    ncp = _ceil_to(n_pages + 1, LANES)
    assert ncp <= 256, "tie-break prefix counts are carried in bf16, exact up to 256"
    ar = jnp.arange(ncp)
    lin = (ar[None, :] <= ar[:, None]).astype(BF16)
    kern = functools.partial(_sidx_t_kernel, n_keep=n_keep, n_pages=n_pages, n_q=n_q, ppc=ppc,
                             ncs=ncs, ncp=ncp, gb=gb)
    return pl.pallas_call(
        kern,
        grid_spec=pltpu.PrefetchScalarGridSpec(
            num_scalar_prefetch=1,
            grid=(nb,),
            in_specs=[
                pl.BlockSpec((gb, rows, IDX_DIM), lambda b, pt: (b, 0, 0)),
                pl.BlockSpec((gb, rows, 1), lambda b, pt: (b, 0, 0)),
                pl.BlockSpec((gb, LANES, IDX_DIM), lambda b, pt: (b, 0, 0)),
                pl.BlockSpec((LANES, LANES), lambda b, pt: (0, 0)),
                pl.BlockSpec((ncp, ncp), lambda b, pt: (0, 0)),
                pl.BlockSpec(memory_space=pl.ANY),
            ],
            out_specs=pl.BlockSpec((gb, n_q, ncs, LANES), lambda b, pt: (b, 0, 0, 0)),
            scratch_shapes=[
                pltpu.VMEM((2, gb * n_pages, IDX_DIM, PAGE_SIZE), F32),
                pltpu.SemaphoreType.DMA((2,)),
                pltpu.VMEM((gb * n_q, ncp, LANES), I32),
            ],
        ),
        out_shape=jax.ShapeDtypeStruct((nb_all, n_q, ncs, LANES), F32),
        compiler_params=_cparams(("arbitrary",)),
        name="sample_index",
    )(page_table, qi_rows, wi_rows, ki_new, tri, lin, cache_ikt)


def _sdense_kernel(pt_ref, qbd_ref, sel_ref, seln_ref, kn_ref, vn_ref, bias_ref, hsel_ref, ck_ref, cv_ref, o_ref,
                   kbuf, vbuf, sem, bias_buf, m_ref, l_ref, acc_ref, *, n_q, n_pages, ppc):
    b = pl.program_id(0)
    c = pl.program_id(1)
    nb = pl.num_programs(0)
    n_chunks = n_pages // ppc
    step = b * n_chunks + c
    slot = step % 2
    rows = n_q * N_HEADS
    ck = ppc * PAGE_SIZE

    def copies(bb, cc, sl, p):
        page = pt_ref[bb, cc * ppc + p]
        return (pltpu.make_async_copy(ck_ref.at[page], kbuf.at[sl, p], sem.at[0, sl]),
                pltpu.make_async_copy(cv_ref.at[page], vbuf.at[sl, p], sem.at[1, sl]))

    def start_all(bb, cc, sl):
        for p in range(ppc):
            kc, vc = copies(bb, cc, sl, p)
            kc.start()
            vc.start()

    def wait_all(bb, cc, sl):
        for p in range(ppc):
            kc, vc = copies(bb, cc, sl, p)
            kc.wait()
            vc.wait()

    @pl.when(step == 0)
    def _():
        start_all(b, c, slot)

    @pl.when(step + 1 < nb * n_chunks)
    def _():
        nxt = step + 1
        start_all(nxt // n_chunks, nxt % n_chunks, 1 - slot)

    @pl.when(c == 0)
    def _():
        m_ref[...] = jnp.full(m_ref.shape, NEG_BIG, F32)
        l_ref[...] = jnp.zeros(l_ref.shape, F32)
        acc_ref[...] = jnp.zeros(acc_ref.shape, F32)
        bias_buf[...] = jnp.broadcast_to(bias_ref[2][:, 0:1], bias_buf.shape)

    @pl.when(c == n_chunks - 1)
    def _():
        bias_buf[:, ck - PAGE_SIZE:ck] = bias_ref[0]

    wait_all(b, c, slot)
    qbd = qbd_ref[0]

    def expand(sel):
        n = sel.shape[-1]
        return jnp.broadcast_to(sel[:, None, :], (n_q, N_HEADS, n)).reshape(rows, n)

    def update(s, live, pv_fn):
        s = jnp.where(live, s, NEG_BIG)
        m_old = m_ref[...]
        m_new = jnp.maximum(m_old, jnp.max(s, axis=1, keepdims=True))
        p = jnp.exp(s - m_new)
        alpha = jnp.exp(m_old - m_new)
        l_ref[...] = alpha * l_ref[...] + jnp.sum(p, axis=1, keepdims=True)
        acc_ref[...] = alpha * acc_ref[...] + pv_fn(p.astype(BF16))
        m_ref[...] = m_new

    s = jnp.concatenate([jnp.dot(qbd, kbuf[slot, p].astype(BF16), preferred_element_type=F32)
                         for p in range(ppc)], axis=1) + bias_buf[...]
    live = jnp.concatenate([expand(sel_ref[0, :, p, :]) for p in range(ppc)], axis=1) > 0.5

    def pv_pages(pb):
        out = jnp.zeros((rows, D_ATTN), F32)
        for p in range(ppc):
            out = out + lax.dot_general(pb[:, p * PAGE_SIZE:(p + 1) * PAGE_SIZE], vbuf[slot, p].astype(BF16),
                                        NT_DIMS, preferred_element_type=F32)
        return out

    update(s, live, pv_pages)

    @pl.when(c == n_chunks - 1)
    def _():
        s_new = lax.dot_general(qbd, kn_ref[0], NT_DIMS, preferred_element_type=F32) + bias_ref[1]
        update(s_new, expand(seln_ref[0, :, 0, :]) > 0.5,
               lambda pb: jnp.dot(pb, vn_ref[0], preferred_element_type=F32))
        full = acc_ref[...] / l_ref[...] * hsel_ref[...]
        o_ref[0] = jnp.sum(full.reshape(n_q, N_HEADS, D_ATTN), axis=1)


def _sample_attention_dense(page_table, qbd, sel, k_new, v_new, sbias, hsel, ckt, cvt):
    nb, n_pages = page_table.shape
    rows = qbd.shape[1]
    n_q = rows // N_HEADS
    ppc = 16
    assert n_pages % ppc == 0, "cached pages are attended in chunks of 16"
    n_chunks = n_pages // ppc
    ck = ppc * PAGE_SIZE
    kern = functools.partial(_sdense_kernel, n_q=n_q, n_pages=n_pages, ppc=ppc)
    return pl.pallas_call(
        kern,
        grid_spec=pltpu.PrefetchScalarGridSpec(
            num_scalar_prefetch=1,
            grid=(nb, n_chunks),
            in_specs=[
                pl.BlockSpec((1, rows, D_ATTN), lambda b, c, pt: (b, 0, 0)),
                pl.BlockSpec((1, n_q, ppc, LANES), lambda b, c, pt: (b, 0, c, 0)),
                pl.BlockSpec((1, n_q, SUBLANES, LANES), lambda b, c, pt: (b, 0, n_pages // SUBLANES, 0)),
                pl.BlockSpec((1, LANES, D_ATTN), lambda b, c, pt: (b, 0, 0)),
                pl.BlockSpec((1, LANES, D_ATTN), lambda b, c, pt: (b, 0, 0)),
                pl.BlockSpec((3, rows, LANES), lambda b, c, pt: (0, 0, 0)),
                pl.BlockSpec((rows, D_ATTN), lambda b, c, pt: (0, 0)),
                pl.BlockSpec(memory_space=pl.ANY),
                pl.BlockSpec(memory_space=pl.ANY),
            ],
            out_specs=pl.BlockSpec((1, n_q, D_ATTN), lambda b, c, pt: (b, 0, 0)),
            scratch_shapes=[
                pltpu.VMEM((2, ppc, D_ATTN, PAGE_SIZE), F32),
                pltpu.VMEM((2, ppc, D_ATTN, PAGE_SIZE), F32),
                pltpu.SemaphoreType.DMA((2, 2)),
                pltpu.VMEM((rows, ck), F32),
                pltpu.VMEM((rows, 1), F32),
                pltpu.VMEM((rows, 1), F32),
                pltpu.VMEM((rows, D_ATTN), F32),
            ],
        ),
        out_shape=jax.ShapeDtypeStruct((nb, n_q, D_ATTN), F32),
        compiler_params=_cparams(("arbitrary", "arbitrary")),
        name="sample_attention",
    )(page_table, qbd, sel, sel, k_new, v_new, sbias, hsel, ckt, cvt)


def _sample_bias_kernel(rbt_ref, o_ref, *, n_q):
    rows = n_q * N_HEADS
    q = lax.broadcasted_iota(I32, (rows, LANES), 0) // N_HEADS
    col = lax.broadcasted_iota(I32, (rows, LANES), 1)
    rb = rbt_ref[...]
    o_ref[0] = _bias_of(PAGE_SIZE + q - col, rb)
    o_ref[1] = _bias_of(q - col, rb)
    o_ref[2] = _bias_of(jnp.full((rows, LANES), REL_MAX_DIST, I32), rb)


def _sample_bias(rbt_rows, n_q):
    rows = n_q * N_HEADS
    return pl.pallas_call(
        functools.partial(_sample_bias_kernel, n_q=n_q),
        out_shape=jax.ShapeDtypeStruct((3, rows, LANES), F32),
        name="sample_bias",
    )(rbt_rows)


def _merge_kernel(o_ref, ya_ref, ga_ref, gb_ref, x_ref, wb_ref, wo_ref, x1_ref):
    yb = jnp.dot(o_ref[...].astype(BF16), wb_ref[...], preferred_element_type=F32)
    m = (_sigmoid(ga_ref[...].astype(F32)) * ya_ref[...].astype(F32)
         + _sigmoid(gb_ref[...].astype(F32)) * yb)
    x1_ref[...] = x_ref[...] + jnp.dot(m.astype(BF16), wo_ref[...], preferred_element_type=F32)


def _merge(o2d, ya, gates, x2d, wb, wo, tm):
    n = x2d.shape[0]
    return pl.pallas_call(
        _merge_kernel,
        grid=(n // tm,),
        in_specs=[
            pl.BlockSpec((tm, D_ATTN), lambda i: (i, 0)),
            pl.BlockSpec((tm, D_MODEL), lambda i: (i, 0)),
            pl.BlockSpec((tm, D_MODEL), lambda i: (i, 0)),
            pl.BlockSpec((tm, D_MODEL), lambda i: (i, 1)),
            pl.BlockSpec((tm, D_MODEL), lambda i: (i, 0)),
            pl.BlockSpec((D_ATTN, D_MODEL), lambda i: (0, 0)),
            pl.BlockSpec((D_MODEL, D_MODEL), lambda i: (0, 0)),
        ],
        out_specs=pl.BlockSpec((tm, D_MODEL), lambda i: (i, 0)),
        out_shape=jax.ShapeDtypeStruct((n, D_MODEL), F32),
        compiler_params=_cparams(("parallel",)),
        name="merge_out_proj",
    )(o2d, ya, gates, gates, x2d, wb, wo)


def _ffn_kernel(x_ref, g_ref, wua_ref, wub_ref, fcw_ref, fcb_ref, past_ref, wd_ref, y_ref, tail_ref,
                xn_ref, carry_ref, ext_ref, acc_ref, *, stride, tm, pad, tail_loc, rs):
    t = pl.program_id(1)
    f = pl.program_id(2)
    nf = pl.num_programs(2)

    @pl.when(f == 0)
    def _():
        x = x_ref[...]
        ms = jnp.mean(x * x, axis=-1, keepdims=True)
        xn_ref[...] = (x * lax.rsqrt(ms + EPS) * g_ref[...]).astype(BF16)
        acc_ref[...] = jnp.zeros(acc_ref.shape, F32)

    @pl.when(t == 0)
    def _():
        ext_ref[0:pad, :] = past_ref[0]

    @pl.when(t > 0)
    def _():
        ext_ref[0:pad, :] = carry_ref[f]

    def up_proj(r0):
        xs = xn_ref[r0:r0 + rs, :]
        a = jnp.dot(xs, wua_ref[...], preferred_element_type=F32)
        ext_ref[pad + r0:pad + r0 + rs, :] = a
        return a, jnp.dot(xs, wub_ref[...], preferred_element_type=F32)

    starts = list(range(0, tm, rs))
    nxt = up_proj(starts[0])
    for k, r0 in enumerate(starts):
        a, bq = nxt
        if k + 1 < len(starts):
            nxt = up_proj(starts[k + 1])
        lo = pad + r0
        conv = (ext_ref[lo - 2 * stride:lo - 2 * stride + rs, :] * fcw_ref[0:1, :]
                + ext_ref[lo - stride:lo - stride + rs, :] * fcw_ref[1:2, :]
                + a * fcw_ref[2:3, :] + fcb_ref[...])
        h = conv * _sigmoid(conv) * bq
        acc_ref[r0:r0 + rs, :] += jnp.dot(h.astype(BF16), wd_ref[...], preferred_element_type=F32)

    carry_ref[f] = ext_ref[tm:tm + pad, :]
    tail_ref[0, 0] = ext_ref[tail_loc:tail_loc + pad, :]

    @pl.when(f == nf - 1)
    def _():
        y_ref[...] = x_ref[...] + acc_ref[...]


def _ffn(x1, g, wup, fcw, fcb, past, wd, *, nb, nt, tm, stride, tf, rows_real):
    pad = past.shape[1]
    nf = D_FF // tf
    tail_tile, tail_loc = _tail_position(rows_real, tm)
    rs = tm // 2 if tm % (2 * SUBLANES) == 0 and tm // 2 >= 2 * stride else tm
    kern = functools.partial(_ffn_kernel, stride=stride, tm=tm, pad=pad, tail_loc=tail_loc, rs=rs)
    y, tail = pl.pallas_call(
        kern,
        grid=(nb, nt, nf),
        in_specs=[
            pl.BlockSpec((tm, D_MODEL), lambda b, t, f: (b * nt + t, 0)),
            pl.BlockSpec((1, D_MODEL), lambda b, t, f: (0, 0)),
            pl.BlockSpec((D_MODEL, tf), lambda b, t, f: (0, f)),
            pl.BlockSpec((D_MODEL, tf), lambda b, t, f: (0, nf + f)),
            pl.BlockSpec((SUBLANES, tf), lambda b, t, f: (0, f)),
            pl.BlockSpec((1, tf), lambda b, t, f: (0, f)),
            pl.BlockSpec((1, pad, tf), lambda b, t, f: (b, 0, f)),
            pl.BlockSpec((tf, D_MODEL), lambda b, t, f: (f, 0)),
        ],
        out_specs=[
            pl.BlockSpec((tm, D_MODEL), lambda b, t, f: (b * nt + t, 0)),
            pl.BlockSpec((1, 1, pad, tf), lambda b, t, f: (b, t, 0, f)),
        ],
        out_shape=[jax.ShapeDtypeStruct((nb * nt * tm, D_MODEL), F32),
                   jax.ShapeDtypeStruct((nb, nt, pad, D_FF), F32)],
        scratch_shapes=[
            pltpu.VMEM((tm, D_MODEL), BF16),
            pltpu.VMEM((nf, pad, tf), F32),
            pltpu.VMEM((pad + tm, tf), F32),
            pltpu.VMEM((tm, D_MODEL), F32),
        ],
        compiler_params=_cparams(("parallel", "arbitrary", "arbitrary")),
        name="conv_ffn",
    )(x1, g, wup, wup, fcw, fcb, past, wd)
    return y, tail[:, tail_tile]


def _row_tile(n, cap):
    best = SUBLANES
    for cand in range(SUBLANES, cap + 1, SUBLANES):
        if n % cand == 0:
            best = cand
    return best


def kernel(x_prompt, x_sample, cache_k, cache_v, cache_idx_k, page_table, state_conv, state_ffn_conv, meta_tokens, g_attn_norm, w_in, conv_w, conv_b, conv_ln_g, conv_ln_b, w_a_out, q_norm_g, k_norm_g, rel_bias, w_b_out, w_o, g_ffn_norm, w_up, ffn_conv_w, ffn_conv_b, w_down):
    nbp, seq, _ = x_prompt.shape
    nbs, n_q, _ = x_sample.shape
    n_pages = page_table.shape[1]
    past_len = n_pages * PAGE_SIZE
    t_real = seq + N_META
    tp = _ceil_to(t_real, LANES)
    keep_p = min(TOP_K_MAX, t_real // 4)
    keep_s = min(TOP_K_MAX, (past_len + n_q) // 4)
    assert g_attn_norm.shape[0] == 1, "single trunk layer"

    w = w_in[0]
    c_q, c_v, c_qi, c_ki, c_wi, c_ga = 1024, 2048, 2560, 3072, 3136, 3144
    wm = jnp.concatenate([w[:, :c_ki], w[:, c_ga:]], axis=1).astype(BF16)
    wt = jnp.stack([w[:, c:c + D_ATTN].T for c in (c_q, c_v, c_qi)]).astype(BF16)
    ws = jnp.concatenate([w[:, c_ki:c_wi], jnp.zeros((D_MODEL, LANES - IDX_DIM), F32)], axis=1).astype(BF16)
    wst = w[:, c_wi:c_ga].T.astype(BF16)
    g_attn = g_attn_norm[0][None, :]
    gq = (jnp.tile(q_norm_g[0], N_HEADS) * ATTN_SCALE)[:, None]
    gk = jnp.tile(k_norm_g[0], N_HEADS)[None, :]
    hid = jnp.arange(D_ATTN) // HEAD_DIM
    seg = jnp.where(hid[:LANES, None] == hid[None, :LANES], 1.0 / HEAD_DIM, 0.0).astype(BF16)
    cw = jnp.concatenate([conv_w[0], jnp.zeros((32 - CONV_W, D_CONV), F32)], axis=0)
    cb, lg, lb = conv_b[0][None, :], conv_ln_g[0][None, :], conv_ln_b[0][None, :]
    wa = w_a_out[0].astype(BF16)
    wb = w_b_out[0].astype(BF16)
    wo = w_o[0].astype(BF16)
    g_ffn = g_ffn_norm[0][None, :]
    wup = w_up[0].astype(BF16)
    fcw = jnp.concatenate([ffn_conv_w[0], jnp.zeros((SUBLANES - FFN_CONV_W, D_FF), F32)], axis=0)
    fcb = ffn_conv_b[0][None, :]
    wd = w_down[0].astype(BF16)
    rbt = rel_bias.T
    ar = jnp.arange(LANES)
    tril = (ar[None, :] <= ar[:, None]).astype(BF16)
    tri = tril.T

    def project(x2d):
        return _in_proj(x2d, g_attn, wm, wt, ws, wst, gq, gk, seg, _row_tile(x2d.shape[0], 1024))

    xp = jnp.concatenate([jnp.broadcast_to(meta_tokens[None], (nbp, N_META, D_MODEL)), x_prompt,
                          jnp.zeros((nbp, tp - t_real, D_MODEL), F32)], axis=1).reshape(nbp * tp, D_MODEL)
    n_p = nbp * tp
    a_p, k_p, kb_p, v_p, qt_p, vt_p, qit_p, g2_p, ki_p, kib_p, wit_p = project(xp)

    tm_p = _row_tile(tp, 640)
    nt_p = tp // tm_p
    ya_p, ctail_p = _conv_branch(a_p, jnp.zeros((nbp, 32, D_CONV), F32), cw, cb, lg, lb, wa,
                                 nb=nbp, nt=nt_p, tm=tm_p, stride=1, rows_real=t_real)
    o_p = _prompt_attention(qt_p, kb_p, vt_p, qit_p, kib_p, wit_p, _prompt_bias(rbt), tril, keep_p, nbp, tp)
    x1_p = _merge(o_p, ya_p, g2_p, xp, wb, wo, _row_tile(n_p, 512))
    y_p, ftail_p = _ffn(x1_p, g_ffn, wup, fcw, fcb, jnp.zeros((nbp, SUBLANES, D_FF), F32), wd,
                        nb=nbp, nt=nt_p, tm=tm_p, stride=1, tf=FFN_TILE, rows_real=t_real)

    y_prompt = y_p.reshape(nbp, tp, D_MODEL)[:, N_META:t_real]
    p_k = k_p.reshape(nbp, tp, N_HEADS, HEAD_DIM)[None, :, :t_real]
    p_v = v_p.reshape(nbp, tp, N_HEADS, HEAD_DIM)[None, :, :t_real]
    p_ik = ki_p.reshape(nbp, tp, IDX_DIM)[None, :, :t_real]
    p_cv = ctail_p[None, :, 32 - (CONV_W - 1):]
    p_fc = ftail_p[None, :, SUBLANES - (FFN_CONV_W - 1):]

    n_s = nbs * n_q
    xs = x_sample.transpose(1, 0, 2).reshape(n_s, D_MODEL)
    a_s, k_s2, _, v_s2, qt_s, _, qit_s, g2_s, ki_s2, _, wit_s = project(xs)

    pad_c = _ceil_to((CONV_W - 1) * nbs, SUBLANES)
    past_c = state_conv[0].transpose(1, 0, 2).reshape(1, (CONV_W - 1) * nbs, D_CONV)
    past_c = jnp.pad(past_c, ((0, 0), (pad_c - (CONV_W - 1) * nbs, 0), (0, 0)))
    ya_s, ctail_s = _conv_branch(a_s, past_c, cw, cb, lg, lb, wa, nb=1, nt=1, tm=n_s, stride=nbs,
                                 rows_real=n_s)

    def batch_major(x2d, *tail):
        return x2d.reshape((n_q, nbs) + tail).transpose((1, 0) + tuple(range(2, 2 + len(tail))))

    q_s = batch_major(qt_s.T.astype(F32), N_HEADS, HEAD_DIM)
    k_s = batch_major(k_s2, N_HEADS, HEAD_DIM)
    v_s = batch_major(v_s2, N_HEADS, HEAD_DIM)
    qi_s = batch_major(qit_s.T, IDX_HEADS, IDX_DIM)
    ki_s = batch_major(ki_s2, IDX_DIM)
    wi_s = batch_major(wit_s.T, IDX_HEADS)
    rows = n_q * N_HEADS

    qi_rows = qi_s.reshape(nbs, rows, IDX_DIM)
    wi_rows = wi_s.reshape(nbs, rows, 1)
    ki_new = jnp.pad(ki_s, ((0, 0), (0, LANES - n_q), (0, 0))).astype(BF16)
    n_pool = cache_k.shape[1]
    ckt = cache_k.transpose(0, 1, 3, 4, 2).reshape(n_pool, D_ATTN, PAGE_SIZE)
    cvt = cache_v.transpose(0, 1, 3, 4, 2).reshape(n_pool, D_ATTN, PAGE_SIZE)
    cikt = cache_idx_k.transpose(0, 1, 3, 2).reshape(n_pool, IDX_DIM, PAGE_SIZE)
    sel = _sample_index_t(page_table, qi_rows, wi_rows, ki_new, tri, cikt, keep_s)

    eye = jnp.eye(N_HEADS, dtype=F32)
    qbd = (q_s[:, :, :, None, :] * eye[None, None, :, :, None]).reshape(nbs, rows, D_ATTN).astype(BF16)
    k_new = jnp.pad(k_s.reshape(nbs, n_q, D_ATTN), ((0, 0), (0, LANES - n_q), (0, 0))).astype(BF16)
    v_new = jnp.pad(v_s.reshape(nbs, n_q, D_ATTN), ((0, 0), (0, LANES - n_q), (0, 0))).astype(BF16)
    hsel = (jnp.arange(rows)[:, None] % N_HEADS == hid[None, :]).astype(F32)
    o_s = _sample_attention_dense(page_table, qbd, sel, k_new, v_new, _sample_bias(jnp.tile(rbt, (n_q, 1)), n_q),
                                  hsel, ckt, cvt)
    o_s2d = o_s.transpose(1, 0, 2).reshape(n_s, D_ATTN)

    x1_s = _merge(o_s2d, ya_s, g2_s, xs, wb, wo, _row_tile(n_s, 512))
    pad_f = _ceil_to((FFN_CONV_W - 1) * nbs, SUBLANES)
    past_f = state_ffn_conv[0].transpose(1, 0, 2).reshape(1, (FFN_CONV_W - 1) * nbs, D_FF)
    past_f = jnp.pad(past_f, ((0, 0), (pad_f - (FFN_CONV_W - 1) * nbs, 0), (0, 0)))
    y_s, ftail_s = _ffn(x1_s, g_ffn, wup, fcw, fcb, past_f, wd, nb=1, nt=1, tm=n_s, stride=nbs, tf=FFN_TILE,
                        rows_real=n_s)

    y_sample = y_s.reshape(n_q, nbs, D_MODEL).transpose(1, 0, 2)
    s_cv = ctail_s[0, pad_c - (CONV_W - 1) * nbs:].reshape(CONV_W - 1, nbs, D_CONV).transpose(1, 0, 2)[None]
    s_fc = ftail_s[0, pad_f - (FFN_CONV_W - 1) * nbs:].reshape(FFN_CONV_W - 1, nbs, D_FF).transpose(1, 0, 2)[None]

    return (y_prompt, y_sample, p_k, p_v, p_ik, p_cv, p_fc,
            k_s[None], v_s[None], ki_s[None], s_cv, s_fc)
```

```python
import functools
import math

import jax
import jax.numpy as jnp
from jax import lax
from jax.experimental import pallas as pl
from jax.experimental.pallas import tpu as pltpu

F32 = jnp.float32
BF16 = jnp.bfloat16
I32 = jnp.int32

D_MODEL = 1024
D_CONV = D_MODEL // 2
CONV_W = 31
N_HEADS = 8
HEAD_DIM = 64
D_ATTN = N_HEADS * HEAD_DIM
IDX_HEADS = 8
IDX_DIM = 64
TOP_K_MAX = 256
N_BUCKETS = 32
MAX_EXACT = N_BUCKETS // 2
REL_MAX_DIST = 128
D_FF = 2816
FFN_CONV_W = 3
N_META = 16
PAGE_SIZE = 128
FFN_TILE = D_FF // 2
SEARCH_HEAD_BITS = 3
SEARCH_GROUP = 4
EPS = 1e-6
ATTN_SCALE = HEAD_DIM ** -0.5
IDX_SCALE = (IDX_HEADS * IDX_DIM) ** -0.5
D_MAIN = 2 * D_CONV + 4 * D_ATTN + 2 * D_MODEL
JB_Q, JB_K, JB_V, JB_QI, JB_G = 2, 3, 4, 5, 6

LANES = 128
SUBLANES = 8
KEY_INVALID = -2 ** 31
NEG_BIG = -1e30
VMEM_LIMIT = 56 * 1024 * 1024
NT_DIMS = (((1,), (1,)), ((), ()))


def _cparams(sem):
    return pltpu.CompilerParams(dimension_semantics=sem, vmem_limit_bytes=VMEM_LIMIT)


def _sigmoid(x):
    return 1.0 / (1.0 + jnp.exp(-x))


def _ceil_to(x, m):
    return -(-x // m) * m


def _col_reduce(x, op, reduce_fn):
    n = x.shape[0]
    while n > SUBLANES and n % (2 * SUBLANES) == 0:
        n //= 2
        x = op(x[:n], x[n:2 * n])
    return reduce_fn(x, axis=0, keepdims=True)


def _split_bf16(x):
    hi = x.astype(BF16)
    return hi, (x - hi.astype(F32)).astype(BF16)


def _in_proj_kernel(x_ref, g_ref, wm_ref, wt_ref, ws_ref, wst_ref, gq_ref, gk_ref, seg_ref,
                    a_ref, k_ref, kb_ref, v_ref, qt_ref, vt_ref, qit_ref, g2_ref, ki_ref, kib_ref, wit_ref,
                    xn_ref):
    j = pl.program_id(1)

    @pl.when(j == 0)
    def _():
        x = x_ref[...]
        ms = jnp.mean(x * x, axis=-1, keepdims=True)
        xn_ref[...] = (x * lax.rsqrt(ms + EPS) * g_ref[...]).astype(BF16)
        ki = jnp.dot(xn_ref[...], ws_ref[...], preferred_element_type=F32)[:, :IDX_DIM]
        ki_ref[...] = ki
        kib_ref[...] = ki.astype(BF16)
        wit_ref[...] = lax.dot_general(wst_ref[...], xn_ref[...], NT_DIMS, preferred_element_type=F32)

    @pl.when(j < JB_Q)
    def _():
        a_ref[...] = jnp.dot(xn_ref[...], wm_ref[...], preferred_element_type=F32)

    @pl.when(j >= JB_G)
    def _():
        g2_ref[...] = jnp.dot(xn_ref[...], wm_ref[...], preferred_element_type=F32).astype(BF16)

    @pl.when(j == JB_Q)
    def _():
        yt = lax.dot_general(wt_ref[0], xn_ref[...], NT_DIMS, preferred_element_type=F32)
        hi, lo = _split_bf16(yt * yt)
        seg = seg_ref[...]
        ms = jnp.concatenate(
            [jnp.dot(seg, hi[r:r + LANES], preferred_element_type=F32)
             + jnp.dot(seg, lo[r:r + LANES], preferred_element_type=F32) for r in range(0, D_ATTN, LANES)],
            axis=0)
        qt_ref[...] = (yt * lax.rsqrt(ms + EPS) * gq_ref[...]).astype(BF16)

    @pl.when(j == JB_K)
    def _():
        y = jnp.dot(xn_ref[...], wm_ref[...], preferred_element_type=F32)
        hi, lo = _split_bf16(y * y)
        seg = seg_ref[...]
        ms = jnp.concatenate(
            [jnp.dot(hi[:, c:c + LANES], seg, preferred_element_type=F32)
             + jnp.dot(lo[:, c:c + LANES], seg, preferred_element_type=F32) for c in range(0, D_ATTN, LANES)],
            axis=1)
        k = y * lax.rsqrt(ms + EPS) * gk_ref[...]
        k_ref[...] = k
        kb_ref[...] = k.astype(BF16)

    @pl.when(j == JB_V)
    def _():
        v_ref[...] = jnp.dot(xn_ref[...], wm_ref[...], preferred_element_type=F32)
        vt_ref[...] = lax.dot_general(wt_ref[1], xn_ref[...], NT_DIMS, preferred_element_type=F32).astype(BF16)

    @pl.when(j == JB_QI)
    def _():
        qit_ref[...] = lax.dot_general(wt_ref[2], xn_ref[...], NT_DIMS, preferred_element_type=F32).astype(BF16)


def _in_proj(x2d, g, wm, wt, ws, wst, gq, gk, seg, tm):
    n = x2d.shape[0]
    tn = D_ATTN
    nj = D_MAIN // tn
    row = lambda i, j: (i, 0)
    colt = lambda i, j: (0, i)
    const2 = lambda i, j: (0, 0)
    out_shape = [
        jax.ShapeDtypeStruct((n, 2 * D_CONV), F32),
        jax.ShapeDtypeStruct((n, D_ATTN), F32),
        jax.ShapeDtypeStruct((n, D_ATTN), BF16),
        jax.ShapeDtypeStruct((n, D_ATTN), F32),
        jax.ShapeDtypeStruct((D_ATTN, n), BF16),
        jax.ShapeDtypeStruct((D_ATTN, n), BF16),
        jax.ShapeDtypeStruct((D_ATTN, n), BF16),
        jax.ShapeDtypeStruct((n, 2 * D_MODEL), BF16),
        jax.ShapeDtypeStruct((n, IDX_DIM), F32),
        jax.ShapeDtypeStruct((n, IDX_DIM), BF16),
        jax.ShapeDtypeStruct((IDX_HEADS, n), F32),
    ]
    out_specs = [
        pl.BlockSpec((tm, tn), lambda i, j: (i, jnp.minimum(j, 1))),
        pl.BlockSpec((tm, tn), row),
        pl.BlockSpec((tm, tn), row),
        pl.BlockSpec((tm, tn), row),
        pl.BlockSpec((tn, tm), colt),
        pl.BlockSpec((tn, tm), colt),
        pl.BlockSpec((tn, tm), colt),
        pl.BlockSpec((tm, tn), lambda i, j: (i, jnp.clip(j - JB_G, 0, nj - JB_G - 1))),
        pl.BlockSpec((tm, IDX_DIM), row),
        pl.BlockSpec((tm, IDX_DIM), row),
        pl.BlockSpec((IDX_HEADS, tm), colt),
    ]
    return pl.pallas_call(
        _in_proj_kernel,
        grid=(n // tm, nj),
        in_specs=[
            pl.BlockSpec((tm, D_MODEL), row),
            pl.BlockSpec((1, D_MODEL), const2),
            pl.BlockSpec((D_MODEL, tn), lambda i, j: (0, j)),
            pl.BlockSpec((3, tn, D_MODEL), lambda i, j: (0, 0, 0)),
            pl.BlockSpec((D_MODEL, LANES), const2),
            pl.BlockSpec((IDX_HEADS, D_MODEL), const2),
            pl.BlockSpec((D_ATTN, 1), const2),
            pl.BlockSpec((1, D_ATTN), const2),
            pl.BlockSpec((LANES, LANES), const2),
        ],
        out_specs=out_specs,
        out_shape=out_shape,
        scratch_shapes=[pltpu.VMEM((tm, D_MODEL), BF16)],
        compiler_params=_cparams(("parallel", "arbitrary")),
        name="in_proj",
    )(x2d, g, wm, wt, ws, wst, gq, gk, seg)


def _conv_kernel(a_ref, past_ref, cw_ref, cb_ref, lg_ref, lb_ref, wa_ref, ya_ref, tail_ref, ext_ref, h_ref,
                 *, stride, tm, pad, rc, tail_tile, tail_loc):
    t = pl.program_id(1)

    @pl.when(t == 0)
    def _():
        ext_ref[0, 0:pad, :] = past_ref[0]

    @pl.when(t > 0)
    def _():
        ext_ref[0, 0:pad, :] = ext_ref[0, tm:tm + pad, :]

    a = a_ref[...]
    ext_ref[0, pad:pad + tm, :] = a[:, :D_CONV] * _sigmoid(a[:, D_CONV:])

    @pl.when(t <= tail_tile)
    def _():
        tail_ref[0] = ext_ref[0, tail_loc:tail_loc + pad, :]

    n_shift = ext_ref.shape[0]
    length = pad + tm
    for r in range(1, n_shift):
        ext_ref[r, 0:length - SUBLANES, :] = ext_ref[0, r:r + length - SUBLANES, :]

    cb = cb_ref[...]
    lg = lg_ref[...]
    lb = lb_ref[...]
    for r0 in range(0, tm, rc):
        acc = jnp.zeros((rc, D_CONV), F32) + cb
        for w in range(CONV_W):
            off = pad - (CONV_W - 1 - w) * stride + r0
            r = off % n_shift
            acc = acc + ext_ref[r, off - r:off - r + rc, :] * cw_ref[w:w + 1, :]
        mu = jnp.mean(acc, axis=-1, keepdims=True)
        d = acc - mu
        var = jnp.mean(d * d, axis=-1, keepdims=True)
        h = d * lax.rsqrt(var + EPS) * lg + lb
        h_ref[r0:r0 + rc, :] = (h * _sigmoid(h)).astype(BF16)
    ya_ref[...] = jnp.dot(h_ref[...], wa_ref[...], preferred_element_type=F32).astype(BF16)


def _tail_position(rows_real, tm):
    tail_tile = (rows_real - 1) // tm
    return tail_tile, rows_real - tail_tile * tm


def _conv_branch(a_in, past, cw, cb, lg, lb, wa, *, nb, nt, tm, stride, rows_real):
    pad = past.shape[1]
    rc = 32 if tm % 32 == 0 else SUBLANES
    tail_tile, tail_loc = _tail_position(rows_real, tm)
    kern = functools.partial(_conv_kernel, stride=stride, tm=tm, pad=pad, rc=rc,
                             tail_tile=tail_tile, tail_loc=tail_loc)
    return pl.pallas_call(
        kern,
        grid=(nb, nt),
        in_specs=[
            pl.BlockSpec((tm, 2 * D_CONV), lambda b, t: (b * nt + t, 0)),
            pl.BlockSpec((1, pad, D_CONV), lambda b, t: (b, 0, 0)),
            pl.BlockSpec((32, D_CONV), lambda b, t: (0, 0)),
            pl.BlockSpec((1, D_CONV), lambda b, t: (0, 0)),
            pl.BlockSpec((1, D_CONV), lambda b, t: (0, 0)),
            pl.BlockSpec((1, D_CONV), lambda b, t: (0, 0)),
            pl.BlockSpec((D_CONV, D_MODEL), lambda b, t: (0, 0)),
        ],
        out_specs=[
            pl.BlockSpec((tm, D_MODEL), lambda b, t: (b * nt + t, 0)),
            pl.BlockSpec((1, pad, D_CONV), lambda b, t: (b, 0, 0)),
        ],
        out_shape=[jax.ShapeDtypeStruct((nb * nt * tm, D_MODEL), BF16),
                   jax.ShapeDtypeStruct((nb, pad, D_CONV), F32)],
        scratch_shapes=[pltpu.VMEM((1 if stride % SUBLANES == 0 else SUBLANES, pad + tm, D_CONV), F32),
                        pltpu.VMEM((tm, D_CONV), BF16)],
        compiler_params=_cparams(("parallel", "arbitrary")),
        name="conv_branch",
    )(a_in, past, cw, cb, lg, lb, wa)


def _rel_bucket(rel):
    n = jnp.maximum(rel, 0)
    nf = jnp.maximum(n, 1).astype(F32)
    large = MAX_EXACT + (jnp.log(nf / MAX_EXACT) / math.log(REL_MAX_DIST / MAX_EXACT)
                         * (N_BUCKETS - MAX_EXACT)).astype(I32)
    large = jnp.minimum(large, N_BUCKETS - 1)
    return jnp.where(n < MAX_EXACT, n, large)


def _bias_of(rel, rb_rows):
    bucket = _rel_bucket(rel)
    out = jnp.zeros(rel.shape, F32)
    for b in range(N_BUCKETS):
        out = jnp.where(bucket == b, rb_rows[:, b:b + 1], out)
    return out


def _prompt_bias_kernel(rbt_ref, b1_ref):
    key = lax.broadcasted_iota(I32, (LANES, LANES), 0)
    qry = lax.broadcasted_iota(I32, (LANES, LANES), 1)
    for h in range(N_HEADS):
        for d in range(3):
            b1_ref[h * 3 + d] = _bias_of(d * LANES + qry - key, rbt_ref[h:h + 1, :])


def _prompt_bias(rbt):
    return pl.pallas_call(
        _prompt_bias_kernel,
        out_shape=jax.ShapeDtypeStruct((N_HEADS * 3, LANES, LANES), F32),
        name="prompt_bias",
    )(rbt)


def _score_to_key(s, valid):
    bits = pltpu.bitcast(s, I32)
    key = bits ^ ((bits >> 31) & 0x7FFFFFFF)
    key = jnp.where(bits == KEY_INVALID, 0, key)
    return jnp.where(valid, key, KEY_INVALID)


def _threshold_search(count_ge, n_keep, shape):
    cnt_valid = count_ge(jnp.full(shape, KEY_INVALID + 1, I32))
    c0 = count_ge(jnp.zeros(shape, I32))
    ok0 = c0 >= n_keep
    base = jnp.where(ok0, 0, KEY_INVALID).astype(I32)
    cnt = jnp.where(ok0, c0, cnt_valid)

    def step(t, base, cnt):
        cand = base | lax.shift_left(jnp.int32(1), 30 - t)
        c = count_ge(cand)
        ok = c >= n_keep
        return jnp.where(ok, cand, base), jnp.where(ok, c, cnt)

    for t in range(SEARCH_HEAD_BITS):
        base, cnt = step(t, base, cnt)

    def cond(state):
        t, _, _, unsettled = state
        return jnp.logical_and(t < 31, unsettled)

    def body(state):
        t, base, cnt, _ = state
        for u in range(SEARCH_GROUP):
            base, cnt = step(t + u, base, cnt)
        return t + SEARCH_GROUP, base, cnt, jnp.max(cnt) > n_keep

    _, base, cnt, _ = lax.while_loop(cond, body, (jnp.int32(SEARCH_HEAD_BITS), base, cnt, jnp.max(cnt) > n_keep))
    return jnp.maximum(base, KEY_INVALID + 1), cnt


def _pattn_kernel(qt_ref, k_ref, vt_ref, qit_ref, ki_ref, wit_ref, b1_ref, tril_ref, o_ref,
                  keys_ref, qiw_ref, qbd_ref, m_ref, l_ref, acc_ref, ot_ref, p_ref, alpha_ref, sa_ref, sb_ref, ia_ref, ib_ref, *, n_keep):
    i = pl.program_id(1)
    tq = LANES
    row1 = (1, tq)
    lead = (i + 1) % 2
    npair = (i + 1) // 2
    qpos = i * tq + lax.broadcasted_iota(I32, row1, 1)

    for h in range(IDX_HEADS):
        qiw_ref[:, h * tq:(h + 1) * tq] = qit_ref[h * IDX_DIM:(h + 1) * IDX_DIM, :]
    qbd_ref[...] = jnp.zeros(qbd_ref.shape, BF16)
    for h in range(N_HEADS):
        hp, e = divmod(h, 2)
        qbd_ref[hp, e * HEAD_DIM:(e + 1) * HEAD_DIM, e * tq:(e + 1) * tq] = qt_ref[h * HEAD_DIM:(h + 1) * HEAD_DIM, :]
    wit = wit_ref[...] * IDX_SCALE

    blk = (LANES, tq)
    nblk = i + 1

    ck = 2 * LANES
    last_off = jnp.maximum(nblk * LANES - ck, 0)

    def chunk_off(c):
        return pl.multiple_of(jnp.minimum(c * ck, last_off), LANES)

    def idx_scores_into(dst, c):
        dst[...] = jnp.dot(ki_ref[pl.ds(chunk_off(c), ck), :], qiw_ref[...], preferred_element_type=F32)

    def idx_keys_from(src, c):
        off = chunk_off(c)
        s = jnp.zeros((ck, tq), F32)
        for h in range(IDX_HEADS):
            s = s + wit[h:h + 1, :] * jnp.maximum(src[:, h * tq:(h + 1) * tq], 0.0)
        kpos = off + lax.broadcasted_iota(I32, (ck, tq), 0)
        keys_ref[pl.ds(off, ck), :] = _score_to_key(s, kpos <= qpos)

    idx_scores_into(ia_ref, 0)

    def idx_body(t, carry):
        idx_scores_into(ib_ref, 2 * t + 1)
        idx_keys_from(ia_ref, 2 * t)
        idx_scores_into(ia_ref, 2 * t + 2)
        idx_keys_from(ib_ref, 2 * t + 1)
        return carry

    lax.fori_loop(0, (nblk + 3) // 4, idx_body, 0)

    def count_ge(cand):
        cb = jnp.broadcast_to(cand, blk)

        def hit(off):
            return jnp.where(keys_ref[pl.ds(off, LANES), :] >= cb, 1.0, 0.0)

        def body(j, acc):
            off = pl.multiple_of(lead * LANES + j * 2 * LANES, LANES)
            return acc + hit(off) + hit(off + LANES)
        acc = lax.fori_loop(0, npair, body, jnp.where(lead == 1, hit(0), 0.0))
        return _col_reduce(acc, jnp.add, jnp.sum)

    base, cnt = _threshold_search(count_ge, float(n_keep), row1)

    @pl.when(jnp.max(cnt) > float(n_keep))
    def _():
        need = float(n_keep) - count_ge(base + 1)
        tie_q = cnt > float(n_keep)

        def body(j, seen):
            off = pl.multiple_of(j * LANES, LANES)
            k = keys_ref[pl.ds(off, LANES), :]
            eq = jnp.logical_and(k == base, tie_q)
            eqf = jnp.where(eq, 1.0, 0.0)
            pref = seen + jnp.dot(tril_ref[...], eqf.astype(BF16), preferred_element_type=F32)
            keys_ref[pl.ds(off, LANES), :] = jnp.where(jnp.logical_and(eq, pref > need), KEY_INVALID, k)
            return seen + jnp.sum(eqf, axis=0, keepdims=True)
        lax.fori_loop(0, nblk, body, jnp.zeros(row1, F32))

    m_ref[...] = jnp.full(m_ref.shape, NEG_BIG, F32)
    l_ref[...] = jnp.zeros(l_ref.shape, F32)
    acc_ref[...] = jnp.zeros(acc_ref.shape, F32)
    p_ref[...] = jnp.zeros(p_ref.shape, BF16)
    alpha_ref[...] = jnp.ones(alpha_ref.shape, F32)
    base_b = jnp.broadcast_to(base, blk)

    def clamp_blk(jb):
        return jnp.clip(jb, 0, nblk - 1)

    def apply_pv(jb):
        off = pl.multiple_of(clamp_blk(jb) * LANES, LANES)
        for h in range(N_HEADS):
            pv = jnp.dot(vt_ref[h * HEAD_DIM:(h + 1) * HEAD_DIM, pl.ds(off, LANES)], p_ref[h],
                         preferred_element_type=F32)
            acc_ref[h] = alpha_ref[h] * acc_ref[h] + pv

    def qk_into(s_dst, jb):
        off = pl.multiple_of(clamp_blk(jb) * LANES, LANES)
        for hp in range(N_HEADS // 2):
            for e in range(2):
                s_dst[2 * hp + e] = jnp.dot(k_ref[pl.ds(off, LANES), hp * LANES:(hp + 1) * LANES],
                                            qbd_ref[hp, :, e * tq:(e + 1) * tq], preferred_element_type=F32)

    def softmax_from(s_src, jb):
        jc = clamp_blk(jb)
        off = pl.multiple_of(jc * LANES, LANES)
        sel = jnp.logical_and(keys_ref[pl.ds(off, LANES), :] >= base_b, jb < nblk)
        d = jnp.minimum(i - jc, 2)
        for h in range(N_HEADS):
            s = jnp.where(sel, s_src[h] + b1_ref[h * 3 + d], NEG_BIG)
            m_old = m_ref[h]
            m_new = jnp.maximum(m_old, _col_reduce(s, jnp.maximum, jnp.max))
            p = jnp.exp(s - m_new)
            alpha = jnp.exp(m_old - m_new)
            l_ref[h] = alpha * l_ref[h] + _col_reduce(p, jnp.add, jnp.sum)
            p_ref[h] = p.astype(BF16)
            alpha_ref[h] = alpha
            m_ref[h] = m_new

    def half_trip(jb, s_cur, s_nxt):
        apply_pv(jb - 1)
        qk_into(s_nxt, jb + 1)
        softmax_from(s_cur, jb)

    qk_into(sa_ref, 0)

    def att_body(t, carry):
        half_trip(2 * t, sa_ref, sb_ref)
        half_trip(2 * t + 1, sb_ref, sa_ref)
        return carry

    n_trips = (nblk + 1) // 2
    lax.fori_loop(0, n_trips, att_body, 0)
    apply_pv(2 * n_trips - 1)
    for h in range(N_HEADS):
        ot_ref[h * HEAD_DIM:(h + 1) * HEAD_DIM, :] = acc_ref[h] / l_ref[h]
    o_ref[...] = ot_ref[...].T


def _prompt_attention(qt, kb, vt, qit, kib, wit, b1, tril, n_keep, nb, tp):
    tq = LANES
    nq = tp // tq
    n = nb * tp
    qcol = lambda b, i: (0, b * nq + i)
    return pl.pallas_call(
        functools.partial(_pattn_kernel, n_keep=n_keep),
        grid=(nb, nq),
        in_specs=[
            pl.BlockSpec((D_ATTN, tq), qcol),
            pl.BlockSpec((tp, D_ATTN), lambda b, i: (b, 0)),
            pl.BlockSpec((D_ATTN, tp), lambda b, i: (0, b)),
            pl.BlockSpec((IDX_HEADS * IDX_DIM, tq), qcol),
            pl.BlockSpec((tp, IDX_DIM), lambda b, i: (b, 0)),
            pl.BlockSpec((IDX_HEADS, tq), qcol),
            pl.BlockSpec((N_HEADS * 3, LANES, LANES), lambda b, i: (0, 0, 0)),
            pl.BlockSpec((LANES, LANES), lambda b, i: (0, 0)),
        ],
        out_specs=pl.BlockSpec((tq, D_ATTN), lambda b, i: (b * nq + i, 0)),
        out_shape=jax.ShapeDtypeStruct((n, D_ATTN), F32),
        scratch_shapes=[
            pltpu.VMEM((tp, tq), I32),
            pltpu.VMEM((IDX_DIM, IDX_HEADS * tq), BF16),
            pltpu.VMEM((N_HEADS // 2, LANES, 2 * tq), BF16),
            pltpu.VMEM((N_HEADS, 1, tq), F32),
            pltpu.VMEM((N_HEADS, 1, tq), F32),
            pltpu.VMEM((N_HEADS, HEAD_DIM, tq), F32),
            pltpu.VMEM((D_ATTN, tq), F32),
            pltpu.VMEM((N_HEADS, LANES, tq), BF16),
            pltpu.VMEM((N_HEADS, 1, tq), F32),
            pltpu.VMEM((N_HEADS, LANES, tq), F32),
            pltpu.VMEM((N_HEADS, LANES, tq), F32),
            pltpu.VMEM((2 * LANES, IDX_HEADS * tq), F32),
            pltpu.VMEM((2 * LANES, IDX_HEADS * tq), F32),
        ],
        compiler_params=_cparams(("parallel", "arbitrary")),
        name="prompt_attention",
    )(qt, kb, vt, qit, kib, wit, b1, tril)


def _sidx_t_kernel(pt_ref, qi_ref, wi_ref, kin_ref, tri_ref, lin_ref, cache_ref, sel_ref,
                   kibuf, sem, keys_ref, *, n_keep, n_pages, n_q, ppc, ncs, ncp, gb):
    b = pl.program_id(0)
    nb = pl.num_programs(0)
    slot = b % 2

    def page_copy(step, g, p, sl):
        return pltpu.make_async_copy(cache_ref.at[pt_ref[step * gb + g, p]], kibuf.at[sl, g * n_pages + p],
                                     sem.at[sl])

    def start_all(step, sl):
        for g in range(gb):
            def body(p, c):
                page_copy(step, g, p, sl).start()
                return c
            lax.fori_loop(0, n_pages, body, 0, unroll=ppc)

    def wait_all(step, sl):
        for g in range(gb):
            def body(p, c):
                page_copy(step, g, p, sl).wait()
                return c
            lax.fori_loop(0, n_pages, body, 0, unroll=ppc)

    @pl.when(b == 0)
    def _():
        start_all(b, slot)

    @pl.when(b + 1 < nb)
    def _():
        start_all(b + 1, 1 - slot)

    wait_all(b, slot)
    keys_ref[...] = jnp.full(keys_ref.shape, KEY_INVALID, I32)
    qrow = lax.broadcasted_iota(I32, (n_q, LANES), 0)
    jcol = lax.broadcasted_iota(I32, (n_q, LANES), 1)

    for g in range(gb):
        qi = qi_ref[g]
        w = wi_ref[g] * IDX_SCALE

        def score_rows(s):
            s = jnp.maximum(s, 0.0) * w
            return jnp.sum(s.reshape(n_q, IDX_HEADS, s.shape[-1]), axis=1)

        def chunk_body(c, carry):
            p0 = pl.multiple_of(c * ppc, ppc)
            kt = jnp.concatenate([kibuf[slot, g * n_pages + p0 + p] for p in range(ppc)],
                                 axis=1).astype(BF16)
            s = score_rows(jnp.dot(qi, kt, preferred_element_type=F32))
            key = _score_to_key(s, jnp.full(s.shape, True))
            for q in range(n_q):
                for p in range(ppc):
                    keys_ref[g * n_q + q, pl.ds(p0 + p, 1), :] = key[q:q + 1, p * LANES:(p + 1) * LANES]
            return carry

        lax.fori_loop(0, n_pages // ppc, chunk_body, 0)
        s_new = score_rows(lax.dot_general(qi, kin_ref[g], NT_DIMS, preferred_element_type=F32))
        key_new = _score_to_key(s_new, jcol <= qrow)
        for q in range(n_q):
            keys_ref[g * n_q + q, n_pages:n_pages + 1, :] = key_new[q:q + 1, :]

    shape = (gb * n_q, 1, LANES)

    def count_ge(cand):
        hit = jnp.where(keys_ref[:, 0:ncs, :] >= cand, 1.0, 0.0)
        part = jnp.sum(hit, axis=1, keepdims=True)
        return jnp.broadcast_to(jnp.sum(part, axis=2, keepdims=True), shape)

    base, cnt = _threshold_search(count_ge, float(n_keep), shape)

    @pl.when(jnp.max(cnt) > float(n_keep))
    def _():
        need = float(n_keep) - count_ge(base + 1)
        ones = jnp.ones((LANES, LANES), BF16)
        for q in range(gb * n_q):
            k = keys_ref[q]
            eq = jnp.logical_and(k == base[q], cnt[q] > float(n_keep))
            eqb = jnp.where(eq, 1.0, 0.0).astype(BF16)
            within = jnp.dot(eqb, tri_ref[...], preferred_element_type=F32)
            rowcnt = jnp.dot(eqb, ones, preferred_element_type=F32).astype(BF16)
            before = jnp.dot(lin_ref[...], rowcnt, preferred_element_type=F32) - rowcnt.astype(F32)
            drop = jnp.logical_and(eq, within + before > need[q])
            keys_ref[q] = jnp.where(drop, KEY_INVALID, k)

    sel = jnp.where(keys_ref[:, 0:ncs, :] >= base, 1.0, 0.0)
    for g in range(gb):
        sel_ref[g] = sel[g * n_q:(g + 1) * n_q]


def _sample_index_t(page_table, qi_rows, wi_rows, ki_new, tri, cache_ikt, n_keep):
    nb_all, n_pages = page_table.shape
    gb = 4 if nb_all % 4 == 0 else (2 if nb_all % 2 == 0 else 1)
    nb = nb_all // gb
    rows = qi_rows.shape[1]
    n_q = rows // IDX_HEADS
    ppc = 8 if n_pages % 8 == 0 else 1
    ncs = _ceil_to(n_pages + 1, SUBLANES)
    ncp = _ceil_to(n_pages + 1, LANES)
    assert ncp <= 256, "tie-break prefix counts are carried in bf16, exact up to 256"
    ar = jnp.arange(ncp)
    lin = (ar[None, :] <= ar[:, None]).astype(BF16)
    kern = functools.partial(_sidx_t_kernel, n_keep=n_keep, n_pages=n_pages, n_q=n_q, ppc=ppc,
                             ncs=ncs, ncp=ncp, gb=gb)
    return pl.pallas_call(
        kern,
        grid_spec=pltpu.PrefetchScalarGridSpec(
            num_scalar_prefetch=1,
            grid=(nb,),
            in_specs=[
                pl.BlockSpec((gb, rows, IDX_DIM), lambda b, pt: (b, 0, 0)),
                pl.BlockSpec((gb, rows, 1), lambda b, pt: (b, 0, 0)),
                pl.BlockSpec((gb, LANES, IDX_DIM), lambda b, pt: (b, 0, 0)),
                pl.BlockSpec((LANES, LANES), lambda b, pt: (0, 0)),
                pl.BlockSpec((ncp, ncp), lambda b, pt: (0, 0)),
                pl.BlockSpec(memory_space=pl.ANY),
            ],
            out_specs=pl.BlockSpec((gb, n_q, ncs, LANES), lambda b, pt: (b, 0, 0, 0)),
            scratch_shapes=[
                pltpu.VMEM((2, gb * n_pages, IDX_DIM, PAGE_SIZE), F32),
                pltpu.SemaphoreType.DMA((2,)),
                pltpu.VMEM((gb * n_q, ncp, LANES), I32),
            ],
        ),
        out_shape=jax.ShapeDtypeStruct((nb_all, n_q, ncs, LANES), F32),
        compiler_params=_cparams(("arbitrary",)),
        name="sample_index",
    )(page_table, qi_rows, wi_rows, ki_new, tri, lin, cache_ikt)


def _sdense_kernel(pt_ref, qbd_ref, sel_ref, seln_ref, kn_ref, vn_ref, bias_ref, hsel_ref, ck_ref, cv_ref, o_ref,
                   kbuf, vbuf, sem, bias_buf, m_ref, l_ref, acc_ref, *, n_q, n_pages, ppc):
    b = pl.program_id(0)
    c = pl.program_id(1)
    nb = pl.num_programs(0)
    n_chunks = n_pages // ppc
    step = b * n_chunks + c
    slot = step % 2
    rows = n_q * N_HEADS
    ck = ppc * PAGE_SIZE

    def copies(bb, cc, sl, p):
        page = pt_ref[bb, cc * ppc + p]
        return (pltpu.make_async_copy(ck_ref.at[page], kbuf.at[sl, p], sem.at[0, sl]),
                pltpu.make_async_copy(cv_ref.at[page], vbuf.at[sl, p], sem.at[1, sl]))

    def start_all(bb, cc, sl):
        for p in range(ppc):
            kc, vc = copies(bb, cc, sl, p)
            kc.start()
            vc.start()

    def wait_all(bb, cc, sl):
        for p in range(ppc):
            kc, vc = copies(bb, cc, sl, p)
            kc.wait()
            vc.wait()

    @pl.when(step == 0)
    def _():
        start_all(b, c, slot)

    @pl.when(step + 1 < nb * n_chunks)
    def _():
        nxt = step + 1
        start_all(nxt // n_chunks, nxt % n_chunks, 1 - slot)

    @pl.when(c == 0)
    def _():
        m_ref[...] = jnp.full(m_ref.shape, NEG_BIG, F32)
        l_ref[...] = jnp.zeros(l_ref.shape, F32)
        acc_ref[...] = jnp.zeros(acc_ref.shape, F32)
        bias_buf[...] = jnp.broadcast_to(bias_ref[2][:, 0:1], bias_buf.shape)

    @pl.when(c == n_chunks - 1)
    def _():
        bias_buf[:, ck - PAGE_SIZE:ck] = bias_ref[0]

    wait_all(b, c, slot)
    qbd = qbd_ref[0]

    def expand(sel):
        n = sel.shape[-1]
        return jnp.broadcast_to(sel[:, None, :], (n_q, N_HEADS, n)).reshape(rows, n)

    def update(s, live, pv_fn):
        s = jnp.where(live, s, NEG_BIG)
        m_old = m_ref[...]
        m_new = jnp.maximum(m_old, jnp.max(s, axis=1, keepdims=True))
        p = jnp.exp(s - m_new)
        alpha = jnp.exp(m_old - m_new)
        l_ref[...] = alpha * l_ref[...] + jnp.sum(p, axis=1, keepdims=True)
        acc_ref[...] = alpha * acc_ref[...] + pv_fn(p.astype(BF16))
        m_ref[...] = m_new

    s = jnp.concatenate([jnp.dot(qbd, kbuf[slot, p].astype(BF16), preferred_element_type=F32)
                         for p in range(ppc)], axis=1) + bias_buf[...]
    live = jnp.concatenate([expand(sel_ref[0, :, p, :]) for p in range(ppc)], axis=1) > 0.5

    def pv_pages(pb):
        out = jnp.zeros((rows, D_ATTN), F32)
        for p in range(ppc):
            out = out + lax.dot_general(pb[:, p * PAGE_SIZE:(p + 1) * PAGE_SIZE], vbuf[slot, p].astype(BF16),
                                        NT_DIMS, preferred_element_type=F32)
        return out

    update(s, live, pv_pages)

    @pl.when(c == n_chunks - 1)
    def _():
        s_new = lax.dot_general(qbd, kn_ref[0], NT_DIMS, preferred_element_type=F32) + bias_ref[1]
        update(s_new, expand(seln_ref[0, :, 0, :]) > 0.5,
               lambda pb: jnp.dot(pb, vn_ref[0], preferred_element_type=F32))
        full = acc_ref[...] / l_ref[...] * hsel_ref[...]
        o_ref[0] = jnp.sum(full.reshape(n_q, N_HEADS, D_ATTN), axis=1)


def _sample_attention_dense(page_table, qbd, sel, k_new, v_new, sbias, hsel, ckt, cvt):
    nb, n_pages = page_table.shape
    rows = qbd.shape[1]
    n_q = rows // N_HEADS
    ppc = 32 if n_pages % 32 == 0 else 16
    assert n_pages % ppc == 0, "cached pages are attended in chunks of 16 or 32"
    n_chunks = n_pages // ppc
    ck = ppc * PAGE_SIZE
    kern = functools.partial(_sdense_kernel, n_q=n_q, n_pages=n_pages, ppc=ppc)
    return pl.pallas_call(
        kern,
        grid_spec=pltpu.PrefetchScalarGridSpec(
            num_scalar_prefetch=1,
            grid=(nb, n_chunks),
            in_specs=[
                pl.BlockSpec((1, rows, D_ATTN), lambda b, c, pt: (b, 0, 0)),
                pl.BlockSpec((1, n_q, ppc, LANES), lambda b, c, pt: (b, 0, c, 0)),
                pl.BlockSpec((1, n_q, SUBLANES, LANES), lambda b, c, pt: (b, 0, n_pages // SUBLANES, 0)),
                pl.BlockSpec((1, LANES, D_ATTN), lambda b, c, pt: (b, 0, 0)),
                pl.BlockSpec((1, LANES, D_ATTN), lambda b, c, pt: (b, 0, 0)),
                pl.BlockSpec((3, rows, LANES), lambda b, c, pt: (0, 0, 0)),
                pl.BlockSpec((rows, D_ATTN), lambda b, c, pt: (0, 0)),
                pl.BlockSpec(memory_space=pl.ANY),
                pl.BlockSpec(memory_space=pl.ANY),
            ],
            out_specs=pl.BlockSpec((1, n_q, D_ATTN), lambda b, c, pt: (b, 0, 0)),
            scratch_shapes=[
                pltpu.VMEM((2, ppc, D_ATTN, PAGE_SIZE), F32),
                pltpu.VMEM((2, ppc, D_ATTN, PAGE_SIZE), F32),
                pltpu.SemaphoreType.DMA((2, 2)),
                pltpu.VMEM((rows, ck), F32),
                pltpu.VMEM((rows, 1), F32),
                pltpu.VMEM((rows, 1), F32),
                pltpu.VMEM((rows, D_ATTN), F32),
            ],
        ),
        out_shape=jax.ShapeDtypeStruct((nb, n_q, D_ATTN), F32),
        compiler_params=_cparams(("arbitrary", "arbitrary")),
        name="sample_attention",
    )(page_table, qbd, sel, sel, k_new, v_new, sbias, hsel, ckt, cvt)


def _sample_bias_kernel(rbt_ref, o_ref, *, n_q):
    rows = n_q * N_HEADS
    q = lax.broadcasted_iota(I32, (rows, LANES), 0) // N_HEADS
    col = lax.broadcasted_iota(I32, (rows, LANES), 1)
    rb = rbt_ref[...]
    o_ref[0] = _bias_of(PAGE_SIZE + q - col, rb)
    o_ref[1] = _bias_of(q - col, rb)
    o_ref[2] = _bias_of(jnp.full((rows, LANES), REL_MAX_DIST, I32), rb)


def _sample_bias(rbt_rows, n_q):
    rows = n_q * N_HEADS
    return pl.pallas_call(
        functools.partial(_sample_bias_kernel, n_q=n_q),
        out_shape=jax.ShapeDtypeStruct((3, rows, LANES), F32),
        name="sample_bias",
    )(rbt_rows)


def _merge_kernel(o_ref, ya_ref, ga_ref, gb_ref, x_ref, wb_ref, wo_ref, x1_ref):
    yb = jnp.dot(o_ref[...].astype(BF16), wb_ref[...], preferred_element_type=F32)
    m = (_sigmoid(ga_ref[...].astype(F32)) * ya_ref[...].astype(F32)
         + _sigmoid(gb_ref[...].astype(F32)) * yb)
    x1_ref[...] = x_ref[...] + jnp.dot(m.astype(BF16), wo_ref[...], preferred_element_type=F32)


def _merge(o2d, ya, gates, x2d, wb, wo, tm):
    n = x2d.shape[0]
    return pl.pallas_call(
        _merge_kernel,
        grid=(n // tm,),
        in_specs=[
            pl.BlockSpec((tm, D_ATTN), lambda i: (i, 0)),
            pl.BlockSpec((tm, D_MODEL), lambda i: (i, 0)),
            pl.BlockSpec((tm, D_MODEL), lambda i: (i, 0)),
            pl.BlockSpec((tm, D_MODEL), lambda i: (i, 1)),
            pl.BlockSpec((tm, D_MODEL), lambda i: (i, 0)),
            pl.BlockSpec((D_ATTN, D_MODEL), lambda i: (0, 0)),
            pl.BlockSpec((D_MODEL, D_MODEL), lambda i: (0, 0)),
        ],
        out_specs=pl.BlockSpec((tm, D_MODEL), lambda i: (i, 0)),
        out_shape=jax.ShapeDtypeStruct((n, D_MODEL), F32),
        compiler_params=_cparams(("parallel",)),
        name="merge_out_proj",
    )(o2d, ya, gates, gates, x2d, wb, wo)


def _ffn_kernel(x_ref, g_ref, wua_ref, wub_ref, fcw_ref, fcb_ref, past_ref, wd_ref, y_ref, tail_ref,
                xn_ref, carry_ref, ext_ref, acc_ref, *, stride, tm, pad, tail_loc, rs):
    t = pl.program_id(1)
    f = pl.program_id(2)
    nf = pl.num_programs(2)

    @pl.when(f == 0)
    def _():
        x = x_ref[...]
        ms = jnp.mean(x * x, axis=-1, keepdims=True)
        xn_ref[...] = (x * lax.rsqrt(ms + EPS) * g_ref[...]).astype(BF16)
        acc_ref[...] = jnp.zeros(acc_ref.shape, F32)

    @pl.when(t == 0)
    def _():
        ext_ref[0:pad, :] = past_ref[0]

    @pl.when(t > 0)
    def _():
        ext_ref[0:pad, :] = carry_ref[f]

    def up_proj(r0):
        xs = xn_ref[r0:r0 + rs, :]
        a = jnp.dot(xs, wua_ref[...], preferred_element_type=F32)
        ext_ref[pad + r0:pad + r0 + rs, :] = a
        return a, jnp.dot(xs, wub_ref[...], preferred_element_type=F32)

    starts = list(range(0, tm, rs))
    nxt = up_proj(starts[0])
    for k, r0 in enumerate(starts):
        a, bq = nxt
        if k + 1 < len(starts):
            nxt = up_proj(starts[k + 1])
        lo = pad + r0
        conv = (ext_ref[lo - 2 * stride:lo - 2 * stride + rs, :] * fcw_ref[0:1, :]
                + ext_ref[lo - stride:lo - stride + rs, :] * fcw_ref[1:2, :]
                + a * fcw_ref[2:3, :] + fcb_ref[...])
        h = conv * _sigmoid(conv) * bq
        acc_ref[r0:r0 + rs, :] += jnp.dot(h.astype(BF16), wd_ref[...], preferred_element_type=F32)

    carry_ref[f] = ext_ref[tm:tm + pad, :]
    tail_ref[0, 0] = ext_ref[tail_loc:tail_loc + pad, :]

    @pl.when(f == nf - 1)
    def _():
        y_ref[...] = x_ref[...] + acc_ref[...]


def _ffn(x1, g, wup, fcw, fcb, past, wd, *, nb, nt, tm, stride, tf, rows_real):
    pad = past.shape[1]
    nf = D_FF // tf
    tail_tile, tail_loc = _tail_position(rows_real, tm)
    rs = tm // 2 if tm % (2 * SUBLANES) == 0 and tm // 2 >= 2 * stride else tm
    kern = functools.partial(_ffn_kernel, stride=stride, tm=tm, pad=pad, tail_loc=tail_loc, rs=rs)
    y, tail = pl.pallas_call(
        kern,
        grid=(nb, nt, nf),
        in_specs=[
            pl.BlockSpec((tm, D_MODEL), lambda b, t, f: (b * nt + t, 0)),
            pl.BlockSpec((1, D_MODEL), lambda b, t, f: (0, 0)),
            pl.BlockSpec((D_MODEL, tf), lambda b, t, f: (0, f)),
            pl.BlockSpec((D_MODEL, tf), lambda b, t, f: (0, nf + f)),
            pl.BlockSpec((SUBLANES, tf), lambda b, t, f: (0, f)),
            pl.BlockSpec((1, tf), lambda b, t, f: (0, f)),
            pl.BlockSpec((1, pad, tf), lambda b, t, f: (b, 0, f)),
            pl.BlockSpec((tf, D_MODEL), lambda b, t, f: (f, 0)),
        ],
        out_specs=[
            pl.BlockSpec((tm, D_MODEL), lambda b, t, f: (b * nt + t, 0)),
            pl.BlockSpec((1, 1, pad, tf), lambda b, t, f: (b, t, 0, f)),
        ],
        out_shape=[jax.ShapeDtypeStruct((nb * nt * tm, D_MODEL), F32),
                   jax.ShapeDtypeStruct((nb, nt, pad, D_FF), F32)],
        scratch_shapes=[
            pltpu.VMEM((tm, D_MODEL), BF16),
            pltpu.VMEM((nf, pad, tf), F32),
            pltpu.VMEM((pad + tm, tf), F32),
            pltpu.VMEM((tm, D_MODEL), F32),
        ],
        compiler_params=_cparams(("parallel", "arbitrary", "arbitrary")),
        name="conv_ffn",
    )(x1, g, wup, wup, fcw, fcb, past, wd)
    return y, tail[:, tail_tile]


def _row_tile(n, cap):
    best = SUBLANES
    for cand in range(SUBLANES, cap + 1, SUBLANES):
        if n % cand == 0:
            best = cand
    return best


def kernel(x_prompt, x_sample, cache_k, cache_v, cache_idx_k, page_table, state_conv, state_ffn_conv, meta_tokens, g_attn_norm, w_in, conv_w, conv_b, conv_ln_g, conv_ln_b, w_a_out, q_norm_g, k_norm_g, rel_bias, w_b_out, w_o, g_ffn_norm, w_up, ffn_conv_w, ffn_conv_b, w_down):
    nbp, seq, _ = x_prompt.shape
    nbs, n_q, _ = x_sample.shape
    n_pages = page_table.shape[1]
    past_len = n_pages * PAGE_SIZE
    t_real = seq + N_META
    tp = _ceil_to(t_real, LANES)
    keep_p = min(TOP_K_MAX, t_real // 4)
    keep_s = min(TOP_K_MAX, (past_len + n_q) // 4)
    assert g_attn_norm.shape[0] == 1, "single trunk layer"

    w = w_in[0]
    c_q, c_v, c_qi, c_ki, c_wi, c_ga = 1024, 2048, 2560, 3072, 3136, 3144
    wm = jnp.concatenate([w[:, :c_ki], w[:, c_ga:]], axis=1).astype(BF16)
    wt = jnp.stack([w[:, c:c + D_ATTN].T for c in (c_q, c_v, c_qi)]).astype(BF16)
    ws = jnp.concatenate([w[:, c_ki:c_wi], jnp.zeros((D_MODEL, LANES - IDX_DIM), F32)], axis=1).astype(BF16)
    wst = w[:, c_wi:c_ga].T.astype(BF16)
    g_attn = g_attn_norm[0][None, :]
    gq = (jnp.tile(q_norm_g[0], N_HEADS) * ATTN_SCALE)[:, None]
    gk = jnp.tile(k_norm_g[0], N_HEADS)[None, :]
    hid = jnp.arange(D_ATTN) // HEAD_DIM
    seg = jnp.where(hid[:LANES, None] == hid[None, :LANES], 1.0 / HEAD_DIM, 0.0).astype(BF16)
    cw = jnp.concatenate([conv_w[0], jnp.zeros((32 - CONV_W, D_CONV), F32)], axis=0)
    cb, lg, lb = conv_b[0][None, :], conv_ln_g[0][None, :], conv_ln_b[0][None, :]
    wa = w_a_out[0].astype(BF16)
    wb = w_b_out[0].astype(BF16)
    wo = w_o[0].astype(BF16)
    g_ffn = g_ffn_norm[0][None, :]
    wup = w_up[0].astype(BF16)
    fcw = jnp.concatenate([ffn_conv_w[0], jnp.zeros((SUBLANES - FFN_CONV_W, D_FF), F32)], axis=0)
    fcb = ffn_conv_b[0][None, :]
    wd = w_down[0].astype(BF16)
    rbt = rel_bias.T
    ar = jnp.arange(LANES)
    tril = (ar[None, :] <= ar[:, None]).astype(BF16)
    tri = tril.T

    def project(x2d):
        return _in_proj(x2d, g_attn, wm, wt, ws, wst, gq, gk, seg, _row_tile(x2d.shape[0], 1024))

    xp = jnp.concatenate([jnp.broadcast_to(meta_tokens[None], (nbp, N_META, D_MODEL)), x_prompt,
                          jnp.zeros((nbp, tp - t_real, D_MODEL), F32)], axis=1).reshape(nbp * tp, D_MODEL)
    n_p = nbp * tp
    a_p, k_p, kb_p, v_p, qt_p, vt_p, qit_p, g2_p, ki_p, kib_p, wit_p = project(xp)

    tm_p = _row_tile(tp, 640)
    nt_p = tp // tm_p
    ya_p, ctail_p = _conv_branch(a_p, jnp.zeros((nbp, 32, D_CONV), F32), cw, cb, lg, lb, wa,
                                 nb=nbp, nt=nt_p, tm=tm_p, stride=1, rows_real=t_real)
    o_p = _prompt_attention(qt_p, kb_p, vt_p, qit_p, kib_p, wit_p, _prompt_bias(rbt), tril, keep_p, nbp, tp)
    x1_p = _merge(o_p, ya_p, g2_p, xp, wb, wo, _row_tile(n_p, 512))
    y_p, ftail_p = _ffn(x1_p, g_ffn, wup, fcw, fcb, jnp.zeros((nbp, SUBLANES, D_FF), F32), wd,
                        nb=nbp, nt=nt_p, tm=tm_p, stride=1, tf=FFN_TILE, rows_real=t_real)

    y_prompt = y_p.reshape(nbp, tp, D_MODEL)[:, N_META:t_real]
    p_k = k_p.reshape(nbp, tp, N_HEADS, HEAD_DIM)[None, :, :t_real]
    p_v = v_p.reshape(nbp, tp, N_HEADS, HEAD_DIM)[None, :, :t_real]
    p_ik = ki_p.reshape(nbp, tp, IDX_DIM)[None, :, :t_real]
    p_cv = ctail_p[None, :, 32 - (CONV_W - 1):]
    p_fc = ftail_p[None, :, SUBLANES - (FFN_CONV_W - 1):]

    n_s = nbs * n_q
    xs = x_sample.transpose(1, 0, 2).reshape(n_s, D_MODEL)
    a_s, k_s2, _, v_s2, qt_s, _, qit_s, g2_s, ki_s2, _, wit_s = project(xs)

    pad_c = _ceil_to((CONV_W - 1) * nbs, SUBLANES)
    past_c = state_conv[0].transpose(1, 0, 2).reshape(1, (CONV_W - 1) * nbs, D_CONV)
    past_c = jnp.pad(past_c, ((0, 0), (pad_c - (CONV_W - 1) * nbs, 0), (0, 0)))
    ya_s, ctail_s = _conv_branch(a_s, past_c, cw, cb, lg, lb, wa, nb=1, nt=1, tm=n_s, stride=nbs,
                                 rows_real=n_s)

    def batch_major(x2d, *tail):
        return x2d.reshape((n_q, nbs) + tail).transpose((1, 0) + tuple(range(2, 2 + len(tail))))

    q_s = batch_major(qt_s.T.astype(F32), N_HEADS, HEAD_DIM)
    k_s = batch_major(k_s2, N_HEADS, HEAD_DIM)
    v_s = batch_major(v_s2, N_HEADS, HEAD_DIM)
    qi_s = batch_major(qit_s.T, IDX_HEADS, IDX_DIM)
    ki_s = batch_major(ki_s2, IDX_DIM)
    wi_s = batch_major(wit_s.T, IDX_HEADS)
    rows = n_q * N_HEADS

    qi_rows = qi_s.reshape(nbs, rows, IDX_DIM)
    wi_rows = wi_s.reshape(nbs, rows, 1)
    ki_new = jnp.pad(ki_s, ((0, 0), (0, LANES - n_q), (0, 0))).astype(BF16)
    n_pool = cache_k.shape[1]
    ckt = cache_k.transpose(0, 1, 3, 4, 2).reshape(n_pool, D_ATTN, PAGE_SIZE)
    cvt = cache_v.transpose(0, 1, 3, 4, 2).reshape(n_pool, D_ATTN, PAGE_SIZE)
    cikt = cache_idx_k.transpose(0, 1, 3, 2).reshape(n_pool, IDX_DIM, PAGE_SIZE)
    sel = _sample_index_t(page_table, qi_rows, wi_rows, ki_new, tri, cikt, keep_s)

    eye = jnp.eye(N_HEADS, dtype=F32)
    qbd = (q_s[:, :, :, None, :] * eye[None, None, :, :, None]).reshape(nbs, rows, D_ATTN).astype(BF16)
    k_new = jnp.pad(k_s.reshape(nbs, n_q, D_ATTN), ((0, 0), (0, LANES - n_q), (0, 0))).astype(BF16)
    v_new = jnp.pad(v_s.reshape(nbs, n_q, D_ATTN), ((0, 0), (0, LANES - n_q), (0, 0))).astype(BF16)
    hsel = (jnp.arange(rows)[:, None] % N_HEADS == hid[None, :]).astype(F32)
    o_s = _sample_attention_dense(page_table, qbd, sel, k_new, v_new, _sample_bias(jnp.tile(rbt, (n_q, 1)), n_q),
                                  hsel, ckt, cvt)
    o_s2d = o_s.transpose(1, 0, 2).reshape(n_s, D_ATTN)

    x1_s = _merge(o_s2d, ya_s, g2_s, xs, wb, wo, _row_tile(n_s, 512))
    pad_f = _ceil_to((FFN_CONV_W - 1) * nbs, SUBLANES)
    past_f = state_ffn_conv[0].transpose(1, 0, 2).reshape(1, (FFN_CONV_W - 1) * nbs, D_FF)
    past_f = jnp.pad(past_f, ((0, 0), (pad_f - (FFN_CONV_W - 1) * nbs, 0), (0, 0)))
    y_s, ftail_s = _ffn(x1_s, g_ffn, wup, fcw, fcb, past_f, wd, nb=1, nt=1, tm=n_s, stride=nbs, tf=FFN_TILE,
                        rows_real=n_s)

    y_sample = y_s.reshape(n_q, nbs, D_MODEL).transpose(1, 0, 2)
    s_cv = ctail_s[0, pad_c - (CONV_W - 1) * nbs:].reshape(CONV_W - 1, nbs, D_CONV).transpose(1, 0, 2)[None]
    s_fc = ftail_s[0, pad_f - (FFN_CONV_W - 1) * nbs:].reshape(FFN_CONV_W - 1, nbs, D_FF).transpose(1, 0, 2)[None]

    return (y_prompt, y_sample, p_k, p_v, p_ik, p_cv, p_fc,
            k_s[None], v_s[None], ki_s[None], s_cv, s_fc)
```
